```python
import jax, jax.numpy as jnp
from jax import lax
import numpy as np

D_MODEL = 1024
BATCH = 8
SEQ = 2048
DEPTH = 4
DEC_BATCH = 32
DEC_SEQ = 1
PAST_LEN = 8192
PAGE_SIZE = 128

MIX_WIDTH = D_MODEL
A_HEADS = 8
A_HEAD_DIM = MIX_WIDTH // 2 // A_HEADS
A_WIDTH = A_HEADS * A_HEAD_DIM
Q_BLOCK = 128
B_GROUPS = 8
B_WIDTH = MIX_WIDTH - A_WIDTH
B_GROUP_DIM = B_WIDTH // B_GROUPS
B_CHUNK = 128
C_HEADS = 4
C_WIDTH = MIX_WIDTH
C_HEAD_DIM = C_WIDTH // C_HEADS
C_CHUNK = 128
CONV_W = 4
N_ATTN_LAYERS = (DEPTH + 1) // 2
N_MLSTM_LAYERS = DEPTH // 2
EVEN_COLS = 4 * A_WIDTH + A_HEADS + 3 * B_WIDTH
ODD_COLS = 3 * C_WIDTH + 2 * C_HEADS
EPS = 1e-6

kernel_name = "fox_gmlp_mlstm_hybrid_step"

F32 = jnp.float32


def _split(x, sizes):
    out, o = [], 0
    for s in sizes:
        out.append(x[..., o:o + s])
        o += s
    return out


def rmsnorm(x, g):
    xf = x.astype(F32)
    return (xf * lax.rsqrt(jnp.mean(xf * xf, -1, keepdims=True) + EPS) * g).astype(x.dtype)


def fox_block(q, k, v, cq, ck, qpos, kpos):
    s = jnp.einsum('bqhd,bkhd->bhqk', q, k).astype(F32) * (q.shape[-1] ** -0.5)
    s = s + cq.swapaxes(1, 2)[:, :, :, None] - ck.swapaxes(1, 2)[:, :, None, :]
    s = jnp.where(kpos[None, :] <= qpos[:, None], s, -jnp.inf)
    p = jax.nn.softmax(s, axis=-1)
    return jnp.einsum('bhqk,bkhd->bqhd', p.astype(v.dtype), v)


def fox_prompt(q, k, v, logf):
    B, S, H, Dh = q.shape
    c = jnp.cumsum(logf, axis=1)
    nb = S // Q_BLOCK
    kpos = jnp.arange(S)
    qb = q.reshape(B, nb, Q_BLOCK, H, Dh).swapaxes(0, 1)
    cb = c.reshape(B, nb, Q_BLOCK, H).swapaxes(0, 1)

    def one_block(args):
        i, qi, ci = args
        return fox_block(qi, k, v, ci, c, i * Q_BLOCK + jnp.arange(Q_BLOCK), kpos)

    out = lax.map(one_block, (jnp.arange(nb), qb, cb))
    return out.swapaxes(0, 1).reshape(B, S, H, Dh)


def fox_sample(q, k, v, logf, kc, vc, lfc, page_table):
    DB, T, H, Dh = q.shape
    past = page_table.shape[1] * kc.shape[1]
    k_all = jnp.concatenate([kc[page_table].reshape(DB, past, H, Dh).astype(k.dtype), k], 1)
    v_all = jnp.concatenate([vc[page_table].reshape(DB, past, H, Dh).astype(v.dtype), v], 1)
    lf_all = jnp.concatenate([lfc[page_table].reshape(DB, past, H).astype(F32), logf], 1)
    c = jnp.cumsum(lf_all, axis=1)
    return fox_block(q, k_all, v_all, c[:, past:], c, past + jnp.arange(T), jnp.arange(past + T))


def gmlp_mix(v, w_s, b_s):
    B, L, _ = v.shape
    n = -(-L // B_CHUNK)
    vp = jnp.pad(v, ((0, 0), (0, n * B_CHUNK - L), (0, 0)))
    vc = vp.reshape(B, n, B_CHUNK, B_GROUPS, B_GROUP_DIM)
    w = jnp.tril(w_s)
    y = jnp.einsum('gts,bnsgc->bntgc', w, vc) + b_s.T[None, None, :, :, None]
    return y.reshape(B, n * B_CHUNK, B_WIDTH)[:, :L]


def even_mixer(h, g, w_in, b_f, ln_g, ln_b, w_s, b_s, w_out, past):
    B, L, _ = h.shape
    xn = rmsnorm(h, g)
    q, k, v, fg, za, u, vb, zb = _split(xn @ w_in, (A_WIDTH, A_WIDTH, A_WIDTH, A_HEADS, A_WIDTH, B_WIDTH, B_WIDTH, B_WIDTH))
    q = q.reshape(B, L, A_HEADS, A_HEAD_DIM)
    k = k.reshape(B, L, A_HEADS, A_HEAD_DIM)
    v = v.reshape(B, L, A_HEADS, A_HEAD_DIM)
    logf = jax.nn.log_sigmoid(fg.astype(F32) + b_f)
    if past is None:
        att = fox_prompt(q, k, v, logf)
    else:
        att = fox_sample(q, k, v, logf, *past)
    ya = att.reshape(B, L, A_WIDTH) * jax.nn.silu(za)
    u = jax.nn.gelu(u, approximate=False)
    vf = jax.nn.gelu(vb, approximate=False).astype(F32)
    mu = jnp.mean(vf, -1, keepdims=True)
    var = jnp.mean((vf - mu) ** 2, -1, keepdims=True)
    vn = ((vf - mu) * lax.rsqrt(var + EPS) * ln_g + ln_b).astype(h.dtype)
    yb = u * gmlp_mix(vn, w_s, b_s) * jax.nn.silu(zb)
    y = jnp.concatenate([ya, yb], -1) @ w_out
    return (h + y).astype(h.dtype), (k, v, logf, vn)


def causal_conv(x, buf, w, b):
    L = x.shape[1]
    xp = jnp.concatenate([buf.astype(x.dtype), x], 1)
    y = b + sum(xp[:, j:j + L] * w[j] for j in range(CONV_W))
    return y, xp[:, xp.shape[1] - (CONV_W - 1):]


def mlstm_chunkwise(q, k, v, log_i, log_f, C0, n0, m0):
    B, L, H, D = q.shape
    Lc = C_CHUNK if L % C_CHUNK == 0 else L
    nc = L // Lc

    def to_chunks(a):
        return a.reshape(B, nc, Lc, *a.shape[2:]).swapaxes(0, 1)

    xs = tuple(map(to_chunks, (q.astype(F32) * (D ** -0.5), k.astype(F32), v.astype(F32), log_i, log_f)))
    causal = jnp.tril(jnp.ones((Lc, Lc), bool))

    def step(carry, inp):
        C, n, m = carry
        qc, kc, vc, ic, fc = inp
        bT = jnp.cumsum(fc, axis=1).swapaxes(1, 2)
        iT = ic.swapaxes(1, 2)
        Dm = jnp.where(causal, bT[:, :, :, None] - bT[:, :, None, :] + iT[:, :, None, :], -jnp.inf)
        prior = bT + m[:, :, None]
        mt = jnp.maximum(prior, jnp.max(Dm, -1))
        W = jnp.exp(Dm - mt[..., None])
        a = jnp.exp(prior - mt)
        Sm = W * jnp.einsum('bthd,bshd->bhts', qc, kc)
        num = jnp.einsum('bhts,bshd->bthd', Sm, vc) + a.swapaxes(1, 2)[..., None] * jnp.einsum('bthk,bhkv->bthv', qc, C)
        den = jnp.sum(Sm, -1) + a * jnp.einsum('bthk,bhk->bht', qc, n)
        den = jnp.maximum(jnp.abs(den), jnp.exp(-mt))
        hc = num / den.swapaxes(1, 2)[..., None]
        m_new = mt[:, :, -1]
        wl = jnp.exp(bT[:, :, -1:] - bT + iT - m_new[..., None])
        a_l = jnp.exp(prior[:, :, -1] - m_new)
        C_new = a_l[..., None, None] * C + jnp.einsum('bhs,bshk,bshv->bhkv', wl, kc, vc)
        n_new = a_l[..., None] * n + jnp.einsum('bhs,bshk->bhk', wl, kc)
        return (C_new, n_new, m_new), hc

    (C, n, m), hs = lax.scan(step, (C0.astype(F32), n0.astype(F32), m0.astype(F32)), xs)
    return hs.swapaxes(0, 1).reshape(B, L, H, D), C, n, m


def mlstm_mixer(h, g, w_in, b_i, b_f, cw, cb, wq, wk, wv, ng, skip, w_out, C0, n0, m0, buf):
    B, L, _ = h.shape
    xn = rmsnorm(h, g)
    xc, z, o, ig, fg = _split(xn @ w_in, (C_WIDTH, C_WIDTH, C_WIDTH, C_HEADS, C_HEADS))
    xconv, new_buf = causal_conv(xc, buf, cw, cb)
    xconv = jax.nn.silu(xconv)
    xh = xconv.reshape(B, L, C_HEADS, C_HEAD_DIM)
    q = jnp.einsum('blhd,hde->blhe', xh, wq)
    k = jnp.einsum('blhd,hde->blhe', xh, wk)
    v = jnp.einsum('blhd,hde->blhe', xc.reshape(B, L, C_HEADS, C_HEAD_DIM), wv)
    log_i = ig.astype(F32) + b_i
    log_f = jax.nn.log_sigmoid(fg.astype(F32) + b_f)
    hc, C, n, m = mlstm_chunkwise(q, k, v, log_i, log_f, C0, n0, m0)
    mu = jnp.mean(hc, -1, keepdims=True)
    var = jnp.mean((hc - mu) ** 2, -1, keepdims=True)
    hn = ((hc - mu) * lax.rsqrt(var + EPS)).reshape(B, L, C_WIDTH) * ng
    y = (jax.nn.sigmoid(o) * hn + skip * xconv) * jax.nn.silu(z)
    return (h + y @ w_out).astype(h.dtype), (C, n, m, new_buf)


def setup_inputs(seed: int = 0) -> dict:
    key = jax.random.key(seed)
    keys = iter(jax.random.split(key, 48))

    def rnd(shape, scale=1.0, shift=0.0):
        return jax.random.normal(next(keys), shape, F32) * scale + shift

    n_pages = PAST_LEN // PAGE_SIZE
    n_used = DEC_BATCH * n_pages
    n_phys = n_used + max(1, n_used // 4)
    perm = jax.random.permutation(next(keys), n_phys)
    page_table = perm[:n_used].reshape(DEC_BATCH, n_pages).astype(jnp.int32)
    LA, LC = N_ATTN_LAYERS, N_MLSTM_LAYERS
    return {
        "x_prompt": rnd((BATCH, SEQ, D_MODEL)),
        "x_sample": rnd((DEC_BATCH, DEC_SEQ, D_MODEL)),
        "cache_k": rnd((LA, n_phys, PAGE_SIZE, A_HEADS, A_HEAD_DIM)),
        "cache_v": rnd((LA, n_phys, PAGE_SIZE, A_HEADS, A_HEAD_DIM)),
        "cache_logf": jax.nn.log_sigmoid(rnd((LA, n_phys, PAGE_SIZE, A_HEADS), 1.0, 3.0)),
        "state_c": rnd((LC, DEC_BATCH, C_HEADS, C_HEAD_DIM, C_HEAD_DIM), 0.1),
        "state_n": rnd((LC, DEC_BATCH, C_HEADS, C_HEAD_DIM), 0.1),
        "state_m": rnd((LC, DEC_BATCH, C_HEADS), 0.5),
        "state_conv": rnd((LC, DEC_BATCH, CONV_W - 1, C_WIDTH)),
        "page_table": page_table,
        "norm_g": rnd((DEPTH, D_MODEL), 0.02, 1.0),
        "final_g": rnd((D_MODEL,), 0.02, 1.0),
        "even_w_in": rnd((LA, D_MODEL, EVEN_COLS), D_MODEL ** -0.5),
        "even_b_f": rnd((LA, A_HEADS), 0.5, 3.0),
        "gmlp_ln_g": rnd((LA, B_WIDTH), 0.02, 1.0),
        "gmlp_ln_b": rnd((LA, B_WIDTH), 0.02),
        "gmlp_w_s": rnd((LA, B_GROUPS, B_CHUNK, B_CHUNK), B_CHUNK ** -0.5),
        "gmlp_b_s": rnd((LA, B_GROUPS, B_CHUNK), 0.02, 1.0),
        "even_w_out": rnd((LA, MIX_WIDTH, D_MODEL), MIX_WIDTH ** -0.5),
        "odd_w_in": rnd((LC, D_MODEL, ODD_COLS), D_MODEL ** -0.5),
        "odd_b_i": rnd((LC, C_HEADS), 0.1),
        "odd_b_f": jnp.linspace(3.0, 6.0, C_HEADS, dtype=F32)[None, :] + rnd((LC, C_HEADS), 0.1),
        "conv_w": rnd((LC, CONV_W, C_WIDTH), CONV_W ** -0.5),
        "conv_b": rnd((LC, C_WIDTH), 0.02),
        "mlstm_w_q": rnd((LC, C_HEADS, C_HEAD_DIM, C_HEAD_DIM), C_HEAD_DIM ** -0.5),
        "mlstm_w_k": rnd((LC, C_HEADS, C_HEAD_DIM, C_HEAD_DIM), C_HEAD_DIM ** -0.5),
        "mlstm_w_v": rnd((LC, C_HEADS, C_HEAD_DIM, C_HEAD_DIM), C_HEAD_DIM ** -0.5),
        "mlstm_norm_g": rnd((LC, C_WIDTH), 0.02, 1.0),
        "mlstm_skip": rnd((LC, C_WIDTH), 0.02, 1.0),
        "odd_w_out": rnd((LC, C_WIDTH, D_MODEL), C_WIDTH ** -0.5),
    }


def reference(x_prompt, x_sample, cache_k, cache_v, cache_logf, state_c, state_n, state_m, state_conv, page_table,
              norm_g, final_g, even_w_in, even_b_f, gmlp_ln_g, gmlp_ln_b, gmlp_w_s, gmlp_b_s, even_w_out,
              odd_w_in, odd_b_i, odd_b_f, conv_w, conv_b, mlstm_w_q, mlstm_w_k, mlstm_w_v, mlstm_norm_g,
              mlstm_skip, odd_w_out):
    hp, hs = x_prompt, x_sample
    kp, vp, lp, ks, vs, ls, chv = [], [], [], [], [], [], []
    cp, np_, mp, bp, cs, ns, ms, bs = [], [], [], [], [], [], [], []
    for l in range(DEPTH):
        j = l // 2
        if l % 2 == 0:
            ew = (norm_g[l], even_w_in[j], even_b_f[j], gmlp_ln_g[j], gmlp_ln_b[j], gmlp_w_s[j], gmlp_b_s[j], even_w_out[j])
            hp, (k1, v1, lf1, _) = even_mixer(hp, *ew, past=None)
            hs, (k2, v2, lf2, vb2) = even_mixer(hs, *ew, past=(cache_k[j], cache_v[j], cache_logf[j], page_table))
            kp.append(k1); vp.append(v1); lp.append(lf1)
            ks.append(k2); vs.append(v2); ls.append(lf2); chv.append(vb2)
        else:
            ow = (norm_g[l], odd_w_in[j], odd_b_i[j], odd_b_f[j], conv_w[j], conv_b[j], mlstm_w_q[j], mlstm_w_k[j],
                  mlstm_w_v[j], mlstm_norm_g[j], mlstm_skip[j], odd_w_out[j])
            B = hp.shape[0]
            C0 = jnp.zeros((B, C_HEADS, C_HEAD_DIM, C_HEAD_DIM), F32)
            n0 = jnp.zeros((B, C_HEADS, C_HEAD_DIM), F32)
            m0 = jnp.zeros((B, C_HEADS), F32)
            buf0 = jnp.zeros((B, CONV_W - 1, C_WIDTH), hp.dtype)
            hp, (c1, n1, m1, b1) = mlstm_mixer(hp, *ow, C0, n0, m0, buf0)
            hs, (c2, n2, m2, b2) = mlstm_mixer(hs, *ow, state_c[j], state_n[j], state_m[j], state_conv[j])
            cp.append(c1); np_.append(n1); mp.append(m1); bp.append(b1)
            cs.append(c2); ns.append(n2); ms.append(m2); bs.append(b2)
    y_prompt = rmsnorm(hp, final_g)
    y_sample = rmsnorm(hs, final_g)
    return (y_prompt, y_sample,
            jnp.stack(kp), jnp.stack(vp), jnp.stack(lp),
            jnp.stack(ks), jnp.stack(vs), jnp.stack(ls), jnp.stack(chv),
            jnp.stack(cp), jnp.stack(np_), jnp.stack(mp), jnp.stack(bp),
            jnp.stack(cs), jnp.stack(ns), jnp.stack(ms), jnp.stack(bs))
```

```python
import functools

import jax
import jax.numpy as jnp
import numpy as np
from jax import lax
from jax.experimental import pallas as pl
from jax.experimental.pallas import tpu as pltpu

F32 = jnp.float32
BF16 = jnp.bfloat16
EPS = 1e-6
NEG_INF = float("-inf")

A_HEADS = 8
A_HEAD_DIM = 64
A_WIDTH = A_HEADS * A_HEAD_DIM
B_GROUPS = 8
B_CHUNK = 128
C_HEADS = 4
C_CHUNK = 128
CONV_W = 4
LANES = 128
GATE_ROWS = 8
VMEM_LIMIT = 56 * 1024 * 1024


def _cparams(*sem):
    return pltpu.CompilerParams(dimension_semantics=sem, vmem_limit_bytes=VMEM_LIMIT)


def _mm(a, b):
    return jnp.dot(a.astype(BF16), b.astype(BF16), preferred_element_type=F32)


def _mm_nt(a, b):
    return lax.dot_general(a.astype(BF16), b.astype(BF16), (((1,), (1,)), ((), ())),
                           preferred_element_type=F32)


def _log_sigmoid(x):
    return jnp.minimum(x, 0.0) - jnp.log1p(jnp.exp(-jnp.abs(x)))


def _sigmoid(x):
    return 1.0 / (1.0 + jnp.exp(-x))


def _silu(x):
    return x * _sigmoid(x)


def _gelu(x):
    return 0.5 * x * (1.0 + lax.erf(x * np.float32(np.sqrt(0.5))))


def _lane_cumsum(x):
    lane = lax.broadcasted_iota(jnp.int32, x.shape, 1)
    k = 1
    while k < x.shape[1]:
        x = x + jnp.where(lane >= k, pltpu.roll(x, k, 1), 0.0)
        k *= 2
    return x


def _row_to_col(r):
    n = r.shape[1]
    eye = lax.broadcasted_iota(jnp.int32, (n, n), 0) == lax.broadcasted_iota(jnp.int32, (n, n), 1)
    return jnp.sum(jnp.where(eye, r, 0.0), axis=1, keepdims=True)


def _proj_kernel(x_ref, g_ref, wnn_ref, wg_ref, gb_ref, *rest, n_nt, n_lin, col_chunk):
    wnt_refs = rest[:n_nt]
    y_ref = rest[n_nt]
    yt_refs = rest[n_nt + 1:n_nt + 1 + n_nt]
    gt_ref = rest[n_nt + 1 + n_nt]
    x = x_ref[...]
    xn = x * lax.rsqrt(jnp.mean(x * x, -1, keepdims=True) + EPS) * g_ref[...]
    xb = xn.astype(BF16)
    n_nn = y_ref.shape[1]
    for c in range(0, n_nn, col_chunk):
        y_ref[:, c:c + col_chunk] = jnp.dot(xb, wnn_ref[:, c:c + col_chunk], preferred_element_type=F32)
    for w_ref, o_ref in zip(wnt_refs, yt_refs):
        o_ref[0] = _mm_nt(w_ref[...], xb)
    gt = _mm_nt(wg_ref[...], xb) + gb_ref[...]
    row = lax.broadcasted_iota(jnp.int32, gt.shape, 0)
    gt_ref[0] = jnp.where(row >= n_lin, _log_sigmoid(gt), gt)


def _proj(x, g, wnn, wnts, wg, gb, *, n_lin, batch, seq, tm):
    T, D = x.shape
    tps = seq // tm
    n_nn = wnn.shape[1]
    n_nt = len(wnts)
    const = lambda i: (0, 0)
    tok_t = lambda i: (i // tps, 0, i % tps)
    in_specs = [pl.BlockSpec((tm, D), lambda i: (i, 0)),
                pl.BlockSpec((1, D), const),
                pl.BlockSpec((D, n_nn), const),
                pl.BlockSpec((GATE_ROWS, D), const),
                pl.BlockSpec((GATE_ROWS, 1), const)]
    in_specs += [pl.BlockSpec(w.shape, const) for w in wnts]
    out_shape = [jax.ShapeDtypeStruct((T, n_nn), F32)]
    out_specs = [pl.BlockSpec((tm, n_nn), lambda i: (i, 0))]
    for w in wnts:
        out_shape.append(jax.ShapeDtypeStruct((batch, w.shape[0], seq), F32))
        out_specs.append(pl.BlockSpec((1, w.shape[0], tm), tok_t))
    out_shape.append(jax.ShapeDtypeStruct((batch, GATE_ROWS, seq), F32))
    out_specs.append(pl.BlockSpec((1, GATE_ROWS, tm), tok_t))
    return pl.pallas_call(
        functools.partial(_proj_kernel, n_nt=n_nt, n_lin=n_lin, col_chunk=512),
        grid=(T // tm,), in_specs=in_specs, out_specs=out_specs, out_shape=out_shape,
        compiler_params=_cparams("parallel"), name="norm_proj",
    )(x, g.reshape(1, D), wnn, wg, gb, *wnts)


def _cumsum_kernel(x_ref, o_ref):
    S = x_ref.shape[2]
    carry = jnp.zeros((GATE_ROWS, 1), F32)
    for c in range(0, S, LANES):
        inc = _lane_cumsum(x_ref[0, :, c:c + LANES]) + carry
        o_ref[0, :, c:c + LANES] = inc
        carry = inc[:, LANES - 1:LANES]


def _seq_cumsum(x):
    B, R, S = x.shape
    spec = pl.BlockSpec((1, R, S), lambda b: (b, 0, 0))
    return pl.pallas_call(_cumsum_kernel, grid=(B,), in_specs=[spec], out_specs=spec,
                          out_shape=jax.ShapeDtypeStruct(x.shape, F32),
                          compiler_params=_cparams("parallel"), name="logf_cumsum")(x)


def _fox_kernel(q_ref, kt_ref, vt_ref, crow_ref, ccol_ref, o_ref, m_s, l_s, a_s, *, tq, tk):
    qi = pl.program_id(2)
    kj = pl.program_id(3)
    last = (qi * tq + tq - 1) // tk

    @pl.when(kj == 0)
    def _():
        m_s[...] = jnp.full(m_s.shape, NEG_INF, F32)
        l_s[...] = jnp.zeros(l_s.shape, F32)
        a_s[...] = jnp.zeros(a_s.shape, F32)

    @pl.when(kj <= last)
    def _():
        q2 = q_ref[...] * np.float32(A_HEAD_DIM ** -0.5)
        lane = lax.broadcasted_iota(jnp.int32, q2.shape, 1)
        kt = kt_ref[0].astype(BF16)
        vt = vt_ref[0].astype(BF16)
        cq = ccol_ref[0, 0]
        ck = crow_ref[0, 0]
        rowp = qi * tq + lax.broadcasted_iota(jnp.int32, (tq, tk), 0)
        colp = kj * tk + lax.broadcasted_iota(jnp.int32, (tq, tk), 1)
        causal = colp <= rowp
        for hh in range(2):
            sel = (lane < A_HEAD_DIM) if hh == 0 else (lane >= A_HEAD_DIM)
            qh = jnp.where(sel, q2, 0.0).astype(BF16)
            s = jnp.dot(qh, kt, preferred_element_type=F32)
            s = s + (cq[:, hh:hh + 1] - ck[hh:hh + 1, :])
            s = jnp.where(causal, s, NEG_INF)
            m_old = m_s[hh]
            m_new = jnp.maximum(m_old, jnp.max(s, -1, keepdims=True))
            alpha = jnp.exp(m_old - m_new)
            p = jnp.exp(s - m_new)
            l_s[hh] = alpha * l_s[hh] + jnp.sum(p, -1, keepdims=True)
            a_s[hh] = alpha * a_s[hh] + _mm_nt(p, vt)
            m_s[hh] = m_new

    @pl.when(kj == last)
    def _():
        lane = lax.broadcasted_iota(jnp.int32, (tq, LANES), 1)
        o_ref[...] = jnp.where(lane < A_HEAD_DIM, a_s[0] / l_s[0], a_s[1] / l_s[1])


def _fox_prompt(y, kt, vt, c, *, batch, seq, tq, tk):
    T = y.shape[0]
    pairs = A_HEADS // 2
    nq, nk = seq // tq, seq // tk
    crow = c.reshape(batch, pairs, 2, seq)
    ccol = jnp.swapaxes(crow, 2, 3)
    kv_idx = lambda b, p, i, j: (b, p, jnp.minimum(j, (i * tq + tq - 1) // tk))
    return pl.pallas_call(
        functools.partial(_fox_kernel, tq=tq, tk=tk),
        grid=(batch, pairs, nq, nk),
        in_specs=[pl.BlockSpec((tq, LANES), lambda b, p, i, j: (b * nq + i, p)),
                  pl.BlockSpec((1, LANES, tk), kv_idx),
                  pl.BlockSpec((1, LANES, tk), kv_idx),
                  pl.BlockSpec((1, 1, 2, tk),
                               lambda b, p, i, j: (b, p, 0, jnp.minimum(j, (i * tq + tq - 1) // tk))),
                  pl.BlockSpec((1, 1, tq, 2), lambda b, p, i, j: (b, p, i, 0))],
        out_specs=pl.BlockSpec((tq, LANES), lambda b, p, i, j: (b * nq + i, p)),
        out_shape=jax.ShapeDtypeStruct((T, A_WIDTH), F32),
        scratch_shapes=[pltpu.VMEM((2, tq, 1), F32), pltpu.VMEM((2, tq, 1), F32),
                        pltpu.VMEM((2, tq, LANES), F32)],
        compiler_params=_cparams("parallel", "parallel", "parallel", "arbitrary"),
        name="fox_prompt",
    )(y, kt, vt, crow, ccol)


def _fox_decode_kernel(pt_ref, q_ref, kn_ref, vn_ref, lfn_ref, *rest, pages_per_step):
    P = pages_per_step
    k_refs = rest[:P]
    v_refs = rest[P:2 * P]
    lf_refs = rest[2 * P:3 * P]
    o_ref = rest[3 * P]
    m_s, l_s, c_s, a_s = rest[3 * P + 1:]
    r = pl.program_id(1)
    scale = np.float32(A_HEAD_DIM ** -0.5)
    lane = lax.broadcasted_iota(jnp.int32, (A_HEAD_DIM, LANES), 1)

    @pl.when(r == 0)
    def _():
        for h in range(A_HEADS):
            sl = slice(h * A_HEAD_DIM, (h + 1) * A_HEAD_DIM)
            s_new = jnp.sum(q_ref[0, sl, :] * kn_ref[0, sl, :], axis=0, keepdims=True) * scale
            m_s[h] = jnp.broadcast_to(s_new, (1, LANES))
            l_s[h] = jnp.ones((1, LANES), F32)
            c_s[h] = jnp.broadcast_to(lfn_ref[0, h:h + 1, :], (1, LANES))
            a_s[h] = jnp.where(lane == 0, vn_ref[0, sl, :], 0.0)

    for i in range(P):
        lf = lf_refs[i][...]
        inc = _lane_cumsum(lf)
        tot = inc[:, LANES - 1:LANES]
        excl = tot - inc
        for h in range(A_HEADS):
            sl = slice(h * A_HEAD_DIM, (h + 1) * A_HEAD_DIM)
            qc = q_ref[0, sl, :] * scale
            s = jnp.sum(k_refs[i][h] * qc, axis=0, keepdims=True)
            s = s + (c_s[h] + excl[h:h + 1, :])
            m_old = m_s[h]
            m_new = jnp.maximum(m_old, jnp.max(s, -1, keepdims=True))
            alpha = jnp.exp(m_old - m_new)
            p = jnp.exp(s - m_new)
            l_s[h] = alpha * l_s[h] + jnp.sum(p, -1, keepdims=True)
            a_s[h] = alpha * a_s[h] + p * v_refs[i][h]
            m_s[h] = m_new
            c_s[h] = c_s[h] + tot[h:h + 1, :]

    @pl.when(r == pl.num_programs(1) - 1)
    def _():
        for h in range(A_HEADS):
            sl = slice(h * A_HEAD_DIM, (h + 1) * A_HEAD_DIM)
            o_ref[0, sl, :] = jnp.sum(a_s[h], axis=1, keepdims=True) / l_s[h][:, 0:1]


def _fox_decode(q, k_new, v_new, lf_new, cache_kt, cache_vt, cache_lft, page_table, layer, *, pages_per_step):
    DB, n_pages = page_table.shape
    P = pages_per_step
    steps = n_pages // P
    col = lambda a: a.reshape(DB, a.shape[1], 1)

    def page_idx(i):
        return lambda b, r, pt: (layer, pt[b, n_pages - 1 - (r * P + i)], 0, 0, 0)

    def lf_idx(i):
        return lambda b, r, pt: (layer, pt[b, n_pages - 1 - (r * P + i)], 0, 0)

    vec = lambda n: pl.BlockSpec((1, n, 1), lambda b, r, pt: (b, 0, 0))
    kv_block = (None, None, A_HEADS, A_HEAD_DIM, LANES)
    in_specs = [vec(A_WIDTH), vec(A_WIDTH), vec(A_WIDTH), vec(A_HEADS)]
    in_specs += [pl.BlockSpec(kv_block, page_idx(i)) for i in range(P)]
    in_specs += [pl.BlockSpec(kv_block, page_idx(i)) for i in range(P)]
    in_specs += [pl.BlockSpec((None, None, A_HEADS, LANES), lf_idx(i)) for i in range(P)]
    grid_spec = pltpu.PrefetchScalarGridSpec(
        num_scalar_prefetch=1, grid=(DB, steps), in_specs=in_specs,
        out_specs=vec(A_WIDTH),
        scratch_shapes=[pltpu.VMEM((A_HEADS, 1, LANES), F32), pltpu.VMEM((A_HEADS, 1, LANES), F32),
                        pltpu.VMEM((A_HEADS, 1, LANES), F32),
                        pltpu.VMEM((A_HEADS, A_HEAD_DIM, LANES), F32)])
    out = pl.pallas_call(
        functools.partial(_fox_decode_kernel, pages_per_step=P),
        grid_spec=grid_spec, out_shape=jax.ShapeDtypeStruct((DB, A_WIDTH, 1), F32),
        compiler_params=_cparams("parallel", "arbitrary"), name="fox_decode",
    )(page_table, col(q), col(k_new), col(v_new), col(lf_new),
      *([cache_kt] * P), *([cache_vt] * P), *([cache_lft] * P))
    return out.reshape(DB, A_WIDTH)


def _even_mix_kernel(att_ref, za_ref, u_ref, vb_ref, zb_ref, h_ref, lng_ref, lnb_ref, ws_ref, bs_ref,
                     wout_ref, *outs, decode, final_g):
    o_ref = outs[0]
    ya = att_ref[...] * _silu(za_ref[...])
    u = _gelu(u_ref[...])
    vf = _gelu(vb_ref[...])
    mu = jnp.mean(vf, -1, keepdims=True)
    var = jnp.mean((vf - mu) ** 2, -1, keepdims=True)
    vn = (vf - mu) * lax.rsqrt(var + EPS) * lng_ref[...] + lnb_ref[...]
    tm, bw = vn.shape
    if decode:
        outs[1][...] = vn
        mix = vn * ws_ref[...] + bs_ref[...]
    else:
        lane = lax.broadcasted_iota(jnp.int32, (B_CHUNK, LANES), 1)
        tri = (lax.broadcasted_iota(jnp.int32, (B_CHUNK, B_CHUNK), 0)
               >= lax.broadcasted_iota(jnp.int32, (B_CHUNK, B_CHUNK), 1))
        wtril = [jnp.where(tri, ws_ref[g], 0.0).astype(BF16) for g in range(B_GROUPS)]
        gpl = LANES // (bw // B_GROUPS)
        rows = []
        for c in range(0, tm, B_CHUNK):
            blocks = []
            for lb in range(bw // LANES):
                vblk = vn[c:c + B_CHUNK, lb * LANES:(lb + 1) * LANES].astype(BF16)
                y0 = jnp.dot(wtril[lb * gpl], vblk, preferred_element_type=F32)
                y1 = jnp.dot(wtril[lb * gpl + 1], vblk, preferred_element_type=F32)
                blocks.append(jnp.where(lane < LANES // gpl, y0, y1))
            rows.append(jnp.concatenate(blocks, axis=1) + bs_ref[...])
        mix = jnp.concatenate(rows, axis=0)
    yb = u * mix * _silu(zb_ref[...])
    aw = ya.shape[1]
    hn = h_ref[...] + _mm(ya, wout_ref[:aw, :]) + _mm(yb, wout_ref[aw:, :])
    o_ref[...] = hn


def _even_mix(att, y, h, ln_g, ln_b, ws, bs, wout, *, col0, tm, decode):
    T, D = h.shape
    aw = att.shape[1]
    const2 = lambda i: (0, 0)
    yblk = lambda k: pl.BlockSpec((tm, aw), lambda i: (i, col0 + k))
    ws_spec = (pl.BlockSpec(ws.shape, const2) if decode else pl.BlockSpec(ws.shape, lambda i: (0, 0, 0)))
    in_specs = [pl.BlockSpec((tm, aw), lambda i: (i, 0)), yblk(0), yblk(1), yblk(2), yblk(3),
                pl.BlockSpec((tm, D), lambda i: (i, 0)),
                pl.BlockSpec((1, aw), const2), pl.BlockSpec((1, aw), const2),
                ws_spec, pl.BlockSpec(bs.shape, const2), pl.BlockSpec(wout.shape, const2)]
    out_shape = [jax.ShapeDtypeStruct((T, D), F32)]
    out_specs = [pl.BlockSpec((tm, D), lambda i: (i, 0))]
    if decode:
        out_shape.append(jax.ShapeDtypeStruct((T, aw), F32))
        out_specs.append(pl.BlockSpec((tm, aw), lambda i: (i, 0)))
    return pl.pallas_call(
        functools.partial(_even_mix_kernel, decode=decode, final_g=None),
        grid=(T // tm,), in_specs=in_specs, out_specs=out_specs, out_shape=out_shape,
        compiler_params=_cparams("parallel"), name="even_mix",
    )(att, y, y, y, y, h, ln_g.reshape(1, aw), ln_b.reshape(1, aw), ws, bs, wout)


def _qkv_kernel(xc_ref, prev_ref, cw_ref, cb_ref, wq_ref, wk_ref, wv_ref,
                xconv_ref, q_ref, k_ref, v_ref, *extra, decode, tiles_per_seq):
    xc = xc_ref[...]
    tm, W = xc.shape
    hd = W // C_HEADS
    if decode:
        acc = cb_ref[...] + cw_ref[CONV_W - 1:CONV_W, :] * xc
        for j in range(CONV_W - 1):
            acc = acc + cw_ref[j:j + 1, :] * prev_ref[j]
        nb_ref = extra[0]
        for j in range(CONV_W - 2):
            nb_ref[j] = prev_ref[j + 1]
        nb_ref[CONV_W - 2] = xc
    else:
        i = pl.program_id(0)
        halo = jnp.where(i % tiles_per_seq == 0, 0.0, prev_ref[...])
        xx = jnp.concatenate([halo, xc], axis=0)
        acc = cb_ref[...] + cw_ref[CONV_W - 1:CONV_W, :] * xc
        for k in range(1, CONV_W):
            acc = acc + cw_ref[CONV_W - 1 - k:CONV_W - k, :] * pltpu.roll(xx, k, 0)[8:]
    xconv = _silu(acc)
    xconv_ref[...] = xconv
    for h in range(C_HEADS):
        sl = slice(h * hd, (h + 1) * hd)
        xh = xconv[:, sl]
        q_ref[:, sl] = _mm(xh, wq_ref[h]) * np.float32(hd ** -0.5)
        if decode:
            k_ref[:, sl] = _mm(xh, wk_ref[h])
        else:
            k_ref[0, sl, :] = _mm_nt(wk_ref[h], xh)
        v_ref[:, sl] = _mm(xc[:, sl], wv_ref[h])


def _qkv(y, prev, cw, cb, wq, wk, wv, *, batch, seq, tm, decode):
    T = y.shape[0]
    W = cw.shape[1]
    tps = seq // tm
    const2 = lambda i: (0, 0)
    const3 = lambda i: (0, 0, 0)
    row_blk = pl.BlockSpec((tm, W), lambda i: (i, 0))
    if decode:
        prev_spec = pl.BlockSpec(prev.shape, const3)
        prev_arg = prev
    else:
        prev_spec = pl.BlockSpec((8, W), lambda i: (jnp.maximum(i * (tm // 8) - 1, 0), 0))
        prev_arg = y
    in_specs = [row_blk, prev_spec, pl.BlockSpec(cw.shape, const2), pl.BlockSpec((1, W), const2),
                pl.BlockSpec(wq.shape, const3), pl.BlockSpec(wk.shape, const3), pl.BlockSpec(wv.shape, const3)]
    out_shape = [jax.ShapeDtypeStruct((T, W), F32), jax.ShapeDtypeStruct((T, W), F32)]
    out_specs = [row_blk, row_blk]
    if decode:
        out_shape.append(jax.ShapeDtypeStruct((T, W), F32))
        out_specs.append(row_blk)
    else:
        out_shape.append(jax.ShapeDtypeStruct((batch, W, seq), F32))
        out_specs.append(pl.BlockSpec((1, W, tm), lambda i: (i // tps, 0, i % tps)))
    out_shape.append(jax.ShapeDtypeStruct((T, W), F32))
    out_specs.append(row_blk)
    if decode:
        out_shape.append(jax.ShapeDtypeStruct(prev.shape, F32))
        out_specs.append(pl.BlockSpec(prev.shape, const3))
    return pl.pallas_call(
        functools.partial(_qkv_kernel, decode=decode, tiles_per_seq=tps),
        grid=(T // tm,), in_specs=in_specs, out_specs=out_specs, out_shape=out_shape,
        compiler_params=_cparams("parallel"), name="conv_qkv",
    )(y, prev_arg, cw, cb.reshape(1, W), wq, wk, wv)


def _mlstm_kernel(q_ref, kt_ref, v_ref, g_ref, o_ref, z_ref, xconv_ref, ng_ref, skip_ref, *rest,
                  decode, valid):
    if decode:
        c0_ref, n0_ref, m0_ref = rest[:3]
        rest = rest[3:]
    y_ref, c_out, n_out, m_out, caug_s, m_s = rest
    L = kt_ref.shape[2]
    W = kt_ref.shape[1]
    hd = W // C_HEADS
    ci = pl.program_id(1)
    lane_l = lax.broadcasted_iota(jnp.int32, (GATE_ROWS, L), 1)
    grow = lax.broadcasted_iota(jnp.int32, (GATE_ROWS, L), 0)

    @pl.when(ci == 0)
    def _():
        if decode:
            for h in range(C_HEADS):
                caug_s[h, :, :hd] = c0_ref[0, h]
                ncol = _row_to_col(n0_ref[0, h:h + 1, :hd // 2])
                ncol = jnp.concatenate([ncol, _row_to_col(n0_ref[0, h:h + 1, hd // 2:])], axis=0)
                lane = lax.broadcasted_iota(jnp.int32, (hd, LANES), 1)
                caug_s[h, :, hd:] = jnp.where(lane == 0, ncol, 0.0)
                m_s[h] = jnp.broadcast_to(m0_ref[0, h:h + 1, :], (8, LANES))
        else:
            caug_s[...] = jnp.zeros(caug_s.shape, F32)
            m_s[...] = jnp.zeros(m_s.shape, F32)

    if decode:
        pad = jnp.where(grow < C_HEADS, NEG_INF, 0.0)
        gates = jnp.where(lane_l < valid, jnp.broadcast_to(g_ref[0], (GATE_ROWS, L)), pad)
        rmask = lax.broadcasted_iota(jnp.int32, (L, W), 0) < valid
        expand = lambda ref: jnp.where(rmask, jnp.broadcast_to(ref[0], (L, W)), 0.0)
        q_all, v_all = expand(q_ref), expand(v_ref)
        o_all, z_all, xc_all = expand(o_ref), expand(z_ref), expand(xconv_ref)
    else:
        gates = g_ref[0]
        q_all, v_all = q_ref[...], v_ref[...]
        o_all, z_all, xc_all = o_ref[...], z_ref[...], xconv_ref[...]
    bsum = _lane_cumsum(gates)
    tri = (lax.broadcasted_iota(jnp.int32, (L, L), 0) >= lax.broadcasted_iota(jnp.int32, (L, L), 1))
    one_col = (lax.broadcasted_iota(jnp.int32, (L, LANES), 1) == 0).astype(F32)
    ys = []
    for h in range(C_HEADS):
        sl = slice(h * hd, (h + 1) * hd)
        qh, vh = q_all[:, sl], v_all[:, sl]
        kth = kt_ref[0, sl, :]
        i_row = gates[h:h + 1, :]
        b_row = bsum[C_HEADS + h:C_HEADS + h + 1, :]
        b_col = _row_to_col(b_row)
        m_prev = m_s[h][0:1, 0:1]
        prior = b_col + m_prev
        dm = jnp.where(tri, b_col - (b_row - i_row), NEG_INF)
        mt = jnp.maximum(prior, jnp.max(dm, -1, keepdims=True))
        wmat = jnp.exp(dm - mt)
        a = jnp.exp(prior - mt)
        sm = wmat * _mm(qh, kth)
        caug = caug_s[h]
        qc = _mm(qh, caug)
        num = _mm(sm, vh) + a * qc[:, :hd]
        den = jnp.sum(sm, -1, keepdims=True) + a * qc[:, hd:hd + 1]
        den = jnp.maximum(jnp.abs(den), jnp.exp(-mt))
        hc = num / den
        m_new = mt[L - 1:L, :]
        b_last = b_row[:, L - 1:L]
        wl = jnp.exp(b_last - b_row + i_row - m_new)
        a_l = jnp.exp(b_last + m_prev - m_new)
        vaug = jnp.concatenate([vh, one_col], axis=1)
        caug_s[h] = a_l * caug + _mm(kth * wl, vaug)
        m_s[h] = jnp.broadcast_to(m_new, (8, LANES))
        mu = jnp.mean(hc, -1, keepdims=True)
        var = jnp.mean((hc - mu) ** 2, -1, keepdims=True)
        hn = (hc - mu) * lax.rsqrt(var + EPS) * ng_ref[:, sl]
        ys.append((_sigmoid(o_all[:, sl]) * hn + skip_ref[:, sl] * xc_all[:, sl]) * _silu(z_all[:, sl]))
    y = jnp.concatenate(ys, axis=1)
    if decode:
        y_ref[0] = y[0:valid]
    else:
        y_ref[...] = y

    @pl.when(ci == pl.num_programs(1) - 1)
    def _():
        for h in range(C_HEADS):
            c_out[0, h] = caug_s[h, :, :hd]
            n_out[0, h] = caug_s[h, :, hd:hd + 1]
            m_out[0, h:h + 1, :] = m_s[h][0:1, 0:1]


def _mlstm(q, kt, v, gates, y_in, xconv, ng, skip, state, *, batch, seq, decode):
    W = kt.shape[1]
    hd = W // C_HEADS
    L = C_CHUNK
    nc = 1 if decode else seq // L
    const2 = lambda b, c: (0, 0)
    if decode:
        rowblk = lambda k: pl.BlockSpec((1, 1, W), lambda b, c: (b, 0, k))
        gspec = pl.BlockSpec((1, GATE_ROWS, 1), lambda b, c: (b, 0, 0))
        y_shape = jax.ShapeDtypeStruct((batch, 1, W), F32)
        y_spec = pl.BlockSpec((1, 1, W), lambda b, c: (b, 0, 0))
    else:
        rowblk = lambda k: pl.BlockSpec((L, W), lambda b, c: (b * nc + c, k))
        gspec = pl.BlockSpec((1, GATE_ROWS, L), lambda b, c: (b, 0, c))
        y_shape = jax.ShapeDtypeStruct((batch * seq, W), F32)
        y_spec = pl.BlockSpec((L, W), lambda b, c: (b * nc + c, 0))
    in_specs = [rowblk(0), pl.BlockSpec((1, W, L), lambda b, c: (b, 0, c)), rowblk(0), gspec,
                rowblk(2), rowblk(1), rowblk(0),
                pl.BlockSpec((1, W), const2), pl.BlockSpec((1, W), const2)]
    args = [q, kt, v, gates, y_in, y_in, xconv, ng.reshape(1, W), skip.reshape(1, W)]
    if decode:
        c0, n0, m0 = state
        in_specs += [pl.BlockSpec((1, C_HEADS, hd, hd), lambda b, c: (b, 0, 0, 0)),
                     pl.BlockSpec((1, C_HEADS, hd), lambda b, c: (b, 0, 0)),
                     pl.BlockSpec((1, C_HEADS, 1), lambda b, c: (b, 0, 0))]
        args += [c0, n0, m0.reshape(batch, C_HEADS, 1)]
    out_shape = [y_shape,
                 jax.ShapeDtypeStruct((batch, C_HEADS, hd, hd), F32),
                 jax.ShapeDtypeStruct((batch, C_HEADS, hd, 1), F32),
                 jax.ShapeDtypeStruct((batch, C_HEADS, 1), F32)]
    out_specs = [y_spec,
                 pl.BlockSpec((1, C_HEADS, hd, hd), lambda b, c: (b, 0, 0, 0)),
                 pl.BlockSpec((1, C_HEADS, hd, 1), lambda b, c: (b, 0, 0, 0)),
                 pl.BlockSpec((1, C_HEADS, 1), lambda b, c: (b, 0, 0))]
    y, c_new, n_new, m_new = pl.pallas_call(
        functools.partial(_mlstm_kernel, decode=decode, valid=1),
        grid=(batch, nc), in_specs=in_specs, out_specs=out_specs, out_shape=out_shape,
        scratch_shapes=[pltpu.VMEM((C_HEADS, hd, hd + LANES), F32), pltpu.VMEM((C_HEADS, 8, LANES), F32)],
        compiler_params=_cparams("parallel", "arbitrary"), name="mlstm_chunk",
    )(*args)
    return y, c_new, n_new.reshape(batch, C_HEADS, hd), m_new.reshape(batch, C_HEADS)


def _out_proj_kernel(y_ref, w_ref, h_ref, *rest, final):
    hn = h_ref[...] + _mm(y_ref[...], w_ref[...])
    if final:
        fg_ref, o_ref = rest
        o_ref[...] = hn * lax.rsqrt(jnp.mean(hn * hn, -1, keepdims=True) + EPS) * fg_ref[...]
    else:
        rest[0][...] = hn


def _out_proj(y, w, h, final_g, *, tm):
    T, D = h.shape
    K = y.shape[1]
    const2 = lambda i: (0, 0)
    in_specs = [pl.BlockSpec((tm, K), lambda i: (i, 0)), pl.BlockSpec(w.shape, const2),
                pl.BlockSpec((tm, D), lambda i: (i, 0))]
    args = [y, w, h]
    if final_g is not None:
        in_specs.append(pl.BlockSpec((1, D), const2))
        args.append(final_g.reshape(1, D))
    return pl.pallas_call(
        functools.partial(_out_proj_kernel, final=final_g is not None),
        grid=(T // tm,), in_specs=in_specs, out_specs=pl.BlockSpec((tm, D), lambda i: (i, 0)),
        out_shape=jax.ShapeDtypeStruct((T, D), F32),
        compiler_params=_cparams("parallel"), name="out_proj",
    )(*args)


def _pad_rows(a, rows):
    return jnp.pad(a, ((0, rows - a.shape[0]), (0, 0)))


def kernel(x_prompt, x_sample, cache_k, cache_v, cache_logf, state_c, state_n, state_m, state_conv, page_table,
           norm_g, final_g, even_w_in, even_b_f, gmlp_ln_g, gmlp_ln_b, gmlp_w_s, gmlp_b_s, even_w_out,
           odd_w_in, odd_b_i, odd_b_f, conv_w, conv_b, mlstm_w_q, mlstm_w_k, mlstm_w_v, mlstm_norm_g,
           mlstm_skip, odd_w_out):
    B, S, D = x_prompt.shape
    DB = x_sample.shape[0]
    depth = norm_g.shape[0]
    AW = A_WIDTH
    BW = gmlp_ln_g.shape[1]
    CW = conv_w.shape[2]
    gdim = BW // B_GROUPS
    tm = 512

    hp = x_prompt.reshape(B * S, D)
    hs = x_sample.reshape(DB, D)
    cache_kt = jnp.transpose(cache_k, (0, 1, 3, 4, 2))
    cache_vt = jnp.transpose(cache_v, (0, 1, 3, 4, 2))
    cache_lft = jnp.transpose(cache_logf, (0, 1, 3, 2))

    kp, vp, lp, ks, vs, ls, chv = [], [], [], [], [], [], []
    cp, np_, mp, bp, cs, ns, ms, bs = [], [], [], [], [], [], [], []
    for l in range(depth):
        j = l // 2
        last = l == depth - 1
        if l % 2 == 0:
            w = even_w_in[j]
            o = 0
            parts = {}
            for name, n in (("q", AW), ("k", AW), ("v", AW), ("fg", A_HEADS), ("za", AW), ("u", BW), ("vb", BW), ("zb", BW)):
                parts[name] = w[:, o:o + n]
                o += n
            wnn_p = jnp.concatenate([parts[n] for n in ("q", "za", "u", "vb", "zb")], 1).astype(BF16)
            wnn_s = jnp.concatenate([parts[n] for n in ("q", "za", "u", "vb", "zb", "k", "v")], 1).astype(BF16)
            wkt = parts["k"].T.astype(BF16)
            wvt = parts["v"].T.astype(BF16)
            wg = parts["fg"].T.astype(BF16)
            gb = even_b_f[j].reshape(GATE_ROWS, 1)
            wout = even_w_out[j].astype(BF16)
            bs_full = jnp.repeat(gmlp_b_s[j].T, gdim, axis=1)
            ws_dec = jnp.repeat(gmlp_w_s[j][:, 0, 0], gdim).reshape(1, BW)
            bs_dec = jnp.repeat(gmlp_b_s[j][:, 0], gdim).reshape(1, BW)

            y, kt, vt, lf = _proj(hp, norm_g[l], wnn_p, [wkt, wvt], wg, gb, n_lin=0, batch=B, seq=S, tm=tm)
            c = _seq_cumsum(lf)
            att = _fox_prompt(y, kt, vt, c, batch=B, seq=S, tq=512, tk=512)
            hp = _even_mix(att, y, hp, gmlp_ln_g[j], gmlp_ln_b[j], gmlp_w_s[j], bs_full, wout,
                           col0=1, tm=tm, decode=False)[0]
            kp.append(jnp.transpose(kt.reshape(B, A_HEADS, A_HEAD_DIM, S), (0, 3, 1, 2)))
            vp.append(jnp.transpose(vt.reshape(B, A_HEADS, A_HEAD_DIM, S), (0, 3, 1, 2)))
            lp.append(jnp.transpose(lf, (0, 2, 1)))

            y, lf = _proj(hs, norm_g[l], wnn_s, [], wg, gb, n_lin=0, batch=1, seq=DB, tm=DB)
            lf_s = lf[0].T
            q_s, k_s, v_s = y[:, :AW], y[:, 5 * AW:6 * AW], y[:, 6 * AW:7 * AW]
            att = _fox_decode(q_s, k_s, v_s, lf_s, cache_kt, cache_vt, cache_lft, page_table, j,
                              pages_per_step=8)
            hs, vn_s = _even_mix(att, y, hs, gmlp_ln_g[j], gmlp_ln_b[j], ws_dec, bs_dec, wout,
                                 col0=1, tm=DB, decode=True)
            ks.append(k_s.reshape(DB, 1, A_HEADS, A_HEAD_DIM))
            vs.append(v_s.reshape(DB, 1, A_HEADS, A_HEAD_DIM))
            ls.append(lf_s.reshape(DB, 1, A_HEADS))
            chv.append(vn_s.reshape(DB, 1, BW))
        else:
            w = odd_w_in[j]
            wnn = w[:, :3 * CW].astype(BF16)
            wg = w[:, 3 * CW:].T.astype(BF16)
            gb = jnp.concatenate([odd_b_i[j], odd_b_f[j]]).reshape(GATE_ROWS, 1)
            wq = mlstm_w_q[j].astype(BF16)
            wk = mlstm_w_k[j].astype(BF16)
            wkT = jnp.swapaxes(mlstm_w_k[j], 1, 2).astype(BF16)
            wv = mlstm_w_v[j].astype(BF16)
            wout = odd_w_out[j].astype(BF16)
            fin = final_g if last else None

            y, g = _proj(hp, norm_g[l], wnn, [], wg, gb, n_lin=C_HEADS, batch=B, seq=S, tm=tm)
            xconv, q, kt, v = _qkv(y, None, conv_w[j], conv_b[j], wq, wkT, wv, batch=B, seq=S, tm=tm, decode=False)
            ym, c1, n1, m1 = _mlstm(q, kt, v, g, y, xconv, mlstm_norm_g[j], mlstm_skip[j], None,
                                    batch=B, seq=S, decode=False)
            hp = _out_proj(ym, wout, hp, fin, tm=tm)
            cp.append(c1); np_.append(n1); mp.append(m1)
            bp.append(y.reshape(B, S, 3 * CW)[:, S - (CONV_W - 1):, :CW])

            y, g = _proj(hs, norm_g[l], wnn, [], wg, gb, n_lin=C_HEADS, batch=1, seq=DB, tm=DB)
            buf = jnp.transpose(state_conv[j], (1, 0, 2))
            xconv, q, k, v, nbuf = _qkv(y, buf, conv_w[j], conv_b[j], wq, wk, wv, batch=1, seq=DB, tm=DB, decode=True)
            kt = jnp.pad(k[:, :, None], ((0, 0), (0, 0), (0, C_CHUNK - 1)))
            g_s = jnp.transpose(g, (2, 1, 0))
            r3 = lambda a: a.reshape(DB, 1, a.shape[1])
            ym, c2, n2, m2 = _mlstm(r3(q), kt, r3(v), g_s, r3(y), r3(xconv), mlstm_norm_g[j], mlstm_skip[j],
                                    (state_c[j], state_n[j], state_m[j]), batch=DB, seq=1, decode=True)
            hs = _out_proj(ym.reshape(DB, CW), wout, hs, fin, tm=DB)
            cs.append(c2); ns.append(n2); ms.append(m2)
            bs.append(jnp.transpose(nbuf, (1, 0, 2)))
    y_prompt = hp.reshape(B, S, D)
    y_sample = hs.reshape(DB, 1, D)
    return (y_prompt, y_sample,
            jnp.stack(kp), jnp.stack(vp), jnp.stack(lp),
            jnp.stack(ks), jnp.stack(vs), jnp.stack(ls), jnp.stack(chv),
            jnp.stack(cp), jnp.stack(np_), jnp.stack(mp), jnp.stack(bp),
            jnp.stack(cs), jnp.stack(ns), jnp.stack(ms), jnp.stack(bs))
```

```python
import functools

import jax
import jax.numpy as jnp
import numpy as np
from jax import lax
from jax.experimental import pallas as pl
from jax.experimental.pallas import tpu as pltpu

F32 = jnp.float32
BF16 = jnp.bfloat16
EPS = 1e-6
NEG_INF = float("-inf")

A_HEADS = 8
A_HEAD_DIM = 64
A_WIDTH = A_HEADS * A_HEAD_DIM
B_GROUPS = 8
B_CHUNK = 128
C_HEADS = 4
C_CHUNK = 128
CONV_W = 4
LANES = 128
GATE_ROWS = 8
VMEM_LIMIT = 56 * 1024 * 1024


def _cparams(*sem):
    return pltpu.CompilerParams(dimension_semantics=sem, vmem_limit_bytes=VMEM_LIMIT)


def _mm(a, b):
    return jnp.dot(a.astype(BF16), b.astype(BF16), preferred_element_type=F32)


def _mm_nt(a, b):
    return lax.dot_general(a.astype(BF16), b.astype(BF16), (((1,), (1,)), ((), ())),
                           preferred_element_type=F32)


def _log_sigmoid(x):
    return jnp.minimum(x, 0.0) - jnp.log1p(jnp.exp(-jnp.abs(x)))


def _sigmoid(x):
    return 1.0 / (1.0 + jnp.exp(-x))


def _silu(x):
    return x * _sigmoid(x)


def _gelu(x):
    return 0.5 * x * (1.0 + lax.erf(x * np.float32(np.sqrt(0.5))))


def _lane_cumsum(x):
    lane = lax.broadcasted_iota(jnp.int32, x.shape, 1)
    k = 1
    while k < x.shape[1]:
        x = x + jnp.where(lane >= k, pltpu.roll(x, k, 1), 0.0)
        k *= 2
    return x


def _row_to_col(r):
    n = r.shape[1]
    eye = lax.broadcasted_iota(jnp.int32, (n, n), 0) == lax.broadcasted_iota(jnp.int32, (n, n), 1)
    return jnp.sum(jnp.where(eye, r, 0.0), axis=1, keepdims=True)


def _proj_kernel(x_ref, g_ref, wnn_ref, wg_ref, gb_ref, *rest, n_nt, n_lin, col_chunk):
    wnt_refs = rest[:n_nt]
    y_ref = rest[n_nt]
    yt_refs = rest[n_nt + 1:n_nt + 1 + n_nt]
    gt_ref = rest[n_nt + 1 + n_nt]
    x = x_ref[...]
    xn = x * lax.rsqrt(jnp.mean(x * x, -1, keepdims=True) + EPS) * g_ref[...]
    xb = xn.astype(BF16)
    n_nn = y_ref.shape[1]
    for c in range(0, n_nn, col_chunk):
        y_ref[:, c:c + col_chunk] = jnp.dot(xb, wnn_ref[:, c:c + col_chunk], preferred_element_type=F32)
    for w_ref, o_ref in zip(wnt_refs, yt_refs):
        o_ref[0] = _mm_nt(w_ref[...], xb)
    gt = _mm_nt(wg_ref[...], xb) + gb_ref[...]
    row = lax.broadcasted_iota(jnp.int32, gt.shape, 0)
    gt_ref[0] = jnp.where(row >= n_lin, _log_sigmoid(gt), gt)


def _proj(x, g, wnn, wnts, wg, gb, *, n_lin, batch, seq, tm):
    T, D = x.shape
    tps = seq // tm
    n_nn = wnn.shape[1]
    n_nt = len(wnts)
    const = lambda i: (0, 0)
    tok_t = lambda i: (i // tps, 0, i % tps)
    in_specs = [pl.BlockSpec((tm, D), lambda i: (i, 0)),
                pl.BlockSpec((1, D), const),
                pl.BlockSpec((D, n_nn), const),
                pl.BlockSpec((GATE_ROWS, D), const),
                pl.BlockSpec((GATE_ROWS, 1), const)]
    in_specs += [pl.BlockSpec(w.shape, const) for w in wnts]
    out_shape = [jax.ShapeDtypeStruct((T, n_nn), F32)]
    out_specs = [pl.BlockSpec((tm, n_nn), lambda i: (i, 0))]
    for w in wnts:
        out_shape.append(jax.ShapeDtypeStruct((batch, w.shape[0], seq), F32))
        out_specs.append(pl.BlockSpec((1, w.shape[0], tm), tok_t))
    out_shape.append(jax.ShapeDtypeStruct((batch, GATE_ROWS, seq), F32))
    out_specs.append(pl.BlockSpec((1, GATE_ROWS, tm), tok_t))
    return pl.pallas_call(
        functools.partial(_proj_kernel, n_nt=n_nt, n_lin=n_lin, col_chunk=512),
        grid=(T // tm,), in_specs=in_specs, out_specs=out_specs, out_shape=out_shape,
        compiler_params=_cparams("parallel"), name="norm_proj",
    )(x, g.reshape(1, D), wnn, wg, gb, *wnts)


def _cumsum_kernel(x_ref, o_ref):
    S = x_ref.shape[2]
    carry = jnp.zeros((GATE_ROWS, 1), F32)
    for c in range(0, S, LANES):
        inc = _lane_cumsum(x_ref[0, :, c:c + LANES]) + carry
        o_ref[0, :, c:c + LANES] = inc
        carry = inc[:, LANES - 1:LANES]


def _seq_cumsum(x):
    B, R, S = x.shape
    spec = pl.BlockSpec((1, R, S), lambda b: (b, 0, 0))
    return pl.pallas_call(_cumsum_kernel, grid=(B,), in_specs=[spec], out_specs=spec,
                          out_shape=jax.ShapeDtypeStruct(x.shape, F32),
                          compiler_params=_cparams("parallel"), name="logf_cumsum")(x)


def _fox_kernel(qi_ref, kj_ref, qt_ref, k_ref, vt_ref, crow_ref, ccol_ref, o_ref, q_s, m_s, l_s, a_s, *, tile):
    qi = qi_ref[pl.program_id(2)]
    kj = kj_ref[pl.program_id(2)]
    hd = A_HEAD_DIM

    @pl.when(kj == 0)
    def _():
        qt = qt_ref[0] * np.float32(hd ** -0.5)
        row = lax.broadcasted_iota(jnp.int32, qt.shape, 0)
        q_s[0] = jnp.where(row < hd, qt, 0.0).astype(BF16)
        q_s[1] = jnp.where(row >= hd, qt, 0.0).astype(BF16)
        m_s[...] = jnp.full(m_s.shape, NEG_INF, F32)
        l_s[...] = jnp.zeros(l_s.shape, F32)
        a_s[...] = jnp.zeros(a_s.shape, F32)

    def step(diagonal):
        kb = k_ref[...].astype(BF16)
        vt = vt_ref[0].astype(BF16)
        cq = crow_ref[0, 0]
        ck = ccol_ref[0, 0]
        if diagonal:
            causal = (lax.broadcasted_iota(jnp.int32, (tile, tile), 0)
                      <= lax.broadcasted_iota(jnp.int32, (tile, tile), 1))
        for hh in range(2):
            s = jnp.dot(kb, q_s[hh], preferred_element_type=F32)
            s = s + cq[hh:hh + 1, :] - ck[:, hh:hh + 1]
            if diagonal:
                s = jnp.where(causal, s, NEG_INF)
            m_old = m_s[hh]
            m_new = jnp.maximum(m_old, jnp.max(s, 0, keepdims=True))
            alpha = jnp.exp(m_old - m_new)
            p = jnp.exp(s - m_new)
            l_s[hh] = alpha * l_s[hh] + jnp.sum(p, 0, keepdims=True)
            rows = slice(hh * hd, (hh + 1) * hd)
            a_s[rows, :] = alpha * a_s[rows, :] + jnp.dot(vt[rows, :], p.astype(BF16),
                                                          preferred_element_type=F32)
            m_s[hh] = m_new

    @pl.when(kj < qi)
    def _():
        step(False)

    @pl.when(kj == qi)
    def _():
        step(True)
        row = lax.broadcasted_iota(jnp.int32, (2 * hd, tile), 0)
        out_t = a_s[...] / jnp.where(row < hd, l_s[0], l_s[1])
        o_ref[...] = out_t.T


def _fox_prompt(y, qt, vt, c, *, k_col, batch, seq, tile):
    T = y.shape[0]
    pairs = A_HEADS // 2
    nt = seq // tile
    kb0 = k_col // LANES
    crow = c.reshape(batch, pairs, 2, seq)
    ccol = jnp.swapaxes(crow, 2, 3)
    tri = [(i, j) for i in range(nt) for j in range(i + 1)]
    qi_tab = jnp.asarray([i for i, _ in tri], jnp.int32)
    kj_tab = jnp.asarray([j for _, j in tri], jnp.int32)
    grid_spec = pltpu.PrefetchScalarGridSpec(
        num_scalar_prefetch=2, grid=(batch, pairs, len(tri)),
        in_specs=[pl.BlockSpec((1, LANES, tile), lambda b, p, t, qi, kj: (b, p, qi[t])),
                  pl.BlockSpec((tile, LANES), lambda b, p, t, qi, kj: (b * nt + kj[t], kb0 + p)),
                  pl.BlockSpec((1, LANES, tile), lambda b, p, t, qi, kj: (b, p, kj[t])),
                  pl.BlockSpec((1, 1, 2, tile), lambda b, p, t, qi, kj: (b, p, 0, qi[t])),
                  pl.BlockSpec((1, 1, tile, 2), lambda b, p, t, qi, kj: (b, p, kj[t], 0))],
        out_specs=pl.BlockSpec((tile, LANES), lambda b, p, t, qi, kj: (b * nt + qi[t], p)),
        scratch_shapes=[pltpu.VMEM((2, LANES, tile), BF16), pltpu.VMEM((2, 1, tile), F32),
                        pltpu.VMEM((2, 1, tile), F32), pltpu.VMEM((LANES, tile), F32)])
    return pl.pallas_call(
        functools.partial(_fox_kernel, tile=tile),
        grid_spec=grid_spec, out_shape=jax.ShapeDtypeStruct((T, A_WIDTH), F32),
        compiler_params=_cparams("parallel", "parallel", "arbitrary"),
        name="fox_prompt",
    )(qi_tab, kj_tab, qt, y, vt, crow, ccol)


def _fox_decode_kernel(pt_ref, q_ref, kn_ref, vn_ref, lfn_ref, *rest, pages_per_step):
    P = pages_per_step
    k_refs = rest[:P]
    v_refs = rest[P:2 * P]
    lf_refs = rest[2 * P:3 * P]
    o_ref = rest[3 * P]
    m_s, l_s, c_s, a_s, q_s = rest[3 * P + 1:]
    r = pl.program_id(1)
    H, hd = A_HEADS, A_HEAD_DIM
    hsl = [slice(h * hd, (h + 1) * hd) for h in range(H)]

    @pl.when(r == 0)
    def _():
        qcols = [q_ref[0, hsl[h], :] * np.float32(hd ** -0.5) for h in range(H)]
        for h in range(H):
            q_s[h] = jnp.broadcast_to(qcols[h], (hd, LANES))
        s_new = jnp.concatenate(
            [jnp.sum(qcols[h] * kn_ref[0, hsl[h], :], axis=0, keepdims=True) for h in range(H)], axis=0)
        m_s[...] = s_new
        l_s[...] = jnp.ones(l_s.shape, F32)
        c_s[...] = lfn_ref[0]
        lane = lax.broadcasted_iota(jnp.int32, (hd, LANES), 1)
        for h in range(H):
            a_s[h] = jnp.where(lane == 0, vn_ref[0, hsl[h], :], 0.0)

    c = c_s[...]
    lf_all = jnp.concatenate([lf_refs[i][...] for i in range(P)], axis=0)
    later = (lax.broadcasted_iota(jnp.int32, (LANES, LANES), 0)
             > lax.broadcasted_iota(jnp.int32, (LANES, LANES), 1)).astype(F32)
    excl_all = jnp.dot(lf_all, later, precision=lax.Precision.HIGHEST, preferred_element_type=F32)
    tot_all = jnp.sum(lf_all, -1, keepdims=True)
    s_pages = []
    for i in range(P):
        rows = [jnp.sum(k_refs[i][h] * q_s[h], axis=0, keepdims=True) for h in range(H)]
        s_pages.append(jnp.concatenate(rows, axis=0) + (c + excl_all[i * H:(i + 1) * H, :]))
        c = c + tot_all[i * H:(i + 1) * H, :]
    c_s[...] = c
    s_all = jnp.concatenate(s_pages, axis=1)
    m_old = m_s[...]
    m_new = jnp.maximum(m_old, jnp.max(s_all, -1, keepdims=True))
    alpha = jnp.exp(m_old - m_new)
    p_all = jnp.exp(s_all - m_new)
    l_s[...] = alpha * l_s[...] + jnp.sum(p_all, -1, keepdims=True)
    m_s[...] = m_new
    for h in range(H):
        acc = a_s[h] * alpha[h:h + 1, :]
        for i in range(P):
            acc = acc + p_all[h:h + 1, i * LANES:(i + 1) * LANES] * v_refs[i][h]
        a_s[h] = acc

    @pl.when(r == pl.num_programs(1) - 1)
    def _():
        l = l_s[...]
        for h in range(H):
            o_ref[0, hsl[h], :] = jnp.sum(a_s[h], axis=1, keepdims=True) / l[h:h + 1, :]


def _fox_decode(q, k_new, v_new, lf_new, cache_kt, cache_vt, cache_lft, page_table, layer, *, pages_per_step):
    DB, n_pages = page_table.shape
    P = pages_per_step
    steps = n_pages // P
    col = lambda a: a.reshape(DB, a.shape[1], 1)

    def page_idx(i):
        return lambda b, r, pt: (layer, pt[b, n_pages - 1 - (r * P + i)], 0, 0, 0)

    def lf_idx(i):
        return lambda b, r, pt: (layer, pt[b, n_pages - 1 - (r * P + i)], 0, 0)

    vec = lambda n: pl.BlockSpec((1, n, 1), lambda b, r, pt: (b, 0, 0))
    kv_block = (None, None, A_HEADS, A_HEAD_DIM, LANES)
    in_specs = [vec(A_WIDTH), vec(A_WIDTH), vec(A_WIDTH), vec(A_HEADS)]
    in_specs += [pl.BlockSpec(kv_block, page_idx(i)) for i in range(P)]
    in_specs += [pl.BlockSpec(kv_block, page_idx(i)) for i in range(P)]
    in_specs += [pl.BlockSpec((None, None, A_HEADS, LANES), lf_idx(i)) for i in range(P)]
    grid_spec = pltpu.PrefetchScalarGridSpec(
        num_scalar_prefetch=1, grid=(DB, steps), in_specs=in_specs,
        out_specs=vec(A_WIDTH),
        scratch_shapes=[pltpu.VMEM((A_HEADS, 1), F32), pltpu.VMEM((A_HEADS, 1), F32),
                        pltpu.VMEM((A_HEADS, 1), F32),
                        pltpu.VMEM((A_HEADS, A_HEAD_DIM, LANES), F32),
                        pltpu.VMEM((A_HEADS, A_HEAD_DIM, LANES), F32)])
    out = pl.pallas_call(
        functools.partial(_fox_decode_kernel, pages_per_step=P),
        grid_spec=grid_spec, out_shape=jax.ShapeDtypeStruct((DB, A_WIDTH, 1), F32),
        compiler_params=_cparams("parallel", "arbitrary"), name="fox_decode",
    )(page_table, col(q), col(k_new), col(v_new), col(lf_new),
      *([cache_kt] * P), *([cache_vt] * P), *([cache_lft] * P))
    return out.reshape(DB, A_WIDTH)


def _even_mix_kernel(att_ref, za_ref, u_ref, vb_ref, zb_ref, h_ref, lng_ref, lnb_ref, ws_ref, bs_ref,
                     wout_ref, *outs, decode):
    o_ref = outs[0]
    ya = att_ref[...] * _silu(za_ref[...])
    u = _gelu(u_ref[...])
    vf = _gelu(vb_ref[...])
    mu = jnp.mean(vf, -1, keepdims=True)
    var = jnp.mean((vf - mu) ** 2, -1, keepdims=True)
    vn = (vf - mu) * lax.rsqrt(var + EPS) * lng_ref[...] + lnb_ref[...]
    tm, bw = vn.shape
    if decode:
        outs[1][...] = vn
        mix = vn * ws_ref[...] + bs_ref[...]
    else:
        lane = lax.broadcasted_iota(jnp.int32, (B_CHUNK, LANES), 1)
        tri = (lax.broadcasted_iota(jnp.int32, (B_CHUNK, B_CHUNK), 0)
               >= lax.broadcasted_iota(jnp.int32, (B_CHUNK, B_CHUNK), 1))
        wtril = [jnp.where(tri, ws_ref[g], 0.0).astype(BF16) for g in range(B_GROUPS)]
        gpl = LANES // (bw // B_GROUPS)
        rows = []
        for c in range(0, tm, B_CHUNK):
            blocks = []
            for lb in range(bw // LANES):
                vblk = vn[c:c + B_CHUNK, lb * LANES:(lb + 1) * LANES].astype(BF16)
                y0 = jnp.dot(wtril[lb * gpl], vblk, preferred_element_type=F32)
                y1 = jnp.dot(wtril[lb * gpl + 1], vblk, preferred_element_type=F32)
                blocks.append(jnp.where(lane < LANES // gpl, y0, y1))
            rows.append(jnp.concatenate(blocks, axis=1) + bs_ref[...])
        mix = jnp.concatenate(rows, axis=0)
    yb = u * mix * _silu(zb_ref[...])
    aw = ya.shape[1]
    hn = h_ref[...] + _mm(ya, wout_ref[:aw, :]) + _mm(yb, wout_ref[aw:, :])
    o_ref[...] = hn


def _even_mix(att, y, h, ln_g, ln_b, ws, bs, wout, *, col0, tm, decode):
    T, D = h.shape
    aw = att.shape[1]
    const2 = lambda i: (0, 0)
    yblk = lambda k: pl.BlockSpec((tm, aw), lambda i: (i, col0 + k))
    ws_spec = (pl.BlockSpec(ws.shape, const2) if decode else pl.BlockSpec(ws.shape, lambda i: (0, 0, 0)))
    in_specs = [pl.BlockSpec((tm, aw), lambda i: (i, 0)), yblk(0), yblk(1), yblk(2), yblk(3),
                pl.BlockSpec((tm, D), lambda i: (i, 0)),
                pl.BlockSpec((1, aw), const2), pl.BlockSpec((1, aw), const2),
                ws_spec, pl.BlockSpec(bs.shape, const2), pl.BlockSpec(wout.shape, const2)]
    out_shape = [jax.ShapeDtypeStruct((T, D), F32)]
    out_specs = [pl.BlockSpec((tm, D), lambda i: (i, 0))]
    if decode:
        out_shape.append(jax.ShapeDtypeStruct((T, aw), F32))
        out_specs.append(pl.BlockSpec((tm, aw), lambda i: (i, 0)))
    return pl.pallas_call(
        functools.partial(_even_mix_kernel, decode=decode),
        grid=(T // tm,), in_specs=in_specs, out_specs=out_specs, out_shape=out_shape,
        compiler_params=_cparams("parallel"), name="even_mix",
    )(att, y, y, y, y, h, ln_g.reshape(1, aw), ln_b.reshape(1, aw), ws, bs, wout)


def _qkv_kernel(xc_ref, prev_ref, cw_ref, cb_ref, wq_ref, wk_ref, wv_ref,
                xconv_ref, q_ref, k_ref, v_ref, *extra, decode, tiles_per_seq):
    xc = xc_ref[...]
    tm, W = xc.shape
    hd = W // C_HEADS
    if decode:
        acc = cb_ref[...] + cw_ref[CONV_W - 1:CONV_W, :] * xc
        for j in range(CONV_W - 1):
            acc = acc + cw_ref[j:j + 1, :] * prev_ref[j]
        nb_ref = extra[0]
        for j in range(CONV_W - 2):
            nb_ref[j] = prev_ref[j + 1]
        nb_ref[CONV_W - 2] = xc
    else:
        i = pl.program_id(0)
        halo = jnp.where(i % tiles_per_seq == 0, 0.0, prev_ref[...])
        xx = jnp.concatenate([halo, xc], axis=0)
        acc = cb_ref[...] + cw_ref[CONV_W - 1:CONV_W, :] * xc
        for k in range(1, CONV_W):
            acc = acc + cw_ref[CONV_W - 1 - k:CONV_W - k, :] * pltpu.roll(xx, k, 0)[8:]
    xconv = _silu(acc)
    xconv_ref[...] = xconv
    for h in range(C_HEADS):
        sl = slice(h * hd, (h + 1) * hd)
        xh = xconv[:, sl]
        q_ref[:, sl] = _mm(xh, wq_ref[h]) * np.float32(hd ** -0.5)
        if decode:
            k_ref[:, sl] = _mm(xh, wk_ref[h])
        else:
            k_ref[0, sl, :] = _mm_nt(wk_ref[h], xh)
        v_ref[:, sl] = _mm(xc[:, sl], wv_ref[h])


def _qkv(y, prev, cw, cb, wq, wk, wv, *, batch, seq, tm, decode):
    T = y.shape[0]
    W = cw.shape[1]
    tps = seq // tm
    const2 = lambda i: (0, 0)
    const3 = lambda i: (0, 0, 0)
    row_blk = pl.BlockSpec((tm, W), lambda i: (i, 0))
    if decode:
        prev_spec = pl.BlockSpec(prev.shape, const3)
        prev_arg = prev
    else:
        prev_spec = pl.BlockSpec((8, W), lambda i: (jnp.maximum(i * (tm // 8) - 1, 0), 0))
        prev_arg = y
    in_specs = [row_blk, prev_spec, pl.BlockSpec(cw.shape, const2), pl.BlockSpec((1, W), const2),
                pl.BlockSpec(wq.shape, const3), pl.BlockSpec(wk.shape, const3), pl.BlockSpec(wv.shape, const3)]
    out_shape = [jax.ShapeDtypeStruct((T, W), F32), jax.ShapeDtypeStruct((T, W), F32)]
    out_specs = [row_blk, row_blk]
    if decode:
        out_shape.append(jax.ShapeDtypeStruct((T, W), F32))
        out_specs.append(row_blk)
    else:
        out_shape.append(jax.ShapeDtypeStruct((batch, W, seq), F32))
        out_specs.append(pl.BlockSpec((1, W, tm), lambda i: (i // tps, 0, i % tps)))
    out_shape.append(jax.ShapeDtypeStruct((T, W), F32))
    out_specs.append(row_blk)
    if decode:
        out_shape.append(jax.ShapeDtypeStruct(prev.shape, F32))
        out_specs.append(pl.BlockSpec(prev.shape, const3))
    return pl.pallas_call(
        functools.partial(_qkv_kernel, decode=decode, tiles_per_seq=tps),
        grid=(T // tm,), in_specs=in_specs, out_specs=out_specs, out_shape=out_shape,
        compiler_params=_cparams("parallel"), name="conv_qkv",
    )(y, prev_arg, cw, cb.reshape(1, W), wq, wk, wv)


def _mlstm_kernel(q_ref, kt_ref, v_ref, g_ref, o_ref, z_ref, xconv_ref, ng_ref, skip_ref, *rest,
                  decode, valid):
    if decode:
        c0_ref, n0_ref, m0_ref = rest[:3]
        rest = rest[3:]
    y_ref, c_out, n_out, m_out, caug_s, m_s = rest
    L = kt_ref.shape[2]
    W = kt_ref.shape[1]
    hd = W // C_HEADS
    ci = pl.program_id(1)
    lane_l = lax.broadcasted_iota(jnp.int32, (GATE_ROWS, L), 1)
    grow = lax.broadcasted_iota(jnp.int32, (GATE_ROWS, L), 0)

    @pl.when(ci == 0)
    def _():
        if decode:
            for h in range(C_HEADS):
                caug_s[h, :, :hd] = c0_ref[0, h]
                ncol = _row_to_col(n0_ref[0, h:h + 1, :hd // 2])
                ncol = jnp.concatenate([ncol, _row_to_col(n0_ref[0, h:h + 1, hd // 2:])], axis=0)
                lane = lax.broadcasted_iota(jnp.int32, (hd, LANES), 1)
                caug_s[h, :, hd:] = jnp.where(lane == 0, ncol, 0.0)
                m_s[h] = jnp.broadcast_to(m0_ref[0, h:h + 1, :], (8, LANES))
        else:
            caug_s[...] = jnp.zeros(caug_s.shape, F32)
            m_s[...] = jnp.zeros(m_s.shape, F32)

    if decode:
        pad = jnp.where(grow < C_HEADS, NEG_INF, 0.0)
        gates = jnp.where(lane_l < valid, jnp.broadcast_to(g_ref[0], (GATE_ROWS, L)), pad)
        rmask = lax.broadcasted_iota(jnp.int32, (L, W), 0) < valid
        expand = lambda ref: jnp.where(rmask, jnp.broadcast_to(ref[0], (L, W)), 0.0)
        q_all, v_all = expand(q_ref), expand(v_ref)
        o_all, z_all, xc_all = expand(o_ref), expand(z_ref), expand(xconv_ref)
    else:
        gates = g_ref[0]
        q_all, v_all = q_ref[...], v_ref[...]
        o_all, z_all, xc_all = o_ref[...], z_ref[...], xconv_ref[...]
    bsum = _lane_cumsum(gates)
    tri = (lax.broadcasted_iota(jnp.int32, (L, L), 0) >= lax.broadcasted_iota(jnp.int32, (L, L), 1))
    one_col = (lax.broadcasted_iota(jnp.int32, (L, LANES), 1) == 0).astype(F32)
    ys = []
    for h in range(C_HEADS):
        sl = slice(h * hd, (h + 1) * hd)
        qh, vh = q_all[:, sl], v_all[:, sl]
        kth = kt_ref[0, sl, :]
        i_row = gates[h:h + 1, :]
        b_row = bsum[C_HEADS + h:C_HEADS + h + 1, :]
        b_col = _row_to_col(b_row)
        m_prev = m_s[h][0:1, 0:1]
        prior = b_col + m_prev
        dm = jnp.where(tri, b_col - (b_row - i_row), NEG_INF)
        mt = jnp.maximum(prior, jnp.max(dm, -1, keepdims=True))
        wmat = jnp.exp(dm - mt)
        a = jnp.exp(prior - mt)
        sm = wmat * _mm(qh, kth)
        caug = caug_s[h]
        qc = _mm(qh, caug)
        num = _mm(sm, vh) + a * qc[:, :hd]
        den = jnp.sum(sm, -1, keepdims=True) + a * qc[:, hd:hd + 1]
        den = jnp.maximum(jnp.abs(den), jnp.exp(-mt))
        hc = num / den
        m_new = mt[L - 1:L, :]
        b_last = b_row[:, L - 1:L]
        wl = jnp.exp(b_last - b_row + i_row - m_new)
        a_l = jnp.exp(b_last + m_prev - m_new)
        vaug = jnp.concatenate([vh, one_col], axis=1)
        caug_s[h] = a_l * caug + _mm(kth * wl, vaug)
        m_s[h] = jnp.broadcast_to(m_new, (8, LANES))
        mu = jnp.mean(hc, -1, keepdims=True)
        var = jnp.mean((hc - mu) ** 2, -1, keepdims=True)
        hn = (hc - mu) * lax.rsqrt(var + EPS) * ng_ref[:, sl]
        ys.append((_sigmoid(o_all[:, sl]) * hn + skip_ref[:, sl] * xc_all[:, sl]) * _silu(z_all[:, sl]))
    y = jnp.concatenate(ys, axis=1)
    if decode:
        y_ref[0] = y[0:valid]
    else:
        y_ref[...] = y

    @pl.when(ci == pl.num_programs(1) - 1)
    def _():
        for h in range(C_HEADS):
            c_out[0, h] = caug_s[h, :, :hd]
            n_out[0, h] = caug_s[h, :, hd:hd + 1]
            m_out[0, h:h + 1, :] = m_s[h][0:1, 0:1]


def _mlstm(q, kt, v, gates, y_in, xconv, ng, skip, state, *, batch, seq, decode):
    W = kt.shape[1]
    hd = W // C_HEADS
    L = C_CHUNK
    nc = 1 if decode else seq // L
    const2 = lambda b, c: (0, 0)
    if decode:
        rowblk = lambda k: pl.BlockSpec((1, 1, W), lambda b, c: (b, 0, k))
        gspec = pl.BlockSpec((1, GATE_ROWS, 1), lambda b, c: (b, 0, 0))
        y_shape = jax.ShapeDtypeStruct((batch, 1, W), F32)
        y_spec = pl.BlockSpec((1, 1, W), lambda b, c: (b, 0, 0))
    else:
        rowblk = lambda k: pl.BlockSpec((L, W), lambda b, c: (b * nc + c, k))
        gspec = pl.BlockSpec((1, GATE_ROWS, L), lambda b, c: (b, 0, c))
        y_shape = jax.ShapeDtypeStruct((batch * seq, W), F32)
        y_spec = pl.BlockSpec((L, W), lambda b, c: (b * nc + c, 0))
    in_specs = [rowblk(0), pl.BlockSpec((1, W, L), lambda b, c: (b, 0, c)), rowblk(0), gspec,
                rowblk(2), rowblk(1), rowblk(0),
                pl.BlockSpec((1, W), const2), pl.BlockSpec((1, W), const2)]
    args = [q, kt, v, gates, y_in, y_in, xconv, ng.reshape(1, W), skip.reshape(1, W)]
    if decode:
        c0, n0, m0 = state
        in_specs += [pl.BlockSpec((1, C_HEADS, hd, hd), lambda b, c: (b, 0, 0, 0)),
                     pl.BlockSpec((1, C_HEADS, hd), lambda b, c: (b, 0, 0)),
                     pl.BlockSpec((1, C_HEADS, 1), lambda b, c: (b, 0, 0))]
        args += [c0, n0, m0.reshape(batch, C_HEADS, 1)]
    out_shape = [y_shape,
                 jax.ShapeDtypeStruct((batch, C_HEADS, hd, hd), F32),
                 jax.ShapeDtypeStruct((batch, C_HEADS, hd, 1), F32),
                 jax.ShapeDtypeStruct((batch, C_HEADS, 1), F32)]
    out_specs = [y_spec,
                 pl.BlockSpec((1, C_HEADS, hd, hd), lambda b, c: (b, 0, 0, 0)),
                 pl.BlockSpec((1, C_HEADS, hd, 1), lambda b, c: (b, 0, 0, 0)),
                 pl.BlockSpec((1, C_HEADS, 1), lambda b, c: (b, 0, 0))]
    y, c_new, n_new, m_new = pl.pallas_call(
        functools.partial(_mlstm_kernel, decode=decode, valid=1),
        grid=(batch, nc), in_specs=in_specs, out_specs=out_specs, out_shape=out_shape,
        scratch_shapes=[pltpu.VMEM((C_HEADS, hd, hd + LANES), F32), pltpu.VMEM((C_HEADS, 8, LANES), F32)],
        compiler_params=_cparams("parallel", "arbitrary"), name="mlstm_chunk",
    )(*args)
    return y, c_new, n_new.reshape(batch, C_HEADS, hd), m_new.reshape(batch, C_HEADS)


def _out_proj_kernel(y_ref, w_ref, h_ref, *rest, final):
    hn = h_ref[...] + _mm(y_ref[...], w_ref[...])
    if final:
        fg_ref, o_ref = rest
        o_ref[...] = hn * lax.rsqrt(jnp.mean(hn * hn, -1, keepdims=True) + EPS) * fg_ref[...]
    else:
        rest[0][...] = hn


def _out_proj(y, w, h, final_g, *, tm):
    T, D = h.shape
    K = y.shape[1]
    const2 = lambda i: (0, 0)
    in_specs = [pl.BlockSpec((tm, K), lambda i: (i, 0)), pl.BlockSpec(w.shape, const2),
                pl.BlockSpec((tm, D), lambda i: (i, 0))]
    args = [y, w, h]
    if final_g is not None:
        in_specs.append(pl.BlockSpec((1, D), const2))
        args.append(final_g.reshape(1, D))
    return pl.pallas_call(
        functools.partial(_out_proj_kernel, final=final_g is not None),
        grid=(T // tm,), in_specs=in_specs, out_specs=pl.BlockSpec((tm, D), lambda i: (i, 0)),
        out_shape=jax.ShapeDtypeStruct((T, D), F32),
        compiler_params=_cparams("parallel"), name="out_proj",
    )(*args)


def kernel(x_prompt, x_sample, cache_k, cache_v, cache_logf, state_c, state_n, state_m, state_conv, page_table,
           norm_g, final_g, even_w_in, even_b_f, gmlp_ln_g, gmlp_ln_b, gmlp_w_s, gmlp_b_s, even_w_out,
           odd_w_in, odd_b_i, odd_b_f, conv_w, conv_b, mlstm_w_q, mlstm_w_k, mlstm_w_v, mlstm_norm_g,
           mlstm_skip, odd_w_out):
    B, S, D = x_prompt.shape
    DB = x_sample.shape[0]
    depth = norm_g.shape[0]
    AW = A_WIDTH
    BW = gmlp_ln_g.shape[1]
    CW = conv_w.shape[2]
    gdim = BW // B_GROUPS
    tm = 512

    hp = x_prompt.reshape(B * S, D)
    hs = x_sample.reshape(DB, D)
    cache_kt = jnp.transpose(cache_k, (0, 1, 3, 4, 2))
    cache_vt = jnp.transpose(cache_v, (0, 1, 3, 4, 2))
    cache_lft = jnp.transpose(cache_logf, (0, 1, 3, 2))

    kp, vp, lp, ks, vs, ls, chv = [], [], [], [], [], [], []
    cp, np_, mp, bp, cs, ns, ms, bs = [], [], [], [], [], [], [], []
    for l in range(depth):
        j = l // 2
        last = l == depth - 1
        if l % 2 == 0:
            w = even_w_in[j]
            o = 0
            parts = {}
            for name, n in (("q", AW), ("k", AW), ("v", AW), ("fg", A_HEADS), ("za", AW), ("u", BW), ("vb", BW), ("zb", BW)):
                parts[name] = w[:, o:o + n]
                o += n
            wnn_p = jnp.concatenate([parts[n] for n in ("za", "u", "vb", "zb", "k")], 1).astype(BF16)
            wnn_s = jnp.concatenate([parts[n] for n in ("za", "u", "vb", "zb", "k", "q", "v")], 1).astype(BF16)
            wqt = parts["q"].T.astype(BF16)
            wkt = parts["k"].T.astype(BF16)
            wvt = parts["v"].T.astype(BF16)
            wg = parts["fg"].T.astype(BF16)
            gb = even_b_f[j].reshape(GATE_ROWS, 1)
            wout = even_w_out[j].astype(BF16)
            bs_full = jnp.repeat(gmlp_b_s[j].T, gdim, axis=1)
            ws_dec = jnp.repeat(gmlp_w_s[j][:, 0, 0], gdim).reshape(1, BW)
            bs_dec = jnp.repeat(gmlp_b_s[j][:, 0], gdim).reshape(1, BW)

            y, qt, kt, vt, lf = _proj(hp, norm_g[l], wnn_p, [wqt, wkt, wvt], wg, gb, n_lin=0, batch=B, seq=S, tm=tm)
            c = _seq_cumsum(lf)
            att = _fox_prompt(y, qt, vt, c, k_col=4 * AW, batch=B, seq=S, tile=512)
            hp = _even_mix(att, y, hp, gmlp_ln_g[j], gmlp_ln_b[j], gmlp_w_s[j], bs_full, wout,
                           col0=0, tm=tm, decode=False)[0]
            kp.append(jnp.transpose(kt.reshape(B, A_HEADS, A_HEAD_DIM, S), (0, 3, 1, 2)))
            vp.append(jnp.transpose(vt.reshape(B, A_HEADS, A_HEAD_DIM, S), (0, 3, 1, 2)))
            lp.append(jnp.transpose(lf, (0, 2, 1)))

            y, lf = _proj(hs, norm_g[l], wnn_s, [], wg, gb, n_lin=0, batch=1, seq=DB, tm=DB)
            lf_s = lf[0].T
            k_s, q_s, v_s = y[:, 4 * AW:5 * AW], y[:, 5 * AW:6 * AW], y[:, 6 * AW:7 * AW]
            att = _fox_decode(q_s, k_s, v_s, lf_s, cache_kt, cache_vt, cache_lft, page_table, j,
                              pages_per_step=8)
            hs, vn_s = _even_mix(att, y, hs, gmlp_ln_g[j], gmlp_ln_b[j], ws_dec, bs_dec, wout,
                                 col0=0, tm=DB, decode=True)
            ks.append(k_s.reshape(DB, 1, A_HEADS, A_HEAD_DIM))
            vs.append(v_s.reshape(DB, 1, A_HEADS, A_HEAD_DIM))
            ls.append(lf_s.reshape(DB, 1, A_HEADS))
            chv.append(vn_s.reshape(DB, 1, BW))
        else:
            w = odd_w_in[j]
            wnn = w[:, :3 * CW].astype(BF16)
            wg = w[:, 3 * CW:].T.astype(BF16)
            gb = jnp.concatenate([odd_b_i[j], odd_b_f[j]]).reshape(GATE_ROWS, 1)
            wq = mlstm_w_q[j].astype(BF16)
            wk = mlstm_w_k[j].astype(BF16)
            wkT = jnp.swapaxes(mlstm_w_k[j], 1, 2).astype(BF16)
            wv = mlstm_w_v[j].astype(BF16)
            wout = odd_w_out[j].astype(BF16)
            fin = final_g if last else None

            y, g = _proj(hp, norm_g[l], wnn, [], wg, gb, n_lin=C_HEADS, batch=B, seq=S, tm=tm)
            xconv, q, kt, v = _qkv(y, None, conv_w[j], conv_b[j], wq, wkT, wv, batch=B, seq=S, tm=tm, decode=False)
            ym, c1, n1, m1 = _mlstm(q, kt, v, g, y, xconv, mlstm_norm_g[j], mlstm_skip[j], None,
                                    batch=B, seq=S, decode=False)
            hp = _out_proj(ym, wout, hp, fin, tm=tm)
            cp.append(c1); np_.append(n1); mp.append(m1)
            bp.append(y.reshape(B, S, 3 * CW)[:, S - (CONV_W - 1):, :CW])

            y, g = _proj(hs, norm_g[l], wnn, [], wg, gb, n_lin=C_HEADS, batch=1, seq=DB, tm=DB)
            buf = jnp.transpose(state_conv[j], (1, 0, 2))
            xconv, q, k, v, nbuf = _qkv(y, buf, conv_w[j], conv_b[j], wq, wk, wv, batch=1, seq=DB, tm=DB, decode=True)
            kt = jnp.pad(k[:, :, None], ((0, 0), (0, 0), (0, C_CHUNK - 1)))
            g_s = jnp.transpose(g, (2, 1, 0))
            r3 = lambda a: a.reshape(DB, 1, a.shape[1])
            ym, c2, n2, m2 = _mlstm(r3(q), kt, r3(v), g_s, r3(y), r3(xconv), mlstm_norm_g[j], mlstm_skip[j],
                                    (state_c[j], state_n[j], state_m[j]), batch=DB, seq=1, decode=True)
            hs = _out_proj(ym.reshape(DB, CW), wout, hs, fin, tm=DB)
            cs.append(c2); ns.append(n2); ms.append(m2)
            bs.append(jnp.transpose(nbuf, (1, 0, 2)))
    y_prompt = hp.reshape(B, S, D)
    y_sample = hs.reshape(DB, 1, D)
    return (y_prompt, y_sample,
            jnp.stack(kp), jnp.stack(vp), jnp.stack(lp),
            jnp.stack(ks), jnp.stack(vs), jnp.stack(ls), jnp.stack(chv),
            jnp.stack(cp), jnp.stack(np_), jnp.stack(mp), jnp.stack(bp),
            jnp.stack(cs), jnp.stack(ns), jnp.stack(ms), jnp.stack(bs))
```

```python
import functools

import jax
import jax.numpy as jnp
import numpy as np
from jax import lax
from jax.experimental import pallas as pl
from jax.experimental.pallas import tpu as pltpu

F32 = jnp.float32
BF16 = jnp.bfloat16
EPS = 1e-6
NEG_INF = float("-inf")

A_HEADS = 8
A_HEAD_DIM = 64
A_WIDTH = A_HEADS * A_HEAD_DIM
B_GROUPS = 8
B_CHUNK = 128
C_HEADS = 4
C_CHUNK = 128
CONV_W = 4
LANES = 128
GATE_ROWS = 8
VMEM_LIMIT = 56 * 1024 * 1024


def _cparams(*sem):
    return pltpu.CompilerParams(dimension_semantics=sem, vmem_limit_bytes=VMEM_LIMIT)


def _mm(a, b):
    return jnp.dot(a.astype(BF16), b.astype(BF16), preferred_element_type=F32)


def _mm_nt(a, b):
    return lax.dot_general(a.astype(BF16), b.astype(BF16), (((1,), (1,)), ((), ())),
                           preferred_element_type=F32)


def _log_sigmoid(x):
    return jnp.minimum(x, 0.0) - jnp.log1p(jnp.exp(-jnp.abs(x)))


def _sigmoid(x):
    return 1.0 / (1.0 + jnp.exp(-x))


def _silu(x):
    return x * _sigmoid(x)


def _gelu(x):
    return 0.5 * x * (1.0 + lax.erf(x * np.float32(np.sqrt(0.5))))


def _lane_cumsum(x):
    lane = lax.broadcasted_iota(jnp.int32, x.shape, 1)
    k = 1
    while k < x.shape[1]:
        x = x + jnp.where(lane >= k, pltpu.roll(x, k, 1), 0.0)
        k *= 2
    return x


def _row_to_col(r):
    n = r.shape[1]
    eye = lax.broadcasted_iota(jnp.int32, (n, n), 0) == lax.broadcasted_iota(jnp.int32, (n, n), 1)
    return jnp.sum(jnp.where(eye, r, 0.0), axis=1, keepdims=True)


def _proj_kernel(x_ref, g_ref, wnn_ref, wg_ref, gb_ref, *rest, n_nt, n_alias, n_lin, col_chunk, cum_chunk):
    wnt_refs = rest[:n_nt]
    rest = rest[n_nt + n_alias:]
    y_ref = rest[0]
    yt_refs = rest[1:1 + n_nt]
    gt_ref = rest[1 + n_nt]
    x = x_ref[...]
    xn = x * lax.rsqrt(jnp.mean(x * x, -1, keepdims=True) + EPS) * g_ref[...]
    xb = xn.astype(BF16)
    n_nn = y_ref.shape[1]
    for c in range(0, n_nn, col_chunk):
        y_ref[:, c:c + col_chunk] = jnp.dot(xb, wnn_ref[:, c:c + col_chunk], preferred_element_type=F32)
    for w_ref, o_ref in zip(wnt_refs, yt_refs):
        o_ref[0, 0] = _mm_nt(w_ref[...], xb)
    gt = _mm_nt(wg_ref[...], xb) + gb_ref[...]
    row = lax.broadcasted_iota(jnp.int32, gt.shape, 0)
    gt = jnp.where(row >= n_lin, _log_sigmoid(gt), gt)
    if cum_chunk is None:
        gt_ref[0] = gt
    else:
        upto = (lax.broadcasted_iota(jnp.int32, (cum_chunk, cum_chunk), 0)
                <= lax.broadcasted_iota(jnp.int32, (cum_chunk, cum_chunk), 1)).astype(F32)
        rowc = row[:, :cum_chunk]
        for c in range(0, gt.shape[1], cum_chunk):
            blk = gt[:, c:c + cum_chunk]
            cs = jnp.dot(blk, upto, precision=lax.Precision.HIGHEST, preferred_element_type=F32)
            gt_ref[0, :, c:c + cum_chunk] = jnp.where(rowc >= n_lin, cs, blk)


def _proj(x, g, wnn, wnts, wg, gb, *, n_lin, batch, seq, tm, cum_chunk=None, layer=0, n_layers=1, stacked=None):
    T, D = x.shape
    tps = seq // tm
    n_nn = wnn.shape[1]
    n_nt = len(wnts)
    const = lambda i: (0, 0)
    tok_t = lambda i: (i // tps, 0, i % tps)
    in_specs = [pl.BlockSpec((tm, D), lambda i: (i, 0)),
                pl.BlockSpec((1, D), const),
                pl.BlockSpec((D, n_nn), const),
                pl.BlockSpec((GATE_ROWS, D), const),
                pl.BlockSpec((GATE_ROWS, 1), const)]
    in_specs += [pl.BlockSpec(w.shape, const) for w in wnts]
    args = [x, g.reshape(1, D), wnn, wg, gb, *wnts]
    aliases = {}
    if stacked is not None:
        for k, buf in enumerate(stacked):
            aliases[len(args)] = 1 + k
            in_specs.append(pl.BlockSpec(memory_space=pl.ANY))
            args.append(buf)
    out_shape = [jax.ShapeDtypeStruct((T, n_nn), F32)]
    out_specs = [pl.BlockSpec((tm, n_nn), lambda i: (i, 0))]
    for w in wnts:
        out_shape.append(jax.ShapeDtypeStruct((n_layers, batch, w.shape[0], seq), F32))
        out_specs.append(pl.BlockSpec((1, 1, w.shape[0], tm), lambda i: (layer, i // tps, 0, i % tps)))
    out_shape.append(jax.ShapeDtypeStruct((batch, GATE_ROWS, seq), F32))
    out_specs.append(pl.BlockSpec((1, GATE_ROWS, tm), tok_t))
    return pl.pallas_call(
        functools.partial(_proj_kernel, n_nt=n_nt, n_alias=len(aliases), n_lin=n_lin, col_chunk=512,
                          cum_chunk=cum_chunk),
        grid=(T // tm,), in_specs=in_specs, out_specs=out_specs, out_shape=out_shape,
        input_output_aliases=aliases,
        compiler_params=_cparams("parallel"), name="norm_proj",
    )(*args)


def _cumsum_kernel(x_ref, o_ref):
    S = x_ref.shape[2]
    carry = jnp.zeros((GATE_ROWS, 1), F32)
    for c in range(0, S, LANES):
        inc = _lane_cumsum(x_ref[0, :, c:c + LANES]) + carry
        o_ref[0, :, c:c + LANES] = inc
        carry = inc[:, LANES - 1:LANES]


def _seq_cumsum(x):
    B, R, S = x.shape
    spec = pl.BlockSpec((1, R, S), lambda b: (b, 0, 0))
    return pl.pallas_call(_cumsum_kernel, grid=(B,), in_specs=[spec], out_specs=spec,
                          out_shape=jax.ShapeDtypeStruct(x.shape, F32),
                          compiler_params=_cparams("parallel"), name="logf_cumsum")(x)


def _fox_kernel(qi_ref, kj_ref, qt_ref, k_ref, vt_ref, crow_ref, ccol_ref, o_ref, q_s, m_s, l_s, a_s, *,
                tile, key_chunk):
    qi = qi_ref[pl.program_id(2)]
    kj = kj_ref[pl.program_id(2)]
    hd = A_HEAD_DIM

    @pl.when(kj == 0)
    def _():
        qt = qt_ref[0, 0] * np.float32(hd ** -0.5)
        row = lax.broadcasted_iota(jnp.int32, qt.shape, 0)
        q_s[0] = jnp.where(row < hd, qt, 0.0).astype(BF16)
        q_s[1] = jnp.where(row >= hd, qt, 0.0).astype(BF16)
        m_s[...] = jnp.full(m_s.shape, NEG_INF, F32)
        l_s[...] = jnp.zeros(l_s.shape, F32)
        a_s[...] = jnp.zeros(a_s.shape, F32)

    def step(diagonal):
        cq = crow_ref[0, 0]
        join = lambda old, new, c0: new if c0 == 0 else jnp.concatenate([old[:, :c0], new], axis=1)
        for hh in range(2):
            rows = slice(hh * hd, (hh + 1) * hd)
            m_run, l_run, acc = m_s[hh], l_s[hh], a_s[rows, :]
            for c in range(0, tile, key_chunk):
                c0 = c if diagonal else 0
                kb = k_ref[c:c + key_chunk, :].astype(BF16)
                s = jnp.dot(kb, q_s[hh, :, c0:], preferred_element_type=F32)
                s = s + cq[hh:hh + 1, c0:] - ccol_ref[0, 0, c:c + key_chunk, hh:hh + 1]
                if diagonal:
                    causal = (lax.broadcasted_iota(jnp.int32, s.shape, 0)
                              <= lax.broadcasted_iota(jnp.int32, s.shape, 1))
                    s = jnp.where(causal, s, NEG_INF)
                m_old = m_run[:, c0:]
                m_new = jnp.maximum(m_old, jnp.max(s, 0, keepdims=True))
                alpha = jnp.exp(m_old - m_new)
                p = jnp.exp(s - m_new)
                l_new = alpha * l_run[:, c0:] + jnp.sum(p, 0, keepdims=True)
                vt = vt_ref[0, 0, rows, c:c + key_chunk].astype(BF16)
                a_new = alpha * acc[:, c0:] + jnp.dot(vt, p.astype(BF16), preferred_element_type=F32)
                m_run, l_run, acc = join(m_run, m_new, c0), join(l_run, l_new, c0), join(acc, a_new, c0)
            m_s[hh], l_s[hh] = m_run, l_run
            a_s[rows, :] = acc

    @pl.when(kj < qi)
    def _():
        step(False)

    @pl.when(kj == qi)
    def _():
        step(True)
        row = lax.broadcasted_iota(jnp.int32, (2 * hd, tile), 0)
        out_t = a_s[...] / jnp.where(row < hd, l_s[0], l_s[1])
        o_ref[...] = out_t.T


def _fox_prompt(y, qt, vt, c, *, k_col, layer, batch, seq, tile):
    T = y.shape[0]
    pairs = A_HEADS // 2
    nt = seq // tile
    kb0 = k_col // LANES
    crow = c.reshape(batch, pairs, 2, seq)
    ccol = jnp.swapaxes(crow, 2, 3)
    tri = [(i, j) for i in range(nt) for j in range(i + 1)]
    qi_tab = jnp.asarray([i for i, _ in tri], jnp.int32)
    kj_tab = jnp.asarray([j for _, j in tri], jnp.int32)
    grid_spec = pltpu.PrefetchScalarGridSpec(
        num_scalar_prefetch=2, grid=(batch, pairs, len(tri)),
        in_specs=[pl.BlockSpec((1, 1, LANES, tile), lambda b, p, t, qi, kj: (layer, b, p, qi[t])),
                  pl.BlockSpec((tile, LANES), lambda b, p, t, qi, kj: (b * nt + kj[t], kb0 + p)),
                  pl.BlockSpec((1, 1, LANES, tile), lambda b, p, t, qi, kj: (layer, b, p, kj[t])),
                  pl.BlockSpec((1, 1, 2, tile), lambda b, p, t, qi, kj: (b, p, 0, qi[t])),
                  pl.BlockSpec((1, 1, tile, 2), lambda b, p, t, qi, kj: (b, p, kj[t], 0))],
        out_specs=pl.BlockSpec((tile, LANES), lambda b, p, t, qi, kj: (b * nt + qi[t], p)),
        scratch_shapes=[pltpu.VMEM((2, LANES, tile), BF16), pltpu.VMEM((2, 1, tile), F32),
                        pltpu.VMEM((2, 1, tile), F32), pltpu.VMEM((LANES, tile), F32)])
    return pl.pallas_call(
        functools.partial(_fox_kernel, tile=tile, key_chunk=tile),
        grid_spec=grid_spec, out_shape=jax.ShapeDtypeStruct((T, A_WIDTH), F32),
        compiler_params=_cparams("parallel", "parallel", "arbitrary"),
        name="fox_prompt",
    )(qi_tab, kj_tab, qt, y, vt, crow, ccol)


def _fox_decode_kernel(pt_ref, q_ref, kn_ref, vn_ref, lfn_ref, *rest, pages_per_step):
    P = pages_per_step
    k_refs = rest[:P]
    v_refs = rest[P:2 * P]
    lf_refs = rest[2 * P:3 * P]
    o_ref = rest[3 * P]
    m_s, l_s, c_s, a_s, q_s = rest[3 * P + 1:]
    r = pl.program_id(1)
    H, hd = A_HEADS, A_HEAD_DIM
    hsl = [slice(h * hd, (h + 1) * hd) for h in range(H)]

    @pl.when(r == 0)
    def _():
        qcols = [q_ref[0, hsl[h], :] * np.float32(hd ** -0.5) for h in range(H)]
        for h in range(H):
            q_s[h] = jnp.broadcast_to(qcols[h], (hd, LANES))
        s_new = jnp.concatenate(
            [jnp.sum(qcols[h] * kn_ref[0, hsl[h], :], axis=0, keepdims=True) for h in range(H)], axis=0)
        m_s[...] = s_new
        l_s[...] = jnp.ones(l_s.shape, F32)
        c_s[...] = lfn_ref[0]
        lane = lax.broadcasted_iota(jnp.int32, (hd, LANES), 1)
        for h in range(H):
            a_s[h] = jnp.where(lane == 0, vn_ref[0, hsl[h], :], 0.0)

    c = c_s[...]
    lf_all = jnp.concatenate([lf_refs[i][...] for i in range(P)], axis=0)
    later = (lax.broadcasted_iota(jnp.int32, (LANES, LANES), 0)
             > lax.broadcasted_iota(jnp.int32, (LANES, LANES), 1)).astype(F32)
    excl_all = jnp.dot(lf_all, later, precision=lax.Precision.HIGHEST, preferred_element_type=F32)
    tot_all = jnp.sum(lf_all, -1, keepdims=True)
    s_pages = []
    for i in range(P):
        rows = [jnp.sum(k_refs[i][h] * q_s[h], axis=0, keepdims=True) for h in range(H)]
        s_pages.append(jnp.concatenate(rows, axis=0) + (c + excl_all[i * H:(i + 1) * H, :]))
        c = c + tot_all[i * H:(i + 1) * H, :]
    c_s[...] = c
    s_all = jnp.concatenate(s_pages, axis=1)
    m_old = m_s[...]
    m_new = jnp.maximum(m_old, jnp.max(s_all, -1, keepdims=True))
    alpha = jnp.exp(m_old - m_new)
    p_all = jnp.exp(s_all - m_new)
    l_s[...] = alpha * l_s[...] + jnp.sum(p_all, -1, keepdims=True)
    m_s[...] = m_new
    for h in range(H):
        acc = a_s[h] * alpha[h:h + 1, :]
        for i in range(P):
            acc = acc + p_all[h:h + 1, i * LANES:(i + 1) * LANES] * v_refs[i][h]
        a_s[h] = acc

    @pl.when(r == pl.num_programs(1) - 1)
    def _():
        l = l_s[...]
        for h in range(H):
            o_ref[0, hsl[h], :] = jnp.sum(a_s[h], axis=1, keepdims=True) / l[h:h + 1, :]


def _fox_decode(q, k_new, v_new, lf_new, cache_kt, cache_vt, cache_lft, page_table, layer, *, pages_per_step):
    DB, n_pages = page_table.shape
    P = pages_per_step
    steps = n_pages // P
    col = lambda a: a.reshape(DB, a.shape[1], 1)

    def page_idx(i):
        return lambda b, r, pt: (layer, pt[b, n_pages - 1 - (r * P + i)], 0, 0, 0)

    def lf_idx(i):
        return lambda b, r, pt: (layer, pt[b, n_pages - 1 - (r * P + i)], 0, 0)

    vec = lambda n: pl.BlockSpec((1, n, 1), lambda b, r, pt: (b, 0, 0))
    kv_block = (None, None, A_HEADS, A_HEAD_DIM, LANES)
    in_specs = [vec(A_WIDTH), vec(A_WIDTH), vec(A_WIDTH), vec(A_HEADS)]
    in_specs += [pl.BlockSpec(kv_block, page_idx(i)) for i in range(P)]
    in_specs += [pl.BlockSpec(kv_block, page_idx(i)) for i in range(P)]
    in_specs += [pl.BlockSpec((None, None, A_HEADS, LANES), lf_idx(i)) for i in range(P)]
    grid_spec = pltpu.PrefetchScalarGridSpec(
        num_scalar_prefetch=1, grid=(DB, steps), in_specs=in_specs,
        out_specs=vec(A_WIDTH),
        scratch_shapes=[pltpu.VMEM((A_HEADS, 1), F32), pltpu.VMEM((A_HEADS, 1), F32),
                        pltpu.VMEM((A_HEADS, 1), F32),
                        pltpu.VMEM((A_HEADS, A_HEAD_DIM, LANES), F32),
                        pltpu.VMEM((A_HEADS, A_HEAD_DIM, LANES), F32)])
    out = pl.pallas_call(
        functools.partial(_fox_decode_kernel, pages_per_step=P),
        grid_spec=grid_spec, out_shape=jax.ShapeDtypeStruct((DB, A_WIDTH, 1), F32),
        compiler_params=_cparams("parallel", "arbitrary"), name="fox_decode",
    )(page_table, col(q), col(k_new), col(v_new), col(lf_new),
      *([cache_kt] * P), *([cache_vt] * P), *([cache_lft] * P))
    return out.reshape(DB, A_WIDTH)


def _even_mix_kernel(att_ref, za_ref, u_ref, vb_ref, zb_ref, h_ref, lng_ref, lnb_ref, ws_ref, bs_ref,
                     wout_ref, *outs, decode):
    o_ref = outs[0]
    ya = att_ref[...] * _silu(za_ref[...])
    u = _gelu(u_ref[...])
    vf = _gelu(vb_ref[...])
    mu = jnp.mean(vf, -1, keepdims=True)
    var = jnp.mean((vf - mu) ** 2, -1, keepdims=True)
    vn = (vf - mu) * lax.rsqrt(var + EPS) * lng_ref[...] + lnb_ref[...]
    tm, bw = vn.shape
    if decode:
        outs[1][...] = vn
        mix = vn * ws_ref[...] + bs_ref[...]
    else:
        lane = lax.broadcasted_iota(jnp.int32, (B_CHUNK, LANES), 1)
        tri = (lax.broadcasted_iota(jnp.int32, (B_CHUNK, B_CHUNK), 0)
               >= lax.broadcasted_iota(jnp.int32, (B_CHUNK, B_CHUNK), 1))
        wtril = [jnp.where(tri, ws_ref[g], 0.0).astype(BF16) for g in range(B_GROUPS)]
        gpl = LANES // (bw // B_GROUPS)
        rows = []
        for c in range(0, tm, B_CHUNK):
            blocks = []
            for lb in range(bw // LANES):
                vblk = vn[c:c + B_CHUNK, lb * LANES:(lb + 1) * LANES].astype(BF16)
                y0 = jnp.dot(wtril[lb * gpl], vblk, preferred_element_type=F32)
                y1 = jnp.dot(wtril[lb * gpl + 1], vblk, preferred_element_type=F32)
                blocks.append(jnp.where(lane < LANES // gpl, y0, y1))
            rows.append(jnp.concatenate(blocks, axis=1) + bs_ref[...])
        mix = jnp.concatenate(rows, axis=0)
    yb = u * mix * _silu(zb_ref[...])
    aw = ya.shape[1]
    hn = h_ref[...] + _mm(ya, wout_ref[:aw, :]) + _mm(yb, wout_ref[aw:, :])
    o_ref[...] = hn


def _even_mix(att, y, h, ln_g, ln_b, ws, bs, wout, *, col0, tm, decode):
    T, D = h.shape
    aw = att.shape[1]
    const2 = lambda i: (0, 0)
    yblk = lambda k: pl.BlockSpec((tm, aw), lambda i: (i, col0 + k))
    ws_spec = (pl.BlockSpec(ws.shape, const2) if decode else pl.BlockSpec(ws.shape, lambda i: (0, 0, 0)))
    in_specs = [pl.BlockSpec((tm, aw), lambda i: (i, 0)), yblk(0), yblk(1), yblk(2), yblk(3),
                pl.BlockSpec((tm, D), lambda i: (i, 0)),
                pl.BlockSpec((1, aw), const2), pl.BlockSpec((1, aw), const2),
                ws_spec, pl.BlockSpec(bs.shape, const2), pl.BlockSpec(wout.shape, const2)]
    out_shape = [jax.ShapeDtypeStruct((T, D), F32)]
    out_specs = [pl.BlockSpec((tm, D), lambda i: (i, 0))]
    if decode:
        out_shape.append(jax.ShapeDtypeStruct((T, aw), F32))
        out_specs.append(pl.BlockSpec((tm, aw), lambda i: (i, 0)))
    return pl.pallas_call(
        functools.partial(_even_mix_kernel, decode=decode),
        grid=(T // tm,), in_specs=in_specs, out_specs=out_specs, out_shape=out_shape,
        compiler_params=_cparams("parallel"), name="even_mix",
    )(att, y, y, y, y, h, ln_g.reshape(1, aw), ln_b.reshape(1, aw), ws, bs, wout)


def _qkv_kernel(xc_ref, prev_ref, cw_ref, cb_ref, wq_ref, wk_ref, wv_ref,
                xconv_ref, q_ref, k_ref, v_ref, *extra, decode, tiles_per_seq):
    xc = xc_ref[...]
    tm, W = xc.shape
    hd = W // C_HEADS
    if decode:
        acc = cb_ref[...] + cw_ref[CONV_W - 1:CONV_W, :] * xc
        for j in range(CONV_W - 1):
            acc = acc + cw_ref[j:j + 1, :] * prev_ref[j]
        nb_ref = extra[0]
        for j in range(CONV_W - 2):
            nb_ref[j] = prev_ref[j + 1]
        nb_ref[CONV_W - 2] = xc
    else:
        i = pl.program_id(0)
        halo = jnp.where(i % tiles_per_seq == 0, 0.0, prev_ref[...])
        xx = jnp.concatenate([halo, xc], axis=0)
        acc = cb_ref[...] + cw_ref[CONV_W - 1:CONV_W, :] * xc
        for k in range(1, CONV_W):
            acc = acc + cw_ref[CONV_W - 1 - k:CONV_W - k, :] * pltpu.roll(xx, k, 0)[8:]
    xconv = _silu(acc)
    xconv_ref[...] = xconv
    for h in range(C_HEADS):
        sl = slice(h * hd, (h + 1) * hd)
        xh = xconv[:, sl]
        q_ref[:, sl] = _mm(xh, wq_ref[h]) * np.float32(hd ** -0.5)
        if decode:
            k_ref[:, sl] = _mm(xh, wk_ref[h])
        else:
            k_ref[0, sl, :] = _mm_nt(wk_ref[h], xh)
        v_ref[:, sl] = _mm(xc[:, sl], wv_ref[h])


def _qkv(y, prev, cw, cb, wq, wk, wv, *, batch, seq, tm, decode):
    T = y.shape[0]
    W = cw.shape[1]
    tps = seq // tm
    const2 = lambda i: (0, 0)
    const3 = lambda i: (0, 0, 0)
    row_blk = pl.BlockSpec((tm, W), lambda i: (i, 0))
    if decode:
        prev_spec = pl.BlockSpec(prev.shape, const3)
        prev_arg = prev
    else:
        prev_spec = pl.BlockSpec((8, W), lambda i: (jnp.maximum(i * (tm // 8) - 1, 0), 0))
        prev_arg = y
    in_specs = [row_blk, prev_spec, pl.BlockSpec(cw.shape, const2), pl.BlockSpec((1, W), const2),
                pl.BlockSpec(wq.shape, const3), pl.BlockSpec(wk.shape, const3), pl.BlockSpec(wv.shape, const3)]
    out_shape = [jax.ShapeDtypeStruct((T, W), F32), jax.ShapeDtypeStruct((T, W), F32)]
    out_specs = [row_blk, row_blk]
    if decode:
        out_shape.append(jax.ShapeDtypeStruct((T, W), F32))
        out_specs.append(row_blk)
    else:
        out_shape.append(jax.ShapeDtypeStruct((batch, W, seq), F32))
        out_specs.append(pl.BlockSpec((1, W, tm), lambda i: (i // tps, 0, i % tps)))
    out_shape.append(jax.ShapeDtypeStruct((T, W), F32))
    out_specs.append(row_blk)
    if decode:
        out_shape.append(jax.ShapeDtypeStruct(prev.shape, F32))
        out_specs.append(pl.BlockSpec(prev.shape, const3))
    return pl.pallas_call(
        functools.partial(_qkv_kernel, decode=decode, tiles_per_seq=tps),
        grid=(T // tm,), in_specs=in_specs, out_specs=out_specs, out_shape=out_shape,
        compiler_params=_cparams("parallel"), name="conv_qkv",
    )(y, prev_arg, cw, cb.reshape(1, W), wq, wk, wv)


def _mlstm_kernel(q_ref, kt_ref, v_ref, g_ref, o_ref, z_ref, xconv_ref, ng_ref, skip_ref, *rest,
                  decode, valid, nb, n_alias):
    if decode:
        c0_ref, n0_ref, m0_ref = rest[:3]
        rest = rest[3:]
    y_ref, c_out, n_out, m_out, caug_s, m_s = rest[n_alias:]
    L = kt_ref.shape[2]
    W = kt_ref.shape[1]
    hd = W // C_HEADS
    ci = pl.program_id(1)

    @pl.when(ci == 0)
    def _():
        if decode:
            lane = lax.broadcasted_iota(jnp.int32, (hd, LANES), 1)
            for b in range(nb):
                for h in range(C_HEADS):
                    caug_s[b, h, :, :hd] = c0_ref[0, b, h]
                    ncol = jnp.concatenate([_row_to_col(n0_ref[0, b, h:h + 1, k:k + LANES])
                                            for k in range(0, hd, LANES)], axis=0)
                    caug_s[b, h, :, hd:] = jnp.where(lane == 0, ncol, 0.0)
                    m_s[b, h] = jnp.broadcast_to(m0_ref[0, b, h:h + 1, :], (8, LANES))
        else:
            caug_s[...] = jnp.zeros(caug_s.shape, F32)
            m_s[...] = jnp.zeros(m_s.shape, F32)

    tri = (lax.broadcasted_iota(jnp.int32, (L, L), 0) >= lax.broadcasted_iota(jnp.int32, (L, L), 1))
    one_col = (lax.broadcasted_iota(jnp.int32, (L, LANES), 1) == 0).astype(F32)
    lane_l = lax.broadcasted_iota(jnp.int32, (GATE_ROWS, L), 1)
    grow = lax.broadcasted_iota(jnp.int32, (GATE_ROWS, L), 0)
    for b in range(nb):
        if decode:
            graw = jnp.broadcast_to(g_ref[b], (GATE_ROWS, L))
            gates = jnp.where((grow < C_HEADS) & (lane_l >= valid), NEG_INF, graw)
            rmask = lax.broadcasted_iota(jnp.int32, (L, W), 0) < valid
            expand = lambda ref: jnp.where(rmask, jnp.broadcast_to(ref[b], (L, W)), 0.0)
            q_all, v_all = expand(q_ref), expand(v_ref)
            o_all, z_all, xc_all = expand(o_ref), expand(z_ref), expand(xconv_ref)
        else:
            gates = g_ref[b]
            q_all, v_all = q_ref[b], v_ref[b]
            o_all, z_all, xc_all = o_ref[b], z_ref[b], xconv_ref[b]
        ys = []
        for h in range(C_HEADS):
            sl = slice(h * hd, (h + 1) * hd)
            qh, vh = q_all[:, sl], v_all[:, sl]
            kth = kt_ref[b, sl, :]
            b_row = gates[C_HEADS + h:C_HEADS + h + 1, :]
            g_row = gates[h:h + 1, :] - b_row
            m_prev = m_s[b, h][0:1, 0:1]
            dm = jnp.where(tri, g_row, NEG_INF)
            mcol = jnp.maximum(m_prev, jnp.max(dm, -1, keepdims=True))
            wmat = jnp.exp(dm - mcol)
            a = jnp.exp(m_prev - mcol)
            sm = wmat * _mm(qh, kth)
            caug = caug_s[b, h]
            qc = _mm(qh, caug)
            num = _mm(sm, vh) + a * qc[:, :hd]
            den = jnp.sum(sm, -1, keepdims=True) + a * qc[:, hd:hd + 1]
            den = jnp.maximum(jnp.abs(den), jnp.exp(-(_row_to_col(b_row) + mcol)))
            hc = num / den
            m_last = mcol[L - 1:L, :]
            wl = jnp.exp(g_row - m_last)
            a_l = jnp.exp(m_prev - m_last)
            vaug = jnp.concatenate([vh, one_col], axis=1)
            caug_s[b, h] = a_l * caug + _mm(kth * wl, vaug)
            m_s[b, h] = jnp.broadcast_to(b_row[:, L - 1:L] + m_last, (8, LANES))
            mu = jnp.mean(hc, -1, keepdims=True)
            var = jnp.mean((hc - mu) ** 2, -1, keepdims=True)
            hn = (hc - mu) * lax.rsqrt(var + EPS) * ng_ref[:, sl]
            ys.append((_sigmoid(o_all[:, sl]) * hn + skip_ref[:, sl] * xc_all[:, sl]) * _silu(z_all[:, sl]))
        y = jnp.concatenate(ys, axis=1)
        y_ref[b] = y[0:valid] if decode else y

    @pl.when(ci == pl.num_programs(1) - 1)
    def _():
        for b in range(nb):
            for h in range(C_HEADS):
                c_out[0, b, h] = caug_s[b, h, :, :hd]
                n_out[b, h] = caug_s[b, h, :, hd:hd + 1]
                m_out[b, h:h + 1, :] = m_s[b, h][0:1, 0:1]


def _mlstm(q, kt, v, gates, y_in, xconv, ng, skip, state, c_stack, *, layer, n_layers, decode, nb):
    batch, W = kt.shape[0], kt.shape[1]
    hd = W // C_HEADS
    L = C_CHUNK
    nc = 1 if decode else q.shape[1] // L
    rows = 1 if decode else L
    const2 = lambda b, c: (0, 0)
    if decode:
        rowblk = lambda k: pl.BlockSpec((nb, 1, W), lambda b, c: (b, 0, k))
        gspec = pl.BlockSpec((nb, GATE_ROWS, 1), lambda b, c: (b, 0, 0))
    else:
        rowblk = lambda k: pl.BlockSpec((nb, L, W), lambda b, c: (b, c, k))
        gspec = pl.BlockSpec((nb, GATE_ROWS, L), lambda b, c: (b, 0, c))
    in_specs = [rowblk(0), pl.BlockSpec((nb, W, L), lambda b, c: (b, 0, c)), rowblk(0), gspec,
                rowblk(2), rowblk(1), rowblk(0),
                pl.BlockSpec((1, W), const2), pl.BlockSpec((1, W), const2)]
    args = [q, kt, v, gates, y_in, y_in, xconv, ng.reshape(1, W), skip.reshape(1, W)]
    if decode:
        c0, n0, m0 = state
        in_specs += [pl.BlockSpec((1, nb, C_HEADS, hd, hd), lambda b, c: (layer, b, 0, 0, 0)),
                     pl.BlockSpec((1, nb, C_HEADS, hd), lambda b, c: (layer, b, 0, 0)),
                     pl.BlockSpec((1, nb, C_HEADS, 1), lambda b, c: (layer, b, 0, 0))]
        args += [c0, n0, m0.reshape(m0.shape + (1,))]
    aliases = {}
    if c_stack is not None:
        aliases[len(args)] = 1
        in_specs.append(pl.BlockSpec(memory_space=pl.ANY))
        args.append(c_stack)
    out_shape = [jax.ShapeDtypeStruct((batch, q.shape[1], W), F32),
                 jax.ShapeDtypeStruct((n_layers, batch, C_HEADS, hd, hd), F32),
                 jax.ShapeDtypeStruct((batch, C_HEADS, hd, 1), F32),
                 jax.ShapeDtypeStruct((batch, C_HEADS, 1), F32)]
    out_specs = [pl.BlockSpec((nb, rows, W), lambda b, c: (b, c, 0)),
                 pl.BlockSpec((1, nb, C_HEADS, hd, hd), lambda b, c: (layer, b, 0, 0, 0)),
                 pl.BlockSpec((nb, C_HEADS, hd, 1), lambda b, c: (b, 0, 0, 0)),
                 pl.BlockSpec((nb, C_HEADS, 1), lambda b, c: (b, 0, 0))]
    y, c_new, n_new, m_new = pl.pallas_call(
        functools.partial(_mlstm_kernel, decode=decode, valid=1, nb=nb, n_alias=len(aliases)),
        grid=(batch // nb, nc), in_specs=in_specs, out_specs=out_specs, out_shape=out_shape,
        input_output_aliases=aliases,
        scratch_shapes=[pltpu.VMEM((nb, C_HEADS, hd, hd + LANES), F32), pltpu.VMEM((nb, C_HEADS, 8, LANES), F32)],
        compiler_params=_cparams("parallel", "arbitrary"), name="mlstm_chunk",
    )(*args)
    return y, c_new, n_new.reshape(batch, C_HEADS, hd), m_new.reshape(batch, C_HEADS)


def _out_proj_kernel(y_ref, w_ref, h_ref, *rest, final):
    hn = h_ref[...] + _mm(y_ref[...], w_ref[...])
    if final:
        fg_ref, o_ref = rest
        o_ref[...] = hn * lax.rsqrt(jnp.mean(hn * hn, -1, keepdims=True) + EPS) * fg_ref[...]
    else:
        rest[0][...] = hn


def _out_proj(y, w, h, final_g, *, tm):
    T, D = h.shape
    K = y.shape[1]
    const2 = lambda i: (0, 0)
    in_specs = [pl.BlockSpec((tm, K), lambda i: (i, 0)), pl.BlockSpec(w.shape, const2),
                pl.BlockSpec((tm, D), lambda i: (i, 0))]
    args = [y, w, h]
    if final_g is not None:
        in_specs.append(pl.BlockSpec((1, D), const2))
        args.append(final_g.reshape(1, D))
    return pl.pallas_call(
        functools.partial(_out_proj_kernel, final=final_g is not None),
        grid=(T // tm,), in_specs=in_specs, out_specs=pl.BlockSpec((tm, D), lambda i: (i, 0)),
        out_shape=jax.ShapeDtypeStruct((T, D), F32),
        compiler_params=_cparams("parallel"), name="out_proj",
    )(*args)


def kernel(x_prompt, x_sample, cache_k, cache_v, cache_logf, state_c, state_n, state_m, state_conv, page_table,
           norm_g, final_g, even_w_in, even_b_f, gmlp_ln_g, gmlp_ln_b, gmlp_w_s, gmlp_b_s, even_w_out,
           odd_w_in, odd_b_i, odd_b_f, conv_w, conv_b, mlstm_w_q, mlstm_w_k, mlstm_w_v, mlstm_norm_g,
           mlstm_skip, odd_w_out):
    B, S, D = x_prompt.shape
    DB = x_sample.shape[0]
    depth = norm_g.shape[0]
    AW = A_WIDTH
    BW = gmlp_ln_g.shape[1]
    CW = conv_w.shape[2]
    gdim = BW // B_GROUPS
    tm = 512

    hp = x_prompt.reshape(B * S, D)
    hs = x_sample.reshape(DB, D)
    cache_kt = jnp.transpose(cache_k, (0, 1, 3, 4, 2))
    cache_vt = jnp.transpose(cache_v, (0, 1, 3, 4, 2))
    cache_lft = jnp.transpose(cache_logf, (0, 1, 3, 2))

    lp, ks, vs, ls, chv = [], [], [], [], []
    np_, mp, bp, ns, ms, bs = [], [], [], [], [], []
    n_even, n_odd = (depth + 1) // 2, depth // 2
    assert depth % 2 == 0, "the final rmsnorm is fused into the last (mLSTM) layer's output projection"
    qkv_stack = cp_stack = cs_stack = None
    for l in range(depth):
        j = l // 2
        last = l == depth - 1
        if l % 2 == 0:
            w = even_w_in[j]
            o = 0
            parts = {}
            for name, n in (("q", AW), ("k", AW), ("v", AW), ("fg", A_HEADS), ("za", AW), ("u", BW), ("vb", BW), ("zb", BW)):
                parts[name] = w[:, o:o + n]
                o += n
            wnn_p = jnp.concatenate([parts[n] for n in ("za", "u", "vb", "zb", "k")], 1).astype(BF16)
            wnn_s = jnp.concatenate([parts[n] for n in ("za", "u", "vb", "zb", "k", "q", "v")], 1).astype(BF16)
            wqt = parts["q"].T.astype(BF16)
            wkt = parts["k"].T.astype(BF16)
            wvt = parts["v"].T.astype(BF16)
            wg = parts["fg"].T.astype(BF16)
            gb = even_b_f[j].reshape(GATE_ROWS, 1)
            wout = even_w_out[j].astype(BF16)
            bs_full = jnp.repeat(gmlp_b_s[j].T, gdim, axis=1)
            ws_dec = jnp.repeat(gmlp_w_s[j][:, 0, 0], gdim).reshape(1, BW)
            bs_dec = jnp.repeat(gmlp_b_s[j][:, 0], gdim).reshape(1, BW)

            y, qt, kt_all, vt_all, lf = _proj(hp, norm_g[l], wnn_p, [wqt, wkt, wvt], wg, gb, n_lin=0,
                                              batch=B, seq=S, tm=tm, layer=j, n_layers=n_even, stacked=qkv_stack)
            qkv_stack = [qt, kt_all, vt_all]
            c = _seq_cumsum(lf)
            att = _fox_prompt(y, qt, vt_all, c, k_col=4 * AW, layer=j, batch=B, seq=S, tile=512)
            hp = _even_mix(att, y, hp, gmlp_ln_g[j], gmlp_ln_b[j], gmlp_w_s[j], bs_full, wout,
                           col0=0, tm=tm, decode=False)[0]
            lp.append(jnp.transpose(lf, (0, 2, 1)))

            y, lf = _proj(hs, norm_g[l], wnn_s, [], wg, gb, n_lin=0, batch=1, seq=DB, tm=DB)
            lf_s = lf[0].T
            k_s, q_s, v_s = y[:, 4 * AW:5 * AW], y[:, 5 * AW:6 * AW], y[:, 6 * AW:7 * AW]
            att = _fox_decode(q_s, k_s, v_s, lf_s, cache_kt, cache_vt, cache_lft, page_table, j,
                              pages_per_step=8)
            hs, vn_s = _even_mix(att, y, hs, gmlp_ln_g[j], gmlp_ln_b[j], ws_dec, bs_dec, wout,
                                 col0=0, tm=DB, decode=True)
            ks.append(k_s.reshape(DB, 1, A_HEADS, A_HEAD_DIM))
            vs.append(v_s.reshape(DB, 1, A_HEADS, A_HEAD_DIM))
            ls.append(lf_s.reshape(DB, 1, A_HEADS))
            chv.append(vn_s.reshape(DB, 1, BW))
        else:
            w = odd_w_in[j]
            wnn = w[:, :3 * CW].astype(BF16)
            wg = w[:, 3 * CW:].T.astype(BF16)
            gb = jnp.concatenate([odd_b_i[j], odd_b_f[j]]).reshape(GATE_ROWS, 1)
            wq = mlstm_w_q[j].astype(BF16)
            wk = mlstm_w_k[j].astype(BF16)
            wkT = jnp.swapaxes(mlstm_w_k[j], 1, 2).astype(BF16)
            wv = mlstm_w_v[j].astype(BF16)
            wout = odd_w_out[j].astype(BF16)
            fin = final_g if last else None

            y, g = _proj(hp, norm_g[l], wnn, [], wg, gb, n_lin=C_HEADS, batch=B, seq=S, tm=tm, cum_chunk=C_CHUNK)
            xconv, q, kt, v = _qkv(y, None, conv_w[j], conv_b[j], wq, wkT, wv, batch=B, seq=S, tm=tm, decode=False)
            b3 = lambda a: a.reshape(B, S, a.shape[1])
            ym, cp_stack, n1, m1 = _mlstm(b3(q), kt, b3(v), g, b3(y), b3(xconv), mlstm_norm_g[j], mlstm_skip[j],
                                          None, cp_stack, layer=j, n_layers=n_odd, decode=False, nb=2)
            hp = _out_proj(ym.reshape(B * S, CW), wout, hp, fin, tm=tm)
            np_.append(n1); mp.append(m1)
            bp.append(y.reshape(B, S, 3 * CW)[:, S - (CONV_W - 1):, :CW])

            y, g = _proj(hs, norm_g[l], wnn, [], wg, gb, n_lin=C_HEADS, batch=1, seq=DB, tm=DB)
            buf = jnp.transpose(state_conv[j], (1, 0, 2))
            xconv, q, k, v, nbuf = _qkv(y, buf, conv_w[j], conv_b[j], wq, wk, wv, batch=1, seq=DB, tm=DB, decode=True)
            kt = jnp.pad(k[:, :, None], ((0, 0), (0, 0), (0, C_CHUNK - 1)))
            g_s = jnp.transpose(g, (2, 1, 0))
            r3 = lambda a: a.reshape(DB, 1, a.shape[1])
            ym, cs_stack, n2, m2 = _mlstm(r3(q), kt, r3(v), g_s, r3(y), r3(xconv), mlstm_norm_g[j], mlstm_skip[j],
                                          (state_c, state_n, state_m), cs_stack, layer=j, n_layers=n_odd,
                                          decode=True, nb=2)
            hs = _out_proj(ym.reshape(DB, CW), wout, hs, fin, tm=DB)
            ns.append(n2); ms.append(m2)
            bs.append(jnp.transpose(nbuf, (1, 0, 2)))
    y_prompt = hp.reshape(B, S, D)
    y_sample = hs.reshape(DB, 1, D)
    to_bshd = lambda t: jnp.transpose(t.reshape(n_even, B, A_HEADS, A_HEAD_DIM, S), (0, 1, 4, 2, 3))
    return (y_prompt, y_sample,
            to_bshd(qkv_stack[1]), to_bshd(qkv_stack[2]), jnp.stack(lp),
            jnp.stack(ks), jnp.stack(vs), jnp.stack(ls), jnp.stack(chv),
            cp_stack, jnp.stack(np_), jnp.stack(mp), jnp.stack(bp),
            cs_stack, jnp.stack(ns), jnp.stack(ms), jnp.stack(bs))
```

```python
import functools

import jax
import jax.numpy as jnp
import numpy as np
from jax import lax
from jax.experimental import pallas as pl
from jax.experimental.pallas import tpu as pltpu

F32 = jnp.float32
BF16 = jnp.bfloat16
EPS = 1e-6
NEG_INF = float("-inf")

A_HEADS = 8
A_HEAD_DIM = 64
A_WIDTH = A_HEADS * A_HEAD_DIM
B_GROUPS = 8
B_CHUNK = 128
C_HEADS = 4
C_CHUNK = 128
CONV_W = 4
LANES = 128
GATE_ROWS = 8
VMEM_LIMIT = 56 * 1024 * 1024


def _cparams(*sem):
    return pltpu.CompilerParams(dimension_semantics=sem, vmem_limit_bytes=VMEM_LIMIT)


def _mm(a, b):
    return jnp.dot(a.astype(BF16), b.astype(BF16), preferred_element_type=F32)


def _mm_nt(a, b):
    return lax.dot_general(a.astype(BF16), b.astype(BF16), (((1,), (1,)), ((), ())),
                           preferred_element_type=F32)


def _log_sigmoid(x):
    return jnp.minimum(x, 0.0) - jnp.log1p(jnp.exp(-jnp.abs(x)))


def _sigmoid(x):
    return 1.0 / (1.0 + jnp.exp(-x))


def _silu(x):
    return x * _sigmoid(x)


def _gelu(x):
    return 0.5 * x * (1.0 + lax.erf(x * np.float32(np.sqrt(0.5))))


def _lane_cumsum(x):
    lane = lax.broadcasted_iota(jnp.int32, x.shape, 1)
    k = 1
    while k < x.shape[1]:
        x = x + jnp.where(lane >= k, pltpu.roll(x, k, 1), 0.0)
        k *= 2
    return x


def _row_to_col(r):
    n = r.shape[1]
    eye = lax.broadcasted_iota(jnp.int32, (n, n), 0) == lax.broadcasted_iota(jnp.int32, (n, n), 1)
    return jnp.sum(jnp.where(eye, r, 0.0), axis=1, keepdims=True)


def _proj_kernel(x_ref, g_ref, wnn_ref, wg_ref, gb_ref, *rest, n_nt, n_alias, n_lin, col_chunk, cum_chunk):
    wnt_refs = rest[:n_nt]
    rest = rest[n_nt + n_alias:]
    y_ref = rest[0]
    yt_refs = rest[1:1 + n_nt]
    gt_ref = rest[1 + n_nt]
    x = x_ref[...]
    xn = x * lax.rsqrt(jnp.mean(x * x, -1, keepdims=True) + EPS) * g_ref[...]
    xb = xn.astype(BF16)
    n_nn = y_ref.shape[1]
    for c in range(0, n_nn, col_chunk):
        y_ref[:, c:c + col_chunk] = jnp.dot(xb, wnn_ref[:, c:c + col_chunk], preferred_element_type=F32)
    for w_ref, o_ref in zip(wnt_refs, yt_refs):
        o_ref[0, 0] = _mm_nt(w_ref[...], xb)
    gt = _mm_nt(wg_ref[...], xb) + gb_ref[...]
    row = lax.broadcasted_iota(jnp.int32, gt.shape, 0)
    gt = jnp.where(row >= n_lin, _log_sigmoid(gt), gt)
    if cum_chunk is None:
        gt_ref[0] = gt
    else:
        upto = (lax.broadcasted_iota(jnp.int32, (cum_chunk, cum_chunk), 0)
                <= lax.broadcasted_iota(jnp.int32, (cum_chunk, cum_chunk), 1)).astype(F32)
        rowc = row[:, :cum_chunk]
        for c in range(0, gt.shape[1], cum_chunk):
            blk = gt[:, c:c + cum_chunk]
            cs = jnp.dot(blk, upto, precision=lax.Precision.HIGHEST, preferred_element_type=F32)
            gt_ref[0, :, c:c + cum_chunk] = jnp.where(rowc >= n_lin, cs, blk)


def _proj(x, g, wnn, wnts, wg, gb, *, n_lin, batch, seq, tm, cum_chunk=None, layer=0, n_layers=1, stacked=None):
    T, D = x.shape
    tps = seq // tm
    n_nn = wnn.shape[1]
    n_nt = len(wnts)
    const = lambda i: (0, 0)
    tok_t = lambda i: (i // tps, 0, i % tps)
    in_specs = [pl.BlockSpec((tm, D), lambda i: (i, 0)),
                pl.BlockSpec((1, D), const),
                pl.BlockSpec((D, n_nn), const),
                pl.BlockSpec((GATE_ROWS, D), const),
                pl.BlockSpec((GATE_ROWS, 1), const)]
    in_specs += [pl.BlockSpec(w.shape, const) for w in wnts]
    args = [x, g.reshape(1, D), wnn, wg, gb, *wnts]
    aliases = {}
    if stacked is not None:
        for k, buf in enumerate(stacked):
            aliases[len(args)] = 1 + k
            in_specs.append(pl.BlockSpec(memory_space=pl.ANY))
            args.append(buf)
    out_shape = [jax.ShapeDtypeStruct((T, n_nn), F32)]
    out_specs = [pl.BlockSpec((tm, n_nn), lambda i: (i, 0))]
    for w in wnts:
        out_shape.append(jax.ShapeDtypeStruct((n_layers, batch, w.shape[0], seq), F32))
        out_specs.append(pl.BlockSpec((1, 1, w.shape[0], tm), lambda i: (layer, i // tps, 0, i % tps)))
    out_shape.append(jax.ShapeDtypeStruct((batch, GATE_ROWS, seq), F32))
    out_specs.append(pl.BlockSpec((1, GATE_ROWS, tm), tok_t))
    return pl.pallas_call(
        functools.partial(_proj_kernel, n_nt=n_nt, n_alias=len(aliases), n_lin=n_lin, col_chunk=512,
                          cum_chunk=cum_chunk),
        grid=(T // tm,), in_specs=in_specs, out_specs=out_specs, out_shape=out_shape,
        input_output_aliases=aliases,
        compiler_params=_cparams("parallel"), name="norm_proj",
    )(*args)


def _cumsum_kernel(x_ref, o_ref):
    S = x_ref.shape[2]
    carry = jnp.zeros((GATE_ROWS, 1), F32)
    for c in range(0, S, LANES):
        inc = _lane_cumsum(x_ref[0, :, c:c + LANES]) + carry
        o_ref[0, :, c:c + LANES] = inc
        carry = inc[:, LANES - 1:LANES]


def _seq_cumsum(x):
    B, R, S = x.shape
    spec = pl.BlockSpec((1, R, S), lambda b: (b, 0, 0))
    return pl.pallas_call(_cumsum_kernel, grid=(B,), in_specs=[spec], out_specs=spec,
                          out_shape=jax.ShapeDtypeStruct(x.shape, F32),
                          compiler_params=_cparams("parallel"), name="logf_cumsum")(x)


def _fox_kernel(qi_ref, kj_ref, qt_ref, k_ref, vt_ref, crow_ref, ccol_ref, o_ref, q_s, m_s, l_s, a_s, *,
                tile, key_chunk):
    qi = qi_ref[pl.program_id(2)]
    kj = kj_ref[pl.program_id(2)]
    hd = A_HEAD_DIM

    @pl.when(kj == 0)
    def _():
        qt = qt_ref[0, 0] * np.float32(hd ** -0.5)
        row = lax.broadcasted_iota(jnp.int32, qt.shape, 0)
        q_s[0] = jnp.where(row < hd, qt, 0.0).astype(BF16)
        q_s[1] = jnp.where(row >= hd, qt, 0.0).astype(BF16)
        m_s[...] = jnp.full(m_s.shape, NEG_INF, F32)
        l_s[...] = jnp.zeros(l_s.shape, F32)
        a_s[...] = jnp.zeros(a_s.shape, F32)

    def step(diagonal):
        cq = crow_ref[0, 0]
        join = lambda old, new, c0: new if c0 == 0 else jnp.concatenate([old[:, :c0], new], axis=1)
        for hh in range(2):
            rows = slice(hh * hd, (hh + 1) * hd)
            m_run, l_run, acc = m_s[hh], l_s[hh], a_s[rows, :]
            for c in range(0, tile, key_chunk):
                c0 = c if diagonal else 0
                kb = k_ref[c:c + key_chunk, :].astype(BF16)
                s = jnp.dot(kb, q_s[hh, :, c0:], preferred_element_type=F32)
                s = s + cq[hh:hh + 1, c0:] - ccol_ref[0, 0, c:c + key_chunk, hh:hh + 1]
                if diagonal:
                    causal = (lax.broadcasted_iota(jnp.int32, s.shape, 0)
                              <= lax.broadcasted_iota(jnp.int32, s.shape, 1))
                    s = jnp.where(causal, s, NEG_INF)
                m_old = m_run[:, c0:]
                m_new = jnp.maximum(m_old, jnp.max(s, 0, keepdims=True))
                alpha = jnp.exp(m_old - m_new)
                p = jnp.exp(s - m_new)
                l_new = alpha * l_run[:, c0:] + jnp.sum(p, 0, keepdims=True)
                vt = vt_ref[0, 0, rows, c:c + key_chunk].astype(BF16)
                a_new = alpha * acc[:, c0:] + jnp.dot(vt, p.astype(BF16), preferred_element_type=F32)
                m_run, l_run, acc = join(m_run, m_new, c0), join(l_run, l_new, c0), join(acc, a_new, c0)
            m_s[hh], l_s[hh] = m_run, l_run
            a_s[rows, :] = acc

    @pl.when(kj < qi)
    def _():
        step(False)

    @pl.when(kj == qi)
    def _():
        step(True)
        row = lax.broadcasted_iota(jnp.int32, (2 * hd, tile), 0)
        out_t = a_s[...] / jnp.where(row < hd, l_s[0], l_s[1])
        o_ref[...] = out_t.T


def _fox_prompt(y, qt, vt, c, *, k_col, layer, batch, seq, tile):
    T = y.shape[0]
    pairs = A_HEADS // 2
    nt = seq // tile
    kb0 = k_col // LANES
    crow = c.reshape(batch, pairs, 2, seq)
    ccol = jnp.swapaxes(crow, 2, 3)
    tri = [(i, j) for i in range(nt) for j in range(i + 1)]
    qi_tab = jnp.asarray([i for i, _ in tri], jnp.int32)
    kj_tab = jnp.asarray([j for _, j in tri], jnp.int32)
    grid_spec = pltpu.PrefetchScalarGridSpec(
        num_scalar_prefetch=2, grid=(batch, pairs, len(tri)),
        in_specs=[pl.BlockSpec((1, 1, LANES, tile), lambda b, p, t, qi, kj: (layer, b, p, qi[t])),
                  pl.BlockSpec((tile, LANES), lambda b, p, t, qi, kj: (b * nt + kj[t], kb0 + p)),
                  pl.BlockSpec((1, 1, LANES, tile), lambda b, p, t, qi, kj: (layer, b, p, kj[t])),
                  pl.BlockSpec((1, 1, 2, tile), lambda b, p, t, qi, kj: (b, p, 0, qi[t])),
                  pl.BlockSpec((1, 1, tile, 2), lambda b, p, t, qi, kj: (b, p, kj[t], 0))],
        out_specs=pl.BlockSpec((tile, LANES), lambda b, p, t, qi, kj: (b * nt + qi[t], p)),
        scratch_shapes=[pltpu.VMEM((2, LANES, tile), BF16), pltpu.VMEM((2, 1, tile), F32),
                        pltpu.VMEM((2, 1, tile), F32), pltpu.VMEM((LANES, tile), F32)])
    return pl.pallas_call(
        functools.partial(_fox_kernel, tile=tile, key_chunk=tile),
        grid_spec=grid_spec, out_shape=jax.ShapeDtypeStruct((T, A_WIDTH), F32),
        compiler_params=_cparams("parallel", "parallel", "arbitrary"),
        name="fox_prompt",
    )(qi_tab, kj_tab, qt, y, vt, crow, ccol)


def _fox_decode_kernel(pt_ref, q_ref, kn_ref, vn_ref, lfn_ref, kc_hbm, vc_hbm, lfc_hbm, o_ref,
                       kbuf, vbuf, lfbuf, sem, m_s, l_s, c_s, a_s, q_s, *, layer, group_pages):
    P = group_pages
    b = pl.program_id(0)
    n_seq = pl.num_programs(0)
    n_pages = pt_ref.shape[1]
    G = n_pages // P
    H, hd = A_HEADS, A_HEAD_DIM
    hsl = [slice(h * hd, (h + 1) * hd) for h in range(H)]

    def group_copies(slot, page_of):
        out = []
        for i in range(P):
            pid = page_of(i)
            out.append(pltpu.make_async_copy(kc_hbm.at[layer, pid], kbuf.at[slot, i], sem.at[0, slot]))
            out.append(pltpu.make_async_copy(vc_hbm.at[layer, pid], vbuf.at[slot, i], sem.at[1, slot]))
            out.append(pltpu.make_async_copy(lfc_hbm.at[layer, pid], lfbuf.at[slot, i], sem.at[2, slot]))
        return out

    def start_group(seq, g, slot):
        for cp in group_copies(slot, lambda i: pt_ref[seq, n_pages - 1 - (g * P + i)]):
            cp.start()

    def wait_group(slot):
        for cp in group_copies(slot, lambda i: 0):
            cp.wait()

    @pl.when(b == 0)
    def _():
        start_group(0, 0, 0)

    qcols = [q_ref[0, hsl[h], :] * np.float32(hd ** -0.5) for h in range(H)]
    for h in range(H):
        q_s[h] = jnp.broadcast_to(qcols[h], (hd, LANES))
    s_new = jnp.concatenate(
        [jnp.sum(qcols[h] * kn_ref[0, hsl[h], :], axis=0, keepdims=True) for h in range(H)], axis=0)
    m_s[...] = s_new
    l_s[...] = jnp.ones(l_s.shape, F32)
    c_s[...] = lfn_ref[0]
    lane = lax.broadcasted_iota(jnp.int32, (hd, LANES), 1)
    for h in range(H):
        a_s[h] = jnp.where(lane == 0, vn_ref[0, hsl[h], :], 0.0)
    later = (lax.broadcasted_iota(jnp.int32, (LANES, LANES), 0)
             > lax.broadcasted_iota(jnp.int32, (LANES, LANES), 1)).astype(F32)

    def group(g, carry):
        slot = g % 2
        more = g + 1 < G
        nxt_seq = jnp.where(more, b, b + 1)

        @pl.when(nxt_seq < n_seq)
        def _():
            start_group(nxt_seq, jnp.where(more, g + 1, 0), 1 - slot)

        wait_group(slot)
        c = c_s[...]
        lf_all = jnp.concatenate([lfbuf[slot, i] for i in range(P)], axis=0)
        excl_all = jnp.dot(lf_all, later, precision=lax.Precision.HIGHEST, preferred_element_type=F32)
        tot_all = jnp.sum(lf_all, -1, keepdims=True)
        s_pages = []
        for i in range(P):
            rows = [jnp.sum(kbuf[slot, i, h] * q_s[h], axis=0, keepdims=True) for h in range(H)]
            s_pages.append(jnp.concatenate(rows, axis=0) + (c + excl_all[i * H:(i + 1) * H, :]))
            c = c + tot_all[i * H:(i + 1) * H, :]
        c_s[...] = c
        s_all = jnp.concatenate(s_pages, axis=1)
        m_old = m_s[...]
        m_new = jnp.maximum(m_old, jnp.max(s_all, -1, keepdims=True))
        alpha = jnp.exp(m_old - m_new)
        p_all = jnp.exp(s_all - m_new)
        l_s[...] = alpha * l_s[...] + jnp.sum(p_all, -1, keepdims=True)
        m_s[...] = m_new
        for h in range(H):
            acc = a_s[h] * alpha[h:h + 1, :]
            for i in range(P):
                acc = acc + p_all[h:h + 1, i * LANES:(i + 1) * LANES] * vbuf[slot, i, h]
            a_s[h] = acc
        return carry

    lax.fori_loop(0, G, group, 0)
    l = l_s[...]
    for h in range(H):
        o_ref[0, hsl[h], :] = jnp.sum(a_s[h], axis=1, keepdims=True) / l[h:h + 1, :]


def _fox_decode(q, k_new, v_new, lf_new, cache_kt, cache_vt, cache_lft, page_table, layer, *, pages_per_step):
    DB, n_pages = page_table.shape
    P = pages_per_step
    assert n_pages % (2 * P) == 0, "an even number of page groups keeps the buffer parity fixed per sequence"
    col = lambda a: a.reshape(DB, a.shape[1], 1)
    vec = lambda n: pl.BlockSpec((1, n, 1), lambda b, pt: (b, 0, 0))
    hbm = pl.BlockSpec(memory_space=pl.ANY)
    page = (A_HEADS, A_HEAD_DIM, LANES)
    grid_spec = pltpu.PrefetchScalarGridSpec(
        num_scalar_prefetch=1, grid=(DB,),
        in_specs=[vec(A_WIDTH), vec(A_WIDTH), vec(A_WIDTH), vec(A_HEADS), hbm, hbm, hbm],
        out_specs=vec(A_WIDTH),
        scratch_shapes=[pltpu.VMEM((2, P) + page, F32), pltpu.VMEM((2, P) + page, F32),
                        pltpu.VMEM((2, P, A_HEADS, LANES), F32),
                        pltpu.SemaphoreType.DMA((3, 2)),
                        pltpu.VMEM((A_HEADS, 1), F32), pltpu.VMEM((A_HEADS, 1), F32),
                        pltpu.VMEM((A_HEADS, 1), F32),
                        pltpu.VMEM(page, F32), pltpu.VMEM(page, F32)])
    out = pl.pallas_call(
        functools.partial(_fox_decode_kernel, layer=layer, group_pages=P),
        grid_spec=grid_spec, out_shape=jax.ShapeDtypeStruct((DB, A_WIDTH, 1), F32),
        compiler_params=_cparams("arbitrary"), name="fox_decode",
    )(page_table, col(q), col(k_new), col(v_new), col(lf_new), cache_kt, cache_vt, cache_lft)
    return out.reshape(DB, A_WIDTH)


def _even_mix_kernel(att_ref, za_ref, u_ref, vb_ref, zb_ref, h_ref, lng_ref, lnb_ref, ws_ref, bs_ref,
                     wout_ref, *outs, decode):
    o_ref = outs[0]
    ya = att_ref[...] * _silu(za_ref[...])
    u = _gelu(u_ref[...])
    vf = _gelu(vb_ref[...])
    mu = jnp.mean(vf, -1, keepdims=True)
    var = jnp.mean((vf - mu) ** 2, -1, keepdims=True)
    vn = (vf - mu) * lax.rsqrt(var + EPS) * lng_ref[...] + lnb_ref[...]
    tm, bw = vn.shape
    if decode:
        outs[1][...] = vn
        mix = vn * ws_ref[...] + bs_ref[...]
    else:
        lane = lax.broadcasted_iota(jnp.int32, (B_CHUNK, LANES), 1)
        tri = (lax.broadcasted_iota(jnp.int32, (B_CHUNK, B_CHUNK), 0)
               >= lax.broadcasted_iota(jnp.int32, (B_CHUNK, B_CHUNK), 1))
        wtril = [jnp.where(tri, ws_ref[g], 0.0).astype(BF16) for g in range(B_GROUPS)]
        gpl = LANES // (bw // B_GROUPS)
        rows = []
        for c in range(0, tm, B_CHUNK):
            blocks = []
            for lb in range(bw // LANES):
                vblk = vn[c:c + B_CHUNK, lb * LANES:(lb + 1) * LANES].astype(BF16)
                y0 = jnp.dot(wtril[lb * gpl], vblk, preferred_element_type=F32)
                y1 = jnp.dot(wtril[lb * gpl + 1], vblk, preferred_element_type=F32)
                blocks.append(jnp.where(lane < LANES // gpl, y0, y1))
            rows.append(jnp.concatenate(blocks, axis=1) + bs_ref[...])
        mix = jnp.concatenate(rows, axis=0)
    yb = u * mix * _silu(zb_ref[...])
    aw = ya.shape[1]
    hn = h_ref[...] + _mm(ya, wout_ref[:aw, :]) + _mm(yb, wout_ref[aw:, :])
    o_ref[...] = hn


def _even_mix(att, y, h, ln_g, ln_b, ws, bs, wout, *, col0, tm, decode):
    T, D = h.shape
    aw = att.shape[1]
    const2 = lambda i: (0, 0)
    yblk = lambda k: pl.BlockSpec((tm, aw), lambda i: (i, col0 + k))
    ws_spec = (pl.BlockSpec(ws.shape, const2) if decode else pl.BlockSpec(ws.shape, lambda i: (0, 0, 0)))
    in_specs = [pl.BlockSpec((tm, aw), lambda i: (i, 0)), yblk(0), yblk(1), yblk(2), yblk(3),
                pl.BlockSpec((tm, D), lambda i: (i, 0)),
                pl.BlockSpec((1, aw), const2), pl.BlockSpec((1, aw), const2),
                ws_spec, pl.BlockSpec(bs.shape, const2), pl.BlockSpec(wout.shape, const2)]
    out_shape = [jax.ShapeDtypeStruct((T, D), F32)]
    out_specs = [pl.BlockSpec((tm, D), lambda i: (i, 0))]
    if decode:
        out_shape.append(jax.ShapeDtypeStruct((T, aw), F32))
        out_specs.append(pl.BlockSpec((tm, aw), lambda i: (i, 0)))
    return pl.pallas_call(
        functools.partial(_even_mix_kernel, decode=decode),
        grid=(T // tm,), in_specs=in_specs, out_specs=out_specs, out_shape=out_shape,
        compiler_params=_cparams("parallel"), name="even_mix",
    )(att, y, y, y, y, h, ln_g.reshape(1, aw), ln_b.reshape(1, aw), ws, bs, wout)


def _qkv_decode_kernel(xc_ref, prev_ref, cw_ref, cb_ref, wq_ref, wk_ref, wv_ref,
                       xconv_ref, q_ref, k_ref, v_ref, nb_ref):
    xc = xc_ref[...]
    hd = xc.shape[1] // C_HEADS
    acc = cb_ref[...] + cw_ref[CONV_W - 1:CONV_W, :] * xc
    for j in range(CONV_W - 1):
        acc = acc + cw_ref[j:j + 1, :] * prev_ref[j]
    for j in range(CONV_W - 2):
        nb_ref[j] = prev_ref[j + 1]
    nb_ref[CONV_W - 2] = xc
    xconv = _silu(acc)
    xconv_ref[...] = xconv
    for h in range(C_HEADS):
        sl = slice(h * hd, (h + 1) * hd)
        q_ref[:, sl] = _mm(xconv[:, sl], wq_ref[h]) * np.float32(hd ** -0.5)
        k_ref[:, sl] = _mm(xconv[:, sl], wk_ref[h])
        v_ref[:, sl] = _mm(xc[:, sl], wv_ref[h])


def _qkv_decode(y, prev, cw, cb, wq, wk, wv):
    T = y.shape[0]
    W = cw.shape[1]
    const2 = lambda i: (0, 0)
    const3 = lambda i: (0, 0, 0)
    row_blk = pl.BlockSpec((T, W), lambda i: (0, 0))
    row_out = jax.ShapeDtypeStruct((T, W), F32)
    return pl.pallas_call(
        _qkv_decode_kernel, grid=(1,),
        in_specs=[row_blk, pl.BlockSpec(prev.shape, const3), pl.BlockSpec(cw.shape, const2),
                  pl.BlockSpec((1, W), const2), pl.BlockSpec(wq.shape, const3), pl.BlockSpec(wk.shape, const3),
                  pl.BlockSpec(wv.shape, const3)],
        out_specs=[row_blk, row_blk, row_blk, row_blk, pl.BlockSpec(prev.shape, const3)],
        out_shape=[row_out, row_out, row_out, row_out, jax.ShapeDtypeStruct(prev.shape, F32)],
        compiler_params=_cparams("arbitrary"), name="conv_qkv_decode",
    )(y, prev, cw, cb.reshape(1, W), wq, wk, wv)


def _mlstm_decode_kernel(q_ref, kt_ref, v_ref, g_ref, o_ref, z_ref, xconv_ref, ng_ref, skip_ref,
                         c0_ref, n0_ref, m0_ref, *rest, nb, n_alias):
    y_ref, c_out, n_out, m_out = rest[n_alias:]
    L = kt_ref.shape[2]
    W = kt_ref.shape[1]
    hd = W // C_HEADS
    tri = (lax.broadcasted_iota(jnp.int32, (L, L), 0) >= lax.broadcasted_iota(jnp.int32, (L, L), 1))
    one_col = (lax.broadcasted_iota(jnp.int32, (L, LANES), 1) == 0).astype(F32)
    lane_l = lax.broadcasted_iota(jnp.int32, (GATE_ROWS, L), 1)
    grow = lax.broadcasted_iota(jnp.int32, (GATE_ROWS, L), 0)
    first = lax.broadcasted_iota(jnp.int32, (L, W), 0) == 0
    lane_n = lax.broadcasted_iota(jnp.int32, (hd, LANES), 1)
    for b in range(nb):
        gates = jnp.where((grow < C_HEADS) & (lane_l >= 1), NEG_INF, jnp.broadcast_to(g_ref[b], (GATE_ROWS, L)))
        expand = lambda ref: jnp.where(first, jnp.broadcast_to(ref[b], (L, W)), 0.0)
        q_all, v_all = expand(q_ref), expand(v_ref)
        o_all, z_all, xc_all = expand(o_ref), expand(z_ref), expand(xconv_ref)
        ys = []
        for h in range(C_HEADS):
            sl = slice(h * hd, (h + 1) * hd)
            ncol = jnp.concatenate([_row_to_col(n0_ref[0, b, h:h + 1, k:k + LANES])
                                    for k in range(0, hd, LANES)], axis=0)
            caug = jnp.concatenate([c0_ref[0, b, h], jnp.where(lane_n == 0, ncol, 0.0)], axis=1)
            hc, caug_new, m_new = _mlstm_head(
                q_all[:, sl], kt_ref[b, sl, :], v_all[:, sl], gates[h:h + 1, :],
                gates[C_HEADS + h:C_HEADS + h + 1, :], m0_ref[0, b, h:h + 1, :], caug, tri, one_col)
            c_out[0, b, h] = caug_new[:, :hd]
            n_out[b, h] = caug_new[:, hd:hd + 1]
            m_out[b, h:h + 1, :] = m_new
            ys.append(_mlstm_gate(hc, o_all[:, sl], z_all[:, sl], xc_all[:, sl], ng_ref[:, sl], skip_ref[:, sl]))
        y_ref[b] = jnp.concatenate(ys, axis=1)[0:1]


def _mlstm_decode(q, kt, v, gates, y_in, xconv, ng, skip, state, c_stack, *, layer, n_layers, nb):
    batch, W = kt.shape[0], kt.shape[1]
    hd = W // C_HEADS
    const2 = lambda b: (0, 0)
    rowblk = lambda k: pl.BlockSpec((nb, 1, W), lambda b: (b, 0, k))
    c0, n0, m0 = state
    in_specs = [rowblk(0), pl.BlockSpec((nb, W, C_CHUNK), lambda b: (b, 0, 0)), rowblk(0),
                pl.BlockSpec((nb, GATE_ROWS, 1), lambda b: (b, 0, 0)),
                rowblk(2), rowblk(1), rowblk(0),
                pl.BlockSpec((1, W), const2), pl.BlockSpec((1, W), const2),
                pl.BlockSpec((1, nb, C_HEADS, hd, hd), lambda b: (layer, b, 0, 0, 0)),
                pl.BlockSpec((1, nb, C_HEADS, hd), lambda b: (layer, b, 0, 0)),
                pl.BlockSpec((1, nb, C_HEADS, 1), lambda b: (layer, b, 0, 0))]
    args = [q, kt, v, gates, y_in, y_in, xconv, ng.reshape(1, W), skip.reshape(1, W),
            c0, n0, m0.reshape(m0.shape + (1,))]
    aliases = {}
    if c_stack is not None:
        aliases[len(args)] = 1
        in_specs.append(pl.BlockSpec(memory_space=pl.ANY))
        args.append(c_stack)
    out_shape = [jax.ShapeDtypeStruct((batch, 1, W), F32),
                 jax.ShapeDtypeStruct((n_layers, batch, C_HEADS, hd, hd), F32),
                 jax.ShapeDtypeStruct((batch, C_HEADS, hd, 1), F32),
                 jax.ShapeDtypeStruct((batch, C_HEADS, 1), F32)]
    out_specs = [pl.BlockSpec((nb, 1, W), lambda b: (b, 0, 0)),
                 pl.BlockSpec((1, nb, C_HEADS, hd, hd), lambda b: (layer, b, 0, 0, 0)),
                 pl.BlockSpec((nb, C_HEADS, hd, 1), lambda b: (b, 0, 0, 0)),
                 pl.BlockSpec((nb, C_HEADS, 1), lambda b: (b, 0, 0))]
    y, c_new, n_new, m_new = pl.pallas_call(
        functools.partial(_mlstm_decode_kernel, nb=nb, n_alias=len(aliases)),
        grid=(batch // nb,), in_specs=in_specs, out_specs=out_specs, out_shape=out_shape,
        input_output_aliases=aliases,
        compiler_params=_cparams("parallel"), name="mlstm_decode",
    )(*args)
    return y, c_new, n_new.reshape(batch, C_HEADS, hd), m_new.reshape(batch, C_HEADS)


def _mlstm_head(qh, kth, vh, i_row, b_row, m_prev, caug, tri, one_col):
    L, hd = qh.shape
    g_row = i_row - b_row
    dm = jnp.where(tri, g_row, NEG_INF)
    mcol = jnp.maximum(m_prev, jnp.max(dm, -1, keepdims=True))
    wmat = jnp.exp(dm - mcol)
    a = jnp.exp(m_prev - mcol)
    sm = wmat * _mm(qh, kth)
    qc = _mm(qh, caug)
    num = _mm(sm, vh) + a * qc[:, :hd]
    den = jnp.sum(sm, -1, keepdims=True) + a * qc[:, hd:hd + 1]
    den = jnp.maximum(jnp.abs(den), jnp.exp(-(_row_to_col(b_row) + mcol)))
    m_last = mcol[L - 1:L, :]
    wl = jnp.exp(g_row - m_last)
    a_l = jnp.exp(m_prev - m_last)
    caug_new = a_l * caug + _mm(kth * wl, jnp.concatenate([vh, one_col], axis=1))
    return num / den, caug_new, b_row[:, L - 1:L] + m_last


def _mlstm_gate(hc, o, z, xconv, ng, skip):
    mu = jnp.mean(hc, -1, keepdims=True)
    var = jnp.mean((hc - mu) ** 2, -1, keepdims=True)
    hn = (hc - mu) * lax.rsqrt(var + EPS) * ng
    return (_sigmoid(o) * hn + skip * xconv) * _silu(z)


def _mlstm_prompt_kernel(xc_ref, halo_ref, z_ref, o_ref, g_ref, h_ref, cw_ref, cb_ref, wq_ref, wkt_ref, wv_ref,
                         ng_ref, skip_ref, wout_ref, *rest, nb, n_alias, final):
    if final:
        fg_ref = rest[0]
        rest = rest[1:]
    hout_ref, c_out, n_out, m_out, caug_s, m_s = rest[n_alias:]
    L, W = xc_ref.shape[1], xc_ref.shape[2]
    hd = W // C_HEADS
    ci = pl.program_id(1)

    @pl.when(ci == 0)
    def _():
        caug_s[...] = jnp.zeros(caug_s.shape, F32)
        m_s[...] = jnp.zeros(m_s.shape, F32)

    xcs, xconvs = [], []
    for b in range(nb):
        xc = xc_ref[b]
        halo = jnp.where(ci == 0, 0.0, halo_ref[b])
        xx = jnp.concatenate([halo, xc], axis=0)
        acc = cb_ref[...] + cw_ref[CONV_W - 1:CONV_W, :] * xc
        for k in range(1, CONV_W):
            acc = acc + cw_ref[CONV_W - 1 - k:CONV_W - k, :] * pltpu.roll(xx, k, 0)[8:]
        xcs.append(xc)
        xconvs.append(_silu(acc))
    xc_all = jnp.concatenate(xcs, axis=0)
    xconv_all = jnp.concatenate(xconvs, axis=0)
    tri = (lax.broadcasted_iota(jnp.int32, (L, L), 0) >= lax.broadcasted_iota(jnp.int32, (L, L), 1))
    one_col = (lax.broadcasted_iota(jnp.int32, (L, LANES), 1) == 0).astype(F32)
    ys = [[None] * C_HEADS for _ in range(nb)]
    for h in range(C_HEADS):
        sl = slice(h * hd, (h + 1) * hd)
        xh = xconv_all[:, sl]
        q_h = _mm(xh, wq_ref[h]) * np.float32(hd ** -0.5)
        kt_h = _mm_nt(wkt_ref[h], xh)
        v_h = _mm(xc_all[:, sl], wv_ref[h])
        for b in range(nb):
            r = slice(b * L, (b + 1) * L)
            gates = g_ref[b]
            hc, caug_new, m_new = _mlstm_head(
                q_h[r], kt_h[:, r], v_h[r], gates[h:h + 1, :], gates[C_HEADS + h:C_HEADS + h + 1, :],
                m_s[b, h][0:1, 0:1], caug_s[b, h], tri, one_col)
            caug_s[b, h] = caug_new
            m_s[b, h] = jnp.broadcast_to(m_new, (8, LANES))
            ys[b][h] = _mlstm_gate(hc, o_ref[b][:, sl], z_ref[b][:, sl], xconv_all[r, sl],
                                   ng_ref[:, sl], skip_ref[:, sl])
    y_all = jnp.concatenate([jnp.concatenate(ys[b], axis=1) for b in range(nb)], axis=0)
    hn = jnp.concatenate([h_ref[b] for b in range(nb)], axis=0) + _mm(y_all, wout_ref[...])
    if final:
        hn = hn * lax.rsqrt(jnp.mean(hn * hn, -1, keepdims=True) + EPS) * fg_ref[...]
    for b in range(nb):
        hout_ref[b] = hn[b * L:(b + 1) * L]

    @pl.when(ci == pl.num_programs(1) - 1)
    def _():
        for b in range(nb):
            for h in range(C_HEADS):
                c_out[0, b, h] = caug_s[b, h, :, :hd]
                n_out[b, h] = caug_s[b, h, :, hd:hd + 1]
                m_out[b, h:h + 1, :] = m_s[b, h][0:1, 0:1]


def _mlstm_prompt(y_in, gates, h, cw, cb, wq, wkt, wv, ng, skip, wout, final_g, c_stack, *, layer, n_layers, nb):
    batch, S = y_in.shape[0], y_in.shape[1]
    W, D = wout.shape
    hd = W // C_HEADS
    L = C_CHUNK
    const2 = lambda b, c: (0, 0)
    const3 = lambda b, c: (0, 0, 0)
    yblk = lambda k: pl.BlockSpec((nb, L, W), lambda b, c: (b, c, k))
    in_specs = [yblk(0),
                pl.BlockSpec((nb, 8, W), lambda b, c: (b, jnp.maximum(c * (L // 8) - 1, 0), 0)),
                yblk(1), yblk(2),
                pl.BlockSpec((nb, GATE_ROWS, L), lambda b, c: (b, 0, c)),
                pl.BlockSpec((nb, L, D), lambda b, c: (b, c, 0)),
                pl.BlockSpec(cw.shape, const2), pl.BlockSpec((1, W), const2),
                pl.BlockSpec(wq.shape, const3), pl.BlockSpec(wkt.shape, const3), pl.BlockSpec(wv.shape, const3),
                pl.BlockSpec((1, W), const2), pl.BlockSpec((1, W), const2), pl.BlockSpec(wout.shape, const2)]
    args = [y_in, y_in, y_in, y_in, gates, h, cw, cb.reshape(1, W), wq, wkt, wv,
            ng.reshape(1, W), skip.reshape(1, W), wout]
    if final_g is not None:
        in_specs.append(pl.BlockSpec((1, D), const2))
        args.append(final_g.reshape(1, D))
    aliases = {}
    if c_stack is not None:
        aliases[len(args)] = 1
        in_specs.append(pl.BlockSpec(memory_space=pl.ANY))
        args.append(c_stack)
    out_shape = [jax.ShapeDtypeStruct((batch, S, D), F32),
                 jax.ShapeDtypeStruct((n_layers, batch, C_HEADS, hd, hd), F32),
                 jax.ShapeDtypeStruct((batch, C_HEADS, hd, 1), F32),
                 jax.ShapeDtypeStruct((batch, C_HEADS, 1), F32)]
    out_specs = [pl.BlockSpec((nb, L, D), lambda b, c: (b, c, 0)),
                 pl.BlockSpec((1, nb, C_HEADS, hd, hd), lambda b, c: (layer, b, 0, 0, 0)),
                 pl.BlockSpec((nb, C_HEADS, hd, 1), lambda b, c: (b, 0, 0, 0)),
                 pl.BlockSpec((nb, C_HEADS, 1), lambda b, c: (b, 0, 0))]
    hout, c_new, n_new, m_new = pl.pallas_call(
        functools.partial(_mlstm_prompt_kernel, nb=nb, n_alias=len(aliases), final=final_g is not None),
        grid=(batch // nb, S // L), in_specs=in_specs, out_specs=out_specs, out_shape=out_shape,
        input_output_aliases=aliases,
        scratch_shapes=[pltpu.VMEM((nb, C_HEADS, hd, hd + LANES), F32), pltpu.VMEM((nb, C_HEADS, 8, LANES), F32)],
        compiler_params=_cparams("parallel", "arbitrary"), name="mlstm_layer",
    )(*args)
    return hout, c_new, n_new.reshape(batch, C_HEADS, hd), m_new.reshape(batch, C_HEADS)


def _out_proj_kernel(y_ref, w_ref, h_ref, *rest, final):
    hn = h_ref[...] + _mm(y_ref[...], w_ref[...])
    if final:
        fg_ref, o_ref = rest
        o_ref[...] = hn * lax.rsqrt(jnp.mean(hn * hn, -1, keepdims=True) + EPS) * fg_ref[...]
    else:
        rest[0][...] = hn


def _out_proj(y, w, h, final_g, *, tm):
    T, D = h.shape
    K = y.shape[1]
    const2 = lambda i: (0, 0)
    in_specs = [pl.BlockSpec((tm, K), lambda i: (i, 0)), pl.BlockSpec(w.shape, const2),
                pl.BlockSpec((tm, D), lambda i: (i, 0))]
    args = [y, w, h]
    if final_g is not None:
        in_specs.append(pl.BlockSpec((1, D), const2))
        args.append(final_g.reshape(1, D))
    return pl.pallas_call(
        functools.partial(_out_proj_kernel, final=final_g is not None),
        grid=(T // tm,), in_specs=in_specs, out_specs=pl.BlockSpec((tm, D), lambda i: (i, 0)),
        out_shape=jax.ShapeDtypeStruct((T, D), F32),
        compiler_params=_cparams("parallel"), name="out_proj",
    )(*args)


def kernel(x_prompt, x_sample, cache_k, cache_v, cache_logf, state_c, state_n, state_m, state_conv, page_table,
           norm_g, final_g, even_w_in, even_b_f, gmlp_ln_g, gmlp_ln_b, gmlp_w_s, gmlp_b_s, even_w_out,
           odd_w_in, odd_b_i, odd_b_f, conv_w, conv_b, mlstm_w_q, mlstm_w_k, mlstm_w_v, mlstm_norm_g,
           mlstm_skip, odd_w_out):
    B, S, D = x_prompt.shape
    DB = x_sample.shape[0]
    depth = norm_g.shape[0]
    AW = A_WIDTH
    BW = gmlp_ln_g.shape[1]
    CW = conv_w.shape[2]
    gdim = BW // B_GROUPS
    tm = 512

    hp = x_prompt.reshape(B * S, D)
    hs = x_sample.reshape(DB, D)
    cache_kt = jnp.transpose(cache_k, (0, 1, 3, 4, 2))
    cache_vt = jnp.transpose(cache_v, (0, 1, 3, 4, 2))
    cache_lft = jnp.transpose(cache_logf, (0, 1, 3, 2))

    lp, ks, vs, ls, chv = [], [], [], [], []
    np_, mp, bp, ns, ms, bs = [], [], [], [], [], []
    n_even, n_odd = (depth + 1) // 2, depth // 2
    assert depth % 2 == 0, "the final rmsnorm is fused into the last (mLSTM) layer's output projection"
    qkv_stack = cp_stack = cs_stack = None
    for l in range(depth):
        j = l // 2
        last = l == depth - 1
        if l % 2 == 0:
            w = even_w_in[j]
            o = 0
            parts = {}
            for name, n in (("q", AW), ("k", AW), ("v", AW), ("fg", A_HEADS), ("za", AW), ("u", BW), ("vb", BW), ("zb", BW)):
                parts[name] = w[:, o:o + n]
                o += n
            wnn_p = jnp.concatenate([parts[n] for n in ("za", "u", "vb", "zb", "k")], 1).astype(BF16)
            wnn_s = jnp.concatenate([parts[n] for n in ("za", "u", "vb", "zb", "k", "q", "v")], 1).astype(BF16)
            wqt = parts["q"].T.astype(BF16)
            wkt = parts["k"].T.astype(BF16)
            wvt = parts["v"].T.astype(BF16)
            wg = parts["fg"].T.astype(BF16)
            gb = even_b_f[j].reshape(GATE_ROWS, 1)
            wout = even_w_out[j].astype(BF16)
            bs_full = jnp.repeat(gmlp_b_s[j].T, gdim, axis=1)
            ws_dec = jnp.repeat(gmlp_w_s[j][:, 0, 0], gdim).reshape(1, BW)
            bs_dec = jnp.repeat(gmlp_b_s[j][:, 0], gdim).reshape(1, BW)

            y, qt, kt_all, vt_all, lf = _proj(hp, norm_g[l], wnn_p, [wqt, wkt, wvt], wg, gb, n_lin=0,
                                              batch=B, seq=S, tm=tm, layer=j, n_layers=n_even, stacked=qkv_stack)
            qkv_stack = [qt, kt_all, vt_all]
            c = _seq_cumsum(lf)
            att = _fox_prompt(y, qt, vt_all, c, k_col=4 * AW, layer=j, batch=B, seq=S, tile=512)
            hp = _even_mix(att, y, hp, gmlp_ln_g[j], gmlp_ln_b[j], gmlp_w_s[j], bs_full, wout,
                           col0=0, tm=tm, decode=False)[0]
            lp.append(jnp.transpose(lf, (0, 2, 1)))

            y, lf = _proj(hs, norm_g[l], wnn_s, [], wg, gb, n_lin=0, batch=1, seq=DB, tm=DB)
            lf_s = lf[0].T
            k_s, q_s, v_s = y[:, 4 * AW:5 * AW], y[:, 5 * AW:6 * AW], y[:, 6 * AW:7 * AW]
            att = _fox_decode(q_s, k_s, v_s, lf_s, cache_kt, cache_vt, cache_lft, page_table, j,
                              pages_per_step=8)
            hs, vn_s = _even_mix(att, y, hs, gmlp_ln_g[j], gmlp_ln_b[j], ws_dec, bs_dec, wout,
                                 col0=0, tm=DB, decode=True)
            ks.append(k_s.reshape(DB, 1, A_HEADS, A_HEAD_DIM))
            vs.append(v_s.reshape(DB, 1, A_HEADS, A_HEAD_DIM))
            ls.append(lf_s.reshape(DB, 1, A_HEADS))
            chv.append(vn_s.reshape(DB, 1, BW))
        else:
            w = odd_w_in[j]
            wnn = w[:, :3 * CW].astype(BF16)
            wg = w[:, 3 * CW:].T.astype(BF16)
            gb = jnp.concatenate([odd_b_i[j], odd_b_f[j]]).reshape(GATE_ROWS, 1)
            wq = mlstm_w_q[j].astype(BF16)
            wk = mlstm_w_k[j].astype(BF16)
            wkT = jnp.swapaxes(mlstm_w_k[j], 1, 2).astype(BF16)
            wv = mlstm_w_v[j].astype(BF16)
            wout = odd_w_out[j].astype(BF16)
            fin = final_g if last else None

            y, g = _proj(hp, norm_g[l], wnn, [], wg, gb, n_lin=C_HEADS, batch=B, seq=S, tm=tm, cum_chunk=C_CHUNK)
            hout, cp_stack, n1, m1 = _mlstm_prompt(
                y.reshape(B, S, 3 * CW), g, hp.reshape(B, S, D), conv_w[j], conv_b[j], wq, wkT, wv,
                mlstm_norm_g[j], mlstm_skip[j], wout, fin, cp_stack, layer=j, n_layers=n_odd, nb=2)
            hp = hout.reshape(B * S, D)
            np_.append(n1); mp.append(m1)
            bp.append(y.reshape(B, S, 3 * CW)[:, S - (CONV_W - 1):, :CW])

            y, g = _proj(hs, norm_g[l], wnn, [], wg, gb, n_lin=C_HEADS, batch=1, seq=DB, tm=DB)
            buf = jnp.transpose(state_conv[j], (1, 0, 2))
            xconv, q, k, v, nbuf = _qkv_decode(y, buf, conv_w[j], conv_b[j], wq, wk, wv)
            kt = jnp.pad(k[:, :, None], ((0, 0), (0, 0), (0, C_CHUNK - 1)))
            g_s = jnp.transpose(g, (2, 1, 0))
            r3 = lambda a: a.reshape(DB, 1, a.shape[1])
            ym, cs_stack, n2, m2 = _mlstm_decode(r3(q), kt, r3(v), g_s, r3(y), r3(xconv), mlstm_norm_g[j],
                                                 mlstm_skip[j], (state_c, state_n, state_m), cs_stack,
                                                 layer=j, n_layers=n_odd, nb=2)
            hs = _out_proj(ym.reshape(DB, CW), wout, hs, fin, tm=DB)
            ns.append(n2); ms.append(m2)
            bs.append(jnp.transpose(nbuf, (1, 0, 2)))
    y_prompt = hp.reshape(B, S, D)
    y_sample = hs.reshape(DB, 1, D)
    to_bshd = lambda t: jnp.transpose(t.reshape(n_even, B, A_HEADS, A_HEAD_DIM, S), (0, 1, 4, 2, 3))
    return (y_prompt, y_sample,
            to_bshd(qkv_stack[1]), to_bshd(qkv_stack[2]), jnp.stack(lp),
            jnp.stack(ks), jnp.stack(vs), jnp.stack(ls), jnp.stack(chv),
            cp_stack, jnp.stack(np_), jnp.stack(mp), jnp.stack(bp),
            cs_stack, jnp.stack(ns), jnp.stack(ms), jnp.stack(bs))
```

```python
import functools

import jax
import jax.numpy as jnp
import numpy as np
from jax import lax
from jax.experimental import pallas as pl
from jax.experimental.pallas import tpu as pltpu

F32 = jnp.float32
BF16 = jnp.bfloat16
EPS = 1e-6
NEG_INF = float("-inf")

A_HEADS = 8
A_HEAD_DIM = 64
A_WIDTH = A_HEADS * A_HEAD_DIM
B_GROUPS = 8
B_CHUNK = 128
C_HEADS = 4
C_CHUNK = 128
CONV_W = 4
LANES = 128
GATE_ROWS = 8
VMEM_LIMIT = 56 * 1024 * 1024


def _cparams(*sem):
    return pltpu.CompilerParams(dimension_semantics=sem, vmem_limit_bytes=VMEM_LIMIT)


def _mm(a, b):
    return jnp.dot(a.astype(BF16), b.astype(BF16), preferred_element_type=F32)


def _mm_nt(a, b):
    return lax.dot_general(a.astype(BF16), b.astype(BF16), (((1,), (1,)), ((), ())),
                           preferred_element_type=F32)


def _log_sigmoid(x):
    return jnp.minimum(x, 0.0) - jnp.log1p(jnp.exp(-jnp.abs(x)))


def _sigmoid(x):
    return 1.0 / (1.0 + jnp.exp(-x))


def _silu(x):
    return x * _sigmoid(x)


def _gelu(x):
    return 0.5 * x * (1.0 + lax.erf(x * np.float32(np.sqrt(0.5))))


def _row_to_col(r):
    n = r.shape[1]
    eye = lax.broadcasted_iota(jnp.int32, (n, n), 0) == lax.broadcasted_iota(jnp.int32, (n, n), 1)
    return jnp.sum(jnp.where(eye, r, 0.0), axis=1, keepdims=True)


def _proj_kernel(x_ref, g_ref, wnn_ref, wg_ref, gb_ref, *rest, n_nt, n_alias, n_lin, col_chunk, cum_chunk):
    wnt_refs = rest[:n_nt]
    rest = rest[n_nt + n_alias:]
    y_ref = rest[0]
    yt_refs = rest[1:1 + n_nt]
    gt_ref = rest[1 + n_nt]
    x = x_ref[...]
    xn = x * lax.rsqrt(jnp.mean(x * x, -1, keepdims=True) + EPS) * g_ref[...]
    xb = xn.astype(BF16)
    n_nn = y_ref.shape[1]
    for c in range(0, n_nn, col_chunk):
        y_ref[:, c:c + col_chunk] = jnp.dot(xb, wnn_ref[:, c:c + col_chunk], preferred_element_type=F32)
    for w_ref, o_ref in zip(wnt_refs, yt_refs):
        o_ref[0, 0] = _mm_nt(w_ref[...], xb)
    gt = _mm_nt(wg_ref[...], xb) + gb_ref[...]
    row = lax.broadcasted_iota(jnp.int32, gt.shape, 0)
    gt = jnp.where(row >= n_lin, _log_sigmoid(gt), gt)
    if cum_chunk is None:
        gt_ref[0] = gt
    else:
        upto = (lax.broadcasted_iota(jnp.int32, (cum_chunk, cum_chunk), 0)
                <= lax.broadcasted_iota(jnp.int32, (cum_chunk, cum_chunk), 1)).astype(F32)
        rowc = row[:, :cum_chunk]
        for c in range(0, gt.shape[1], cum_chunk):
            blk = gt[:, c:c + cum_chunk]
            cs = jnp.dot(blk, upto, precision=lax.Precision.HIGHEST, preferred_element_type=F32)
            gt_ref[0, :, c:c + cum_chunk] = jnp.where(rowc >= n_lin, cs, blk)


def _proj(x, g, wnn, wnts, wg, gb, *, n_lin, batch, seq, tm, n_nn=None, cum_chunk=None, layer=0, n_layers=1,
          stacked=None):
    T, D = x.shape
    tps = seq // tm
    n_nn = wnn.shape[1] if n_nn is None else n_nn
    n_nt = len(wnts)
    const = lambda i: (0, 0)
    tok_t = lambda i: (i // tps, 0, i % tps)
    in_specs = [pl.BlockSpec((tm, D), lambda i: (i, 0)),
                pl.BlockSpec((1, D), const),
                pl.BlockSpec((D, n_nn), const),
                pl.BlockSpec((GATE_ROWS, D), const),
                pl.BlockSpec((GATE_ROWS, 1), const)]
    in_specs += [pl.BlockSpec(w.shape, const) for w in wnts]
    args = [x, g.reshape(1, D), wnn, wg, gb, *wnts]
    aliases = {}
    if stacked is not None:
        for k, buf in enumerate(stacked):
            aliases[len(args)] = 1 + k
            in_specs.append(pl.BlockSpec(memory_space=pl.ANY))
            args.append(buf)
    out_shape = [jax.ShapeDtypeStruct((T, n_nn), F32)]
    out_specs = [pl.BlockSpec((tm, n_nn), lambda i: (i, 0))]
    for w in wnts:
        out_shape.append(jax.ShapeDtypeStruct((n_layers, batch, w.shape[0], seq), F32))
        out_specs.append(pl.BlockSpec((1, 1, w.shape[0], tm), lambda i: (layer, i // tps, 0, i % tps)))
    out_shape.append(jax.ShapeDtypeStruct((batch, GATE_ROWS, seq), F32))
    out_specs.append(pl.BlockSpec((1, GATE_ROWS, tm), tok_t))
    return pl.pallas_call(
        functools.partial(_proj_kernel, n_nt=n_nt, n_alias=len(aliases), n_lin=n_lin, col_chunk=512,
                          cum_chunk=cum_chunk),
        grid=(T // tm,), in_specs=in_specs, out_specs=out_specs, out_shape=out_shape,
        input_output_aliases=aliases,
        compiler_params=_cparams("parallel"), name="norm_proj",
    )(*args)


def _cumsum_kernel(x_ref, o_ref):
    S = x_ref.shape[2]
    upto = (lax.broadcasted_iota(jnp.int32, (LANES, LANES), 0)
            <= lax.broadcasted_iota(jnp.int32, (LANES, LANES), 1)).astype(F32)
    carry = jnp.zeros((GATE_ROWS, 1), F32)
    for c in range(0, S, LANES):
        inc = jnp.dot(x_ref[0, :, c:c + LANES], upto, precision=lax.Precision.HIGHEST,
                      preferred_element_type=F32) + carry
        for p in range(GATE_ROWS // 2):
            o_ref[0, p, :, c:c + LANES] = inc[2 * p:2 * p + 2, :]
        carry = inc[:, LANES - 1:LANES]


def _seq_cumsum(x):
    B, R, S = x.shape
    return pl.pallas_call(_cumsum_kernel, grid=(B,),
                          in_specs=[pl.BlockSpec((1, R, S), lambda b: (b, 0, 0))],
                          out_specs=pl.BlockSpec((1, R // 2, 2, S), lambda b: (b, 0, 0, 0)),
                          out_shape=jax.ShapeDtypeStruct((B, R // 2, 2, S), F32),
                          compiler_params=_cparams("parallel"), name="logf_cumsum")(x)


def _fox_kernel(qi_ref, kj_ref, qt_ref, k_ref, vt_ref, crow_ref, ccol_ref, o_ref, q_s, m_s, l_s, a_s, *,
                tile, key_chunk):
    qi = qi_ref[pl.program_id(2)]
    kj = kj_ref[pl.program_id(2)]
    hd = A_HEAD_DIM

    @pl.when(kj == 0)
    def _():
        qt = qt_ref[0, 0] * np.float32(hd ** -0.5)
        row = lax.broadcasted_iota(jnp.int32, qt.shape, 0)
        q_s[0] = jnp.where(row < hd, qt, 0.0).astype(BF16)
        q_s[1] = jnp.where(row >= hd, qt, 0.0).astype(BF16)
        m_s[...] = jnp.full(m_s.shape, NEG_INF, F32)
        l_s[...] = jnp.zeros(l_s.shape, F32)
        a_s[...] = jnp.zeros(a_s.shape, F32)

    def step(diagonal):
        cq = crow_ref[0, 0]
        join = lambda old, new, c0: new if c0 == 0 else jnp.concatenate([old[:, :c0], new], axis=1)
        for hh in range(2):
            rows = slice(hh * hd, (hh + 1) * hd)
            m_run, l_run, acc = m_s[hh], l_s[hh], a_s[rows, :]
            for c in range(0, tile, key_chunk):
                c0 = c if diagonal else 0
                kb = k_ref[c:c + key_chunk, :].astype(BF16)
                s = jnp.dot(kb, q_s[hh, :, c0:], preferred_element_type=F32)
                s = s + cq[hh:hh + 1, c0:] - ccol_ref[0, 0, c:c + key_chunk, hh:hh + 1]
                if diagonal:
                    causal = (lax.broadcasted_iota(jnp.int32, s.shape, 0)
                              <= lax.broadcasted_iota(jnp.int32, s.shape, 1))
                    s = jnp.where(causal, s, NEG_INF)
                m_old = m_run[:, c0:]
                m_new = jnp.maximum(m_old, jnp.max(s, 0, keepdims=True))
                alpha = jnp.exp(m_old - m_new)
                p = jnp.exp(s - m_new)
                l_new = alpha * l_run[:, c0:] + jnp.sum(p, 0, keepdims=True)
                vt = vt_ref[0, 0, rows, c:c + key_chunk].astype(BF16)
                a_new = alpha * acc[:, c0:] + jnp.dot(vt, p.astype(BF16), preferred_element_type=F32)
                m_run, l_run, acc = join(m_run, m_new, c0), join(l_run, l_new, c0), join(acc, a_new, c0)
            m_s[hh], l_s[hh] = m_run, l_run
            a_s[rows, :] = acc

    @pl.when(kj < qi)
    def _():
        step(False)

    @pl.when(kj == qi)
    def _():
        step(True)
        row = lax.broadcasted_iota(jnp.int32, (2 * hd, tile), 0)
        out_t = a_s[...] / jnp.where(row < hd, l_s[0], l_s[1])
        o_ref[...] = out_t.T


def _fox_prompt(y, qt, vt, c, *, k_col, layer, batch, seq, tile):
    T = y.shape[0]
    pairs = A_HEADS // 2
    nt = seq // tile
    kb0 = k_col // LANES
    crow = c
    ccol = jnp.swapaxes(crow, 2, 3)
    tri = [(i, j) for i in range(nt) for j in range(i + 1)]
    qi_tab = jnp.asarray([i for i, _ in tri], jnp.int32)
    kj_tab = jnp.asarray([j for _, j in tri], jnp.int32)
    grid_spec = pltpu.PrefetchScalarGridSpec(
        num_scalar_prefetch=2, grid=(batch, pairs, len(tri)),
        in_specs=[pl.BlockSpec((1, 1, LANES, tile), lambda b, p, t, qi, kj: (layer, b, p, qi[t])),
                  pl.BlockSpec((tile, LANES), lambda b, p, t, qi, kj: (b * nt + kj[t], kb0 + p)),
                  pl.BlockSpec((1, 1, LANES, tile), lambda b, p, t, qi, kj: (layer, b, p, kj[t])),
                  pl.BlockSpec((1, 1, 2, tile), lambda b, p, t, qi, kj: (b, p, 0, qi[t])),
                  pl.BlockSpec((1, 1, tile, 2), lambda b, p, t, qi, kj: (b, p, kj[t], 0))],
        out_specs=pl.BlockSpec((tile, LANES), lambda b, p, t, qi, kj: (b * nt + qi[t], p)),
        scratch_shapes=[pltpu.VMEM((2, LANES, tile), BF16), pltpu.VMEM((2, 1, tile), F32),
                        pltpu.VMEM((2, 1, tile), F32), pltpu.VMEM((LANES, tile), F32)])
    return pl.pallas_call(
        functools.partial(_fox_kernel, tile=tile, key_chunk=tile),
        grid_spec=grid_spec, out_shape=jax.ShapeDtypeStruct((T, A_WIDTH), F32),
        compiler_params=_cparams("parallel", "parallel", "arbitrary"),
        name="fox_prompt",
    )(qi_tab, kj_tab, qt, y, vt, crow, ccol)


def _fox_decode_kernel(pt_ref, q_ref, kn_ref, vn_ref, lfn_ref, kc_hbm, vc_hbm, lfc_hbm, o_ref,
                       kbuf, vbuf, lfbuf, sem, m_s, l_s, c_s, a_s, q_s, *, layer, group_pages):
    P = group_pages
    b = pl.program_id(0)
    n_seq = pl.num_programs(0)
    n_pages = pt_ref.shape[1]
    G = n_pages // P
    H, hd = A_HEADS, A_HEAD_DIM
    hsl = [slice(h * hd, (h + 1) * hd) for h in range(H)]

    def group_copies(slot, page_of):
        out = []
        for i in range(P):
            pid = page_of(i)
            out.append(pltpu.make_async_copy(kc_hbm.at[layer, pid], kbuf.at[slot, i], sem.at[0, slot]))
            out.append(pltpu.make_async_copy(vc_hbm.at[layer, pid], vbuf.at[slot, i], sem.at[1, slot]))
            out.append(pltpu.make_async_copy(lfc_hbm.at[layer, pid], lfbuf.at[slot, i], sem.at[2, slot]))
        return out

    def start_group(seq, g, slot):
        for n, cp in enumerate(group_copies(slot, lambda i: pt_ref[seq, n_pages - 1 - (g * P + i)])):
            cp.start(priority=1 if n % 3 == 1 else 0)

    def wait_group(slot):
        for cp in group_copies(slot, lambda i: 0):
            cp.wait()

    @pl.when(b == 0)
    def _():
        start_group(0, 0, 0)

    qcols = [q_ref[0, hsl[h], :] * np.float32(hd ** -0.5) for h in range(H)]
    for h in range(H):
        q_s[h] = jnp.broadcast_to(qcols[h], (hd, LANES))
    s_new = jnp.concatenate(
        [jnp.sum(qcols[h] * kn_ref[0, hsl[h], :], axis=0, keepdims=True) for h in range(H)], axis=0)
    m_s[...] = s_new
    l_s[...] = jnp.ones(l_s.shape, F32)
    c_s[...] = lfn_ref[0]
    lane = lax.broadcasted_iota(jnp.int32, (hd, LANES), 1)
    for h in range(H):
        a_s[h] = jnp.where(lane == 0, vn_ref[0, hsl[h], :], 0.0)
    later = (lax.broadcasted_iota(jnp.int32, (LANES, LANES), 0)
             > lax.broadcasted_iota(jnp.int32, (LANES, LANES), 1)).astype(F32)

    def group(g, carry):
        slot = g % 2
        more = g + 1 < G
        nxt_seq = jnp.where(more, b, b + 1)

        @pl.when(nxt_seq < n_seq)
        def _():
            start_group(nxt_seq, jnp.where(more, g + 1, 0), 1 - slot)

        wait_group(slot)
        c = c_s[...]
        lf_all = jnp.concatenate([lfbuf[slot, i] for i in range(P)], axis=0)
        excl_all = jnp.dot(lf_all, later, precision=lax.Precision.HIGHEST, preferred_element_type=F32)
        tot_all = jnp.sum(lf_all, -1, keepdims=True)
        s_pages = []
        for i in range(P):
            rows = [jnp.sum(kbuf[slot, i, h] * q_s[h], axis=0, keepdims=True) for h in range(H)]
            s_pages.append(jnp.concatenate(rows, axis=0) + (c + excl_all[i * H:(i + 1) * H, :]))
            c = c + tot_all[i * H:(i + 1) * H, :]
        c_s[...] = c
        s_all = jnp.concatenate(s_pages, axis=1)
        m_old = m_s[...]
        m_new = jnp.maximum(m_old, jnp.max(s_all, -1, keepdims=True))
        alpha = jnp.exp(m_old - m_new)
        p_all = jnp.exp(s_all - m_new)
        l_s[...] = alpha * l_s[...] + jnp.sum(p_all, -1, keepdims=True)
        m_s[...] = m_new
        for h in range(H):
            acc = a_s[h] * alpha[h:h + 1, :]
            for i in range(P):
                acc = acc + p_all[h:h + 1, i * LANES:(i + 1) * LANES] * vbuf[slot, i, h]
            a_s[h] = acc
        return carry

    lax.fori_loop(0, G, group, 0)
    l = l_s[...]
    for h in range(H):
        o_ref[0, hsl[h], :] = jnp.sum(a_s[h], axis=1, keepdims=True) / l[h:h + 1, :]


def _fox_decode(q, k_new, v_new, lf_new, cache_kt, cache_vt, cache_lft, page_table, layer, *, pages_per_step):
    DB, n_pages = page_table.shape
    P = pages_per_step
    assert n_pages % (2 * P) == 0, "an even number of page groups keeps the buffer parity fixed per sequence"
    col = lambda a: a.reshape(DB, a.shape[1], 1)
    vec = lambda n: pl.BlockSpec((1, n, 1), lambda b, pt: (b, 0, 0))
    hbm = pl.BlockSpec(memory_space=pl.ANY)
    page = (A_HEADS, A_HEAD_DIM, LANES)
    grid_spec = pltpu.PrefetchScalarGridSpec(
        num_scalar_prefetch=1, grid=(DB,),
        in_specs=[vec(A_WIDTH), vec(A_WIDTH), vec(A_WIDTH), vec(A_HEADS), hbm, hbm, hbm],
        out_specs=vec(A_WIDTH),
        scratch_shapes=[pltpu.VMEM((2, P) + page, F32), pltpu.VMEM((2, P) + page, F32),
                        pltpu.VMEM((2, P, A_HEADS, LANES), F32),
                        pltpu.SemaphoreType.DMA((3, 2)),
                        pltpu.VMEM((A_HEADS, 1), F32), pltpu.VMEM((A_HEADS, 1), F32),
                        pltpu.VMEM((A_HEADS, 1), F32),
                        pltpu.VMEM(page, F32), pltpu.VMEM(page, F32)])
    out = pl.pallas_call(
        functools.partial(_fox_decode_kernel, layer=layer, group_pages=P),
        grid_spec=grid_spec, out_shape=jax.ShapeDtypeStruct((DB, A_WIDTH, 1), F32),
        compiler_params=_cparams("arbitrary"), name="fox_decode",
    )(page_table, col(q), col(k_new), col(v_new), col(lf_new), cache_kt, cache_vt, cache_lft)
    return out.reshape(DB, A_WIDTH)


def _even_mix_kernel(att_ref, za_ref, u_ref, vb_ref, zb_ref, h_ref, lng_ref, lnb_ref, ws_ref, bs_ref,
                     wout_ref, *outs, decode):
    o_ref = outs[0]
    ya = att_ref[...] * _silu(za_ref[...])
    u = _gelu(u_ref[...])
    vf = _gelu(vb_ref[...])
    mu = jnp.mean(vf, -1, keepdims=True)
    var = jnp.mean((vf - mu) ** 2, -1, keepdims=True)
    vn = (vf - mu) * lax.rsqrt(var + EPS) * lng_ref[...] + lnb_ref[...]
    tm, bw = vn.shape
    if decode:
        outs[1][...] = vn
        mix = vn * ws_ref[...] + bs_ref[...]
    else:
        lane = lax.broadcasted_iota(jnp.int32, (B_CHUNK, LANES), 1)
        tri = (lax.broadcasted_iota(jnp.int32, (B_CHUNK, B_CHUNK), 0)
               >= lax.broadcasted_iota(jnp.int32, (B_CHUNK, B_CHUNK), 1))
        wtril = [jnp.where(tri, ws_ref[g], 0.0).astype(BF16) for g in range(B_GROUPS)]
        gpl = LANES // (bw // B_GROUPS)
        rows = []
        for c in range(0, tm, B_CHUNK):
            blocks = []
            for lb in range(bw // LANES):
                vblk = vn[c:c + B_CHUNK, lb * LANES:(lb + 1) * LANES].astype(BF16)
                y0 = jnp.dot(wtril[lb * gpl], vblk, preferred_element_type=F32)
                y1 = jnp.dot(wtril[lb * gpl + 1], vblk, preferred_element_type=F32)
                blocks.append(jnp.where(lane < LANES // gpl, y0, y1))
            rows.append(jnp.concatenate(blocks, axis=1) + bs_ref[...])
        mix = jnp.concatenate(rows, axis=0)
    yb = u * mix * _silu(zb_ref[...])
    aw = ya.shape[1]
    hn = h_ref[...] + _mm(ya, wout_ref[:aw, :]) + _mm(yb, wout_ref[aw:, :])
    o_ref[...] = hn


def _even_mix(att, y, h, ln_g, ln_b, ws, bs, wout, *, col0, tm, decode):
    T, D = h.shape
    aw = att.shape[1]
    const2 = lambda i: (0, 0)
    yblk = lambda k: pl.BlockSpec((tm, aw), lambda i: (i, col0 + k))
    ws_spec = (pl.BlockSpec(ws.shape, const2) if decode else pl.BlockSpec(ws.shape, lambda i: (0, 0, 0)))
    in_specs = [pl.BlockSpec((tm, aw), lambda i: (i, 0)), yblk(0), yblk(1), yblk(2), yblk(3),
                pl.BlockSpec((tm, D), lambda i: (i, 0)),
                pl.BlockSpec((1, aw), const2), pl.BlockSpec((1, aw), const2),
                ws_spec, pl.BlockSpec(bs.shape, const2), pl.BlockSpec(wout.shape, const2)]
    out_shape = [jax.ShapeDtypeStruct((T, D), F32)]
    out_specs = [pl.BlockSpec((tm, D), lambda i: (i, 0))]
    if decode:
        out_shape.append(jax.ShapeDtypeStruct((T, aw), F32))
        out_specs.append(pl.BlockSpec((tm, aw), lambda i: (i, 0)))
    return pl.pallas_call(
        functools.partial(_even_mix_kernel, decode=decode),
        grid=(T // tm,), in_specs=in_specs, out_specs=out_specs, out_shape=out_shape,
        compiler_params=_cparams("parallel"), name="even_mix",
    )(att, y, y, y, y, h, ln_g.reshape(1, aw), ln_b.reshape(1, aw), ws, bs, wout)


def _qkv_decode_kernel(xc_ref, prev_ref, cw_ref, cb_ref, wq_ref, wk_ref, wv_ref,
                       xconv_ref, q_ref, k_ref, v_ref, nb_ref):
    xc = xc_ref[...]
    hd = xc.shape[1] // C_HEADS
    acc = cb_ref[...] + cw_ref[CONV_W - 1:CONV_W, :] * xc
    for j in range(CONV_W - 1):
        acc = acc + cw_ref[j:j + 1, :] * prev_ref[j]
    for j in range(CONV_W - 2):
        nb_ref[j] = prev_ref[j + 1]
    nb_ref[CONV_W - 2] = xc
    xconv = _silu(acc)
    xconv_ref[...] = xconv
    for h in range(C_HEADS):
        sl = slice(h * hd, (h + 1) * hd)
        q_ref[:, sl] = _mm(xconv[:, sl], wq_ref[h]) * np.float32(hd ** -0.5)
        k_ref[:, sl] = _mm(xconv[:, sl], wk_ref[h])
        v_ref[:, sl] = _mm(xc[:, sl], wv_ref[h])


def _qkv_decode(y, prev, cw, cb, wq, wk, wv):
    T = y.shape[0]
    W = cw.shape[1]
    const2 = lambda i: (0, 0)
    const3 = lambda i: (0, 0, 0)
    row_blk = pl.BlockSpec((T, W), lambda i: (0, 0))
    row_out = jax.ShapeDtypeStruct((T, W), F32)
    return pl.pallas_call(
        _qkv_decode_kernel, grid=(1,),
        in_specs=[row_blk, pl.BlockSpec(prev.shape, const3), pl.BlockSpec(cw.shape, const2),
                  pl.BlockSpec((1, W), const2), pl.BlockSpec(wq.shape, const3), pl.BlockSpec(wk.shape, const3),
                  pl.BlockSpec(wv.shape, const3)],
        out_specs=[row_blk, row_blk, row_blk, row_blk, pl.BlockSpec(prev.shape, const3)],
        out_shape=[row_out, row_out, row_out, row_out, jax.ShapeDtypeStruct(prev.shape, F32)],
        compiler_params=_cparams("arbitrary"), name="conv_qkv_decode",
    )(y, prev, cw, cb.reshape(1, W), wq, wk, wv)


def _mlstm_decode_kernel(q_ref, k_ref, v_ref, g_ref, o_ref, z_ref, xconv_ref, ng_ref, skip_ref,
                         c0_ref, n0_ref, m0_ref, *rest, nb, n_alias):
    y_ref, c_out, n_out, m_out = rest[n_alias:]
    W = q_ref.shape[2]
    hd = W // C_HEADS
    first = lax.broadcasted_iota(jnp.int32, (8, hd), 0) == 0
    pad8 = lambda r: jnp.where(first, jnp.broadcast_to(r, (8, hd)), 0.0)
    for b in range(nb):
        ys = []
        for h in range(C_HEADS):
            sl = slice(h * hd, (h + 1) * hd)
            q, k, v = q_ref[b, :, sl], k_ref[b, :, sl], v_ref[b, :, sl]
            log_i, log_f = g_ref[b, h:h + 1, :], g_ref[b, C_HEADS + h:C_HEADS + h + 1, :]
            c0, n0, m0 = c0_ref[0, b, h], n0_ref[0, b, h:h + 1, :], m0_ref[0, b, h:h + 1, :]
            m_new = jnp.maximum(log_f + m0, log_i)
            a = jnp.exp(log_f + m0 - m_new)
            w = jnp.exp(log_i - m_new)
            sm = w * jnp.sum(q * k, -1, keepdims=True)
            num = sm * v + a * _mm(pad8(q), c0)[0:1, :]
            den = sm + a * jnp.sum(q * n0, -1, keepdims=True)
            hc = num / jnp.maximum(jnp.abs(den), jnp.exp(-m_new))
            ktv = lax.dot_general(pad8(k).astype(BF16), pad8(v).astype(BF16), (((0,), (0,)), ((), ())),
                                  preferred_element_type=F32)
            c_out[0, b, h] = a * c0 + w * ktv
            n_out[b, h:h + 1, :] = a * n0 + w * k
            m_out[b, h:h + 1, :] = m_new
            ys.append(_mlstm_gate(hc, o_ref[b, :, sl], z_ref[b, :, sl], xconv_ref[b, :, sl],
                                  ng_ref[:, sl], skip_ref[:, sl]))
        y_ref[b] = jnp.concatenate(ys, axis=1)


def _mlstm_decode(q, k, v, gates, y_in, xconv, ng, skip, state, c_stack, *, layer, n_layers, nb):
    batch, W = q.shape[0], q.shape[2]
    hd = W // C_HEADS
    const2 = lambda b: (0, 0)
    rowblk = lambda k: pl.BlockSpec((nb, 1, W), lambda b: (b, 0, k))
    c0, n0, m0 = state
    in_specs = [rowblk(0), rowblk(0), rowblk(0),
                pl.BlockSpec((nb, GATE_ROWS, 1), lambda b: (b, 0, 0)),
                rowblk(2), rowblk(1), rowblk(0),
                pl.BlockSpec((1, W), const2), pl.BlockSpec((1, W), const2),
                pl.BlockSpec((1, nb, C_HEADS, hd, hd), lambda b: (layer, b, 0, 0, 0)),
                pl.BlockSpec((1, nb, C_HEADS, hd), lambda b: (layer, b, 0, 0)),
                pl.BlockSpec((1, nb, C_HEADS, 1), lambda b: (layer, b, 0, 0))]
    args = [q, k, v, gates, y_in, y_in, xconv, ng.reshape(1, W), skip.reshape(1, W),
            c0, n0, m0.reshape(m0.shape + (1,))]
    aliases = {}
    if c_stack is not None:
        aliases[len(args)] = 1
        in_specs.append(pl.BlockSpec(memory_space=pl.ANY))
        args.append(c_stack)
    out_shape = [jax.ShapeDtypeStruct((batch, 1, W), F32),
                 jax.ShapeDtypeStruct((n_layers, batch, C_HEADS, hd, hd), F32),
                 jax.ShapeDtypeStruct((batch, C_HEADS, hd), F32),
                 jax.ShapeDtypeStruct((batch, C_HEADS, 1), F32)]
    out_specs = [pl.BlockSpec((nb, 1, W), lambda b: (b, 0, 0)),
                 pl.BlockSpec((1, nb, C_HEADS, hd, hd), lambda b: (layer, b, 0, 0, 0)),
                 pl.BlockSpec((nb, C_HEADS, hd), lambda b: (b, 0, 0)),
                 pl.BlockSpec((nb, C_HEADS, 1), lambda b: (b, 0, 0))]
    y, c_new, n_new, m_new = pl.pallas_call(
        functools.partial(_mlstm_decode_kernel, nb=nb, n_alias=len(aliases)),
        grid=(batch // nb,), in_specs=in_specs, out_specs=out_specs, out_shape=out_shape,
        input_output_aliases=aliases,
        compiler_params=_cparams("parallel"), name="mlstm_decode",
    )(*args)
    return y, c_new, n_new, m_new.reshape(batch, C_HEADS)


def _mlstm_head(qh, kth, vh, i_row, b_row, m_prev, caug, tri, one_col):
    L, hd = qh.shape
    g_row = i_row - b_row
    dm = jnp.where(tri, g_row, NEG_INF)
    mcol = jnp.maximum(m_prev, jnp.max(dm, -1, keepdims=True))
    wmat = jnp.exp(dm - mcol)
    a = jnp.exp(m_prev - mcol)
    sm = wmat * _mm(qh, kth)
    qc = _mm(qh, caug)
    num = _mm(sm, vh) + a * qc[:, :hd]
    den = jnp.sum(sm, -1, keepdims=True) + a * qc[:, hd:hd + 1]
    den = jnp.maximum(jnp.abs(den), jnp.exp(-(_row_to_col(b_row) + mcol)))
    m_last = mcol[L - 1:L, :]
    wl = jnp.exp(g_row - m_last)
    a_l = jnp.exp(m_prev - m_last)
    caug_new = a_l * caug + _mm(kth * wl, jnp.concatenate([vh, one_col], axis=1))
    return num / den, caug_new, b_row[:, L - 1:L] + m_last


def _mlstm_gate(hc, o, z, xconv, ng, skip):
    mu = jnp.mean(hc, -1, keepdims=True)
    var = jnp.mean((hc - mu) ** 2, -1, keepdims=True)
    hn = (hc - mu) * lax.rsqrt(var + EPS) * ng
    return (_sigmoid(o) * hn + skip * xconv) * _silu(z)


def _mlstm_prompt_kernel(xc_ref, halo_ref, z_ref, o_ref, g_ref, h_ref, cw_ref, cb_ref, wq_ref, wkt_ref, wv_ref,
                         ng_ref, skip_ref, wout_ref, *rest, nb, n_alias, final):
    if final:
        fg_ref = rest[0]
        rest = rest[1:]
    hout_ref, c_out, n_out, m_out, caug_s, m_s = rest[n_alias:]
    L, W = xc_ref.shape[1], xc_ref.shape[2]
    hd = W // C_HEADS
    ci = pl.program_id(1)

    @pl.when(ci == 0)
    def _():
        caug_s[...] = jnp.zeros(caug_s.shape, F32)
        m_s[...] = jnp.zeros(m_s.shape, F32)

    xcs, xconvs = [], []
    for b in range(nb):
        xc = xc_ref[b]
        halo = jnp.where(ci == 0, 0.0, halo_ref[b])
        xx = jnp.concatenate([halo, xc], axis=0)
        acc = cb_ref[...] + cw_ref[CONV_W - 1:CONV_W, :] * xc
        for k in range(1, CONV_W):
            acc = acc + cw_ref[CONV_W - 1 - k:CONV_W - k, :] * pltpu.roll(xx, k, 0)[8:]
        xcs.append(xc)
        xconvs.append(_silu(acc))
    xc_all = jnp.concatenate(xcs, axis=0)
    xconv_all = jnp.concatenate(xconvs, axis=0)
    tri = (lax.broadcasted_iota(jnp.int32, (L, L), 0) >= lax.broadcasted_iota(jnp.int32, (L, L), 1))
    one_col = (lax.broadcasted_iota(jnp.int32, (L, LANES), 1) == 0).astype(F32)
    ys = [[None] * C_HEADS for _ in range(nb)]
    for h in range(C_HEADS):
        sl = slice(h * hd, (h + 1) * hd)
        xh = xconv_all[:, sl]
        q_h = _mm(xh, wq_ref[h]) * np.float32(hd ** -0.5)
        kt_h = _mm_nt(wkt_ref[h], xh)
        v_h = _mm(xc_all[:, sl], wv_ref[h])
        for b in range(nb):
            r = slice(b * L, (b + 1) * L)
            gates = g_ref[b]
            hc, caug_new, m_new = _mlstm_head(
                q_h[r], kt_h[:, r], v_h[r], gates[h:h + 1, :], gates[C_HEADS + h:C_HEADS + h + 1, :],
                m_s[b, h][0:1, 0:1], caug_s[b, h], tri, one_col)
            caug_s[b, h] = caug_new
            m_s[b, h] = jnp.broadcast_to(m_new, (8, LANES))
            ys[b][h] = _mlstm_gate(hc, o_ref[b][:, sl], z_ref[b][:, sl], xconv_all[r, sl],
                                   ng_ref[:, sl], skip_ref[:, sl])
    y_all = jnp.concatenate([jnp.concatenate(ys[b], axis=1) for b in range(nb)], axis=0)
    hn = jnp.concatenate([h_ref[b] for b in range(nb)], axis=0) + _mm(y_all, wout_ref[...])
    if final:
        hn = hn * lax.rsqrt(jnp.mean(hn * hn, -1, keepdims=True) + EPS) * fg_ref[...]
    for b in range(nb):
        hout_ref[b] = hn[b * L:(b + 1) * L]

    @pl.when(ci == pl.num_programs(1) - 1)
    def _():
        for b in range(nb):
            for h in range(C_HEADS):
                c_out[0, b, h] = caug_s[b, h, :, :hd]
                n_out[b, h] = caug_s[b, h, :, hd:hd + 1]
                m_out[b, h:h + 1, :] = m_s[b, h][0:1, 0:1]


def _mlstm_prompt(y_in, gates, h, cw, cb, wq, wkt, wv, ng, skip, wout, final_g, c_stack, *, layer, n_layers, nb):
    batch, S = y_in.shape[0], y_in.shape[1]
    W, D = wout.shape
    hd = W // C_HEADS
    L = C_CHUNK
    const2 = lambda b, c: (0, 0)
    const3 = lambda b, c: (0, 0, 0)
    yblk = lambda k: pl.BlockSpec((nb, L, W), lambda b, c: (b, c, k))
    in_specs = [yblk(0),
                pl.BlockSpec((nb, 8, W), lambda b, c: (b, jnp.maximum(c * (L // 8) - 1, 0), 0)),
                yblk(1), yblk(2),
                pl.BlockSpec((nb, GATE_ROWS, L), lambda b, c: (b, 0, c)),
                pl.BlockSpec((nb, L, D), lambda b, c: (b, c, 0)),
                pl.BlockSpec(cw.shape, const2), pl.BlockSpec((1, W), const2),
                pl.BlockSpec(wq.shape, const3), pl.BlockSpec(wkt.shape, const3), pl.BlockSpec(wv.shape, const3),
                pl.BlockSpec((1, W), const2), pl.BlockSpec((1, W), const2), pl.BlockSpec(wout.shape, const2)]
    args = [y_in, y_in, y_in, y_in, gates, h, cw, cb.reshape(1, W), wq, wkt, wv,
            ng.reshape(1, W), skip.reshape(1, W), wout]
    if final_g is not None:
        in_specs.append(pl.BlockSpec((1, D), const2))
        args.append(final_g.reshape(1, D))
    aliases = {}
    if c_stack is not None:
        aliases[len(args)] = 1
        in_specs.append(pl.BlockSpec(memory_space=pl.ANY))
        args.append(c_stack)
    out_shape = [jax.ShapeDtypeStruct((batch, S, D), F32),
                 jax.ShapeDtypeStruct((n_layers, batch, C_HEADS, hd, hd), F32),
                 jax.ShapeDtypeStruct((batch, C_HEADS, hd, 1), F32),
                 jax.ShapeDtypeStruct((batch, C_HEADS, 1), F32)]
    out_specs = [pl.BlockSpec((nb, L, D), lambda b, c: (b, c, 0)),
                 pl.BlockSpec((1, nb, C_HEADS, hd, hd), lambda b, c: (layer, b, 0, 0, 0)),
                 pl.BlockSpec((nb, C_HEADS, hd, 1), lambda b, c: (b, 0, 0, 0)),
                 pl.BlockSpec((nb, C_HEADS, 1), lambda b, c: (b, 0, 0))]
    hout, c_new, n_new, m_new = pl.pallas_call(
        functools.partial(_mlstm_prompt_kernel, nb=nb, n_alias=len(aliases), final=final_g is not None),
        grid=(batch // nb, S // L), in_specs=in_specs, out_specs=out_specs, out_shape=out_shape,
        input_output_aliases=aliases,
        scratch_shapes=[pltpu.VMEM((nb, C_HEADS, hd, hd + LANES), F32), pltpu.VMEM((nb, C_HEADS, 8, LANES), F32)],
        compiler_params=_cparams("parallel", "arbitrary"), name="mlstm_layer",
    )(*args)
    return hout, c_new, n_new.reshape(batch, C_HEADS, hd), m_new.reshape(batch, C_HEADS)


def _out_proj_kernel(y_ref, w_ref, h_ref, *rest, final):
    hn = h_ref[...] + _mm(y_ref[...], w_ref[...])
    if final:
        fg_ref, o_ref = rest
        o_ref[...] = hn * lax.rsqrt(jnp.mean(hn * hn, -1, keepdims=True) + EPS) * fg_ref[...]
    else:
        rest[0][...] = hn


def _out_proj(y, w, h, final_g, *, tm):
    T, D = h.shape
    K = y.shape[1]
    const2 = lambda i: (0, 0)
    in_specs = [pl.BlockSpec((tm, K), lambda i: (i, 0)), pl.BlockSpec(w.shape, const2),
                pl.BlockSpec((tm, D), lambda i: (i, 0))]
    args = [y, w, h]
    if final_g is not None:
        in_specs.append(pl.BlockSpec((1, D), const2))
        args.append(final_g.reshape(1, D))
    return pl.pallas_call(
        functools.partial(_out_proj_kernel, final=final_g is not None),
        grid=(T // tm,), in_specs=in_specs, out_specs=pl.BlockSpec((tm, D), lambda i: (i, 0)),
        out_shape=jax.ShapeDtypeStruct((T, D), F32),
        compiler_params=_cparams("parallel"), name="out_proj",
    )(*args)


def kernel(x_prompt, x_sample, cache_k, cache_v, cache_logf, state_c, state_n, state_m, state_conv, page_table,
           norm_g, final_g, even_w_in, even_b_f, gmlp_ln_g, gmlp_ln_b, gmlp_w_s, gmlp_b_s, even_w_out,
           odd_w_in, odd_b_i, odd_b_f, conv_w, conv_b, mlstm_w_q, mlstm_w_k, mlstm_w_v, mlstm_norm_g,
           mlstm_skip, odd_w_out):
    B, S, D = x_prompt.shape
    DB = x_sample.shape[0]
    depth = norm_g.shape[0]
    AW = A_WIDTH
    BW = gmlp_ln_g.shape[1]
    CW = conv_w.shape[2]
    gdim = BW // B_GROUPS
    tm = 512
    assert AW == BW, "even-layer column blocks are addressed in units of one common width"

    hp = x_prompt.reshape(B * S, D)
    hs = x_sample.reshape(DB, D)
    cache_kt = jnp.transpose(cache_k, (0, 1, 3, 4, 2))
    cache_vt = jnp.transpose(cache_v, (0, 1, 3, 4, 2))
    cache_lft = jnp.transpose(cache_logf, (0, 1, 3, 2))

    lp, ks, vs, ls, chv = [], [], [], [], []
    np_, mp, bp, ns, ms, bs = [], [], [], [], [], []
    n_even, n_odd = (depth + 1) // 2, depth // 2
    assert depth % 2 == 0, "the final rmsnorm is fused into the last (mLSTM) layer's output projection"
    qkv_stack = cp_stack = cs_stack = None
    for l in range(depth):
        j = l // 2
        last = l == depth - 1
        if l % 2 == 0:
            w = even_w_in[j]
            o = 0
            parts = {}
            for name, n in (("q", AW), ("k", AW), ("v", AW), ("fg", A_HEADS), ("za", AW), ("u", BW), ("vb", BW), ("zb", BW)):
                parts[name] = w[:, o:o + n]
                o += n
            wnn_s = jnp.concatenate([parts[n] for n in ("za", "u", "vb", "zb", "k", "q", "v")], 1).astype(BF16)
            wqt = parts["q"].T.astype(BF16)
            wkt = parts["k"].T.astype(BF16)
            wvt = parts["v"].T.astype(BF16)
            wg = parts["fg"].T.astype(BF16)
            gb = even_b_f[j].reshape(GATE_ROWS, 1)
            wout = even_w_out[j].astype(BF16)
            bs_full = jnp.repeat(gmlp_b_s[j].T, gdim, axis=1)
            ws_dec = jnp.repeat(gmlp_w_s[j][:, 0, 0], gdim).reshape(1, BW)
            bs_dec = jnp.repeat(gmlp_b_s[j][:, 0], gdim).reshape(1, BW)

            y, qt, kt_all, vt_all, lf = _proj(hp, norm_g[l], wnn_s, [wqt, wkt, wvt], wg, gb, n_lin=0, n_nn=5 * AW,
                                              batch=B, seq=S, tm=tm, layer=j, n_layers=n_even, stacked=qkv_stack)
            qkv_stack = [qt, kt_all, vt_all]
            c = _seq_cumsum(lf)
            att = _fox_prompt(y, qt, vt_all, c, k_col=4 * AW, layer=j, batch=B, seq=S, tile=512)
            hp = _even_mix(att, y, hp, gmlp_ln_g[j], gmlp_ln_b[j], gmlp_w_s[j], bs_full, wout,
                           col0=0, tm=tm, decode=False)[0]
            lp.append(jnp.transpose(lf, (0, 2, 1)))

            y, lf = _proj(hs, norm_g[l], wnn_s, [], wg, gb, n_lin=0, batch=1, seq=DB, tm=DB)
            lf_s = lf[0].T
            k_s, q_s, v_s = y[:, 4 * AW:5 * AW], y[:, 5 * AW:6 * AW], y[:, 6 * AW:7 * AW]
            att = _fox_decode(q_s, k_s, v_s, lf_s, cache_kt, cache_vt, cache_lft, page_table, j,
                              pages_per_step=8)
            hs, vn_s = _even_mix(att, y, hs, gmlp_ln_g[j], gmlp_ln_b[j], ws_dec, bs_dec, wout,
                                 col0=0, tm=DB, decode=True)
            ks.append(k_s.reshape(DB, 1, A_HEADS, A_HEAD_DIM))
            vs.append(v_s.reshape(DB, 1, A_HEADS, A_HEAD_DIM))
            ls.append(lf_s.reshape(DB, 1, A_HEADS))
            chv.append(vn_s.reshape(DB, 1, BW))
        else:
            w = odd_w_in[j]
            wnn = w[:, :3 * CW].astype(BF16)
            wg = w[:, 3 * CW:].T.astype(BF16)
            gb = jnp.concatenate([odd_b_i[j], odd_b_f[j]]).reshape(GATE_ROWS, 1)
            wq = mlstm_w_q[j].astype(BF16)
            wk = mlstm_w_k[j].astype(BF16)
            wkT = jnp.swapaxes(mlstm_w_k[j], 1, 2).astype(BF16)
            wv = mlstm_w_v[j].astype(BF16)
            wout = odd_w_out[j].astype(BF16)
            fin = final_g if last else None

            y, g = _proj(hp, norm_g[l], wnn, [], wg, gb, n_lin=C_HEADS, batch=B, seq=S, tm=tm, cum_chunk=C_CHUNK)
            hout, cp_stack, n1, m1 = _mlstm_prompt(
                y.reshape(B, S, 3 * CW), g, hp.reshape(B, S, D), conv_w[j], conv_b[j], wq, wkT, wv,
                mlstm_norm_g[j], mlstm_skip[j], wout, fin, cp_stack, layer=j, n_layers=n_odd, nb=2)
            hp = hout.reshape(B * S, D)
            np_.append(n1); mp.append(m1)
            bp.append(y.reshape(B, S, 3 * CW)[:, S - (CONV_W - 1):, :CW])

            y, g = _proj(hs, norm_g[l], wnn, [], wg, gb, n_lin=C_HEADS, batch=1, seq=DB, tm=DB)
            buf = jnp.transpose(state_conv[j], (1, 0, 2))
            xconv, q, k, v, nbuf = _qkv_decode(y, buf, conv_w[j], conv_b[j], wq, wk, wv)
            g_s = jnp.transpose(g, (2, 1, 0))
            r3 = lambda a: a.reshape(DB, 1, a.shape[1])
            ym, cs_stack, n2, m2 = _mlstm_decode(r3(q), r3(k), r3(v), g_s, r3(y), r3(xconv), mlstm_norm_g[j],
                                                 mlstm_skip[j], (state_c, state_n, state_m), cs_stack,
                                                 layer=j, n_layers=n_odd, nb=4)
            hs = _out_proj(ym.reshape(DB, CW), wout, hs, fin, tm=DB)
            ns.append(n2); ms.append(m2)
            bs.append(jnp.transpose(nbuf, (1, 0, 2)))
    y_prompt = hp.reshape(B, S, D)
    y_sample = hs.reshape(DB, 1, D)
    to_bshd = lambda t: jnp.transpose(t.reshape(n_even, B, A_HEADS, A_HEAD_DIM, S), (0, 1, 4, 2, 3))
    return (y_prompt, y_sample,
            to_bshd(qkv_stack[1]), to_bshd(qkv_stack[2]), jnp.stack(lp),
            jnp.stack(ks), jnp.stack(vs), jnp.stack(ls), jnp.stack(chv),
            cp_stack, jnp.stack(np_), jnp.stack(mp), jnp.stack(bp),
            cs_stack, jnp.stack(ns), jnp.stack(ms), jnp.stack(bs))
```

```python
import functools

import jax
import jax.numpy as jnp
import numpy as np
from jax import lax
from jax.experimental import pallas as pl
from jax.experimental.pallas import tpu as pltpu

F32 = jnp.float32
BF16 = jnp.bfloat16
EPS = 1e-6
NEG_INF = float("-inf")

A_HEADS = 8
A_HEAD_DIM = 64
A_WIDTH = A_HEADS * A_HEAD_DIM
B_GROUPS = 8
B_CHUNK = 128
C_HEADS = 4
C_CHUNK = 128
CONV_W = 4
LANES = 128
GATE_ROWS = 8
VMEM_LIMIT = 56 * 1024 * 1024


def _cparams(*sem):
    return pltpu.CompilerParams(dimension_semantics=sem, vmem_limit_bytes=VMEM_LIMIT)


def _mm(a, b):
    return jnp.dot(a.astype(BF16), b.astype(BF16), preferred_element_type=F32)


def _mm_nt(a, b):
    return lax.dot_general(a.astype(BF16), b.astype(BF16), (((1,), (1,)), ((), ())),
                           preferred_element_type=F32)


def _log_sigmoid(x):
    return jnp.minimum(x, 0.0) - jnp.log1p(jnp.exp(-jnp.abs(x)))


def _sigmoid(x):
    return 1.0 / (1.0 + jnp.exp(-x))


def _silu(x):
    return x * _sigmoid(x)


def _gelu(x):
    return 0.5 * x * (1.0 + lax.erf(x * np.float32(np.sqrt(0.5))))


def _row_to_col(r):
    n = r.shape[1]
    eye = lax.broadcasted_iota(jnp.int32, (n, n), 0) == lax.broadcasted_iota(jnp.int32, (n, n), 1)
    return jnp.sum(jnp.where(eye, r, 0.0), axis=1, keepdims=True)


def _proj_kernel(x_ref, g_ref, wnn_ref, wg_ref, gb_ref, *rest, n_nt, n_alias, n_lin, col_chunk, cum_chunk):
    wnt_refs = rest[:n_nt]
    rest = rest[n_nt + n_alias:]
    y_ref = rest[0]
    yt_refs = rest[1:1 + n_nt]
    gt_ref = rest[1 + n_nt]
    x = x_ref[...]
    xn = x * lax.rsqrt(jnp.mean(x * x, -1, keepdims=True) + EPS) * g_ref[...]
    xb = xn.astype(BF16)
    n_nn = y_ref.shape[1]
    for c in range(0, n_nn, col_chunk):
        y_ref[:, c:c + col_chunk] = jnp.dot(xb, wnn_ref[:, c:c + col_chunk], preferred_element_type=F32)
    for w_ref, o_ref in zip(wnt_refs, yt_refs):
        o_ref[0, 0] = _mm_nt(w_ref[...], xb)
    gt = _mm_nt(wg_ref[...], xb) + gb_ref[...]
    row = lax.broadcasted_iota(jnp.int32, gt.shape, 0)
    gt = jnp.where(row >= n_lin, _log_sigmoid(gt), gt)
    if cum_chunk is None:
        gt_ref[0] = gt
    else:
        upto = (lax.broadcasted_iota(jnp.int32, (cum_chunk, cum_chunk), 0)
                <= lax.broadcasted_iota(jnp.int32, (cum_chunk, cum_chunk), 1)).astype(F32)
        rowc = row[:, :cum_chunk]
        for c in range(0, gt.shape[1], cum_chunk):
            blk = gt[:, c:c + cum_chunk]
            cs = jnp.dot(blk, upto, precision=lax.Precision.HIGHEST, preferred_element_type=F32)
            gt_ref[0, :, c:c + cum_chunk] = jnp.where(rowc >= n_lin, cs, blk)


def _proj(x, g, wnn, wnts, wg, gb, *, n_lin, batch, seq, tm, n_nn=None, cum_chunk=None, layer=0, n_layers=1,
          stacked=None):
    T, D = x.shape
    tps = seq // tm
    n_nn = wnn.shape[1] if n_nn is None else n_nn
    n_nt = len(wnts)
    const = lambda i: (0, 0)
    tok_t = lambda i: (i // tps, 0, i % tps)
    in_specs = [pl.BlockSpec((tm, D), lambda i: (i, 0)),
                pl.BlockSpec((1, D), const),
                pl.BlockSpec((D, n_nn), const),
                pl.BlockSpec((GATE_ROWS, D), const),
                pl.BlockSpec((GATE_ROWS, 1), const)]
    in_specs += [pl.BlockSpec(w.shape, const) for w in wnts]
    args = [x, g.reshape(1, D), wnn, wg, gb, *wnts]
    aliases = {}
    if stacked is not None:
        for k, buf in enumerate(stacked):
            aliases[len(args)] = 1 + k
            in_specs.append(pl.BlockSpec(memory_space=pl.ANY))
            args.append(buf)
    out_shape = [jax.ShapeDtypeStruct((T, n_nn), F32)]
    out_specs = [pl.BlockSpec((tm, n_nn), lambda i: (i, 0))]
    for w in wnts:
        out_shape.append(jax.ShapeDtypeStruct((n_layers, batch, w.shape[0], seq), F32))
        out_specs.append(pl.BlockSpec((1, 1, w.shape[0], tm), lambda i: (layer, i // tps, 0, i % tps)))
    out_shape.append(jax.ShapeDtypeStruct((batch, GATE_ROWS, seq), F32))
    out_specs.append(pl.BlockSpec((1, GATE_ROWS, tm), tok_t))
    return pl.pallas_call(
        functools.partial(_proj_kernel, n_nt=n_nt, n_alias=len(aliases), n_lin=n_lin, col_chunk=512,
                          cum_chunk=cum_chunk),
        grid=(T // tm,), in_specs=in_specs, out_specs=out_specs, out_shape=out_shape,
        input_output_aliases=aliases,
        compiler_params=_cparams("parallel"), name="norm_proj",
    )(*args)


def _cumsum_kernel(x_ref, o_ref):
    S = x_ref.shape[2]
    upto = (lax.broadcasted_iota(jnp.int32, (LANES, LANES), 0)
            <= lax.broadcasted_iota(jnp.int32, (LANES, LANES), 1)).astype(F32)
    carry = jnp.zeros((GATE_ROWS, 1), F32)
    for c in range(0, S, LANES):
        inc = jnp.dot(x_ref[0, :, c:c + LANES], upto, precision=lax.Precision.HIGHEST,
                      preferred_element_type=F32) + carry
        for p in range(GATE_ROWS // 2):
            o_ref[0, p, :, c:c + LANES] = inc[2 * p:2 * p + 2, :]
        carry = inc[:, LANES - 1:LANES]


def _seq_cumsum(x):
    B, R, S = x.shape
    return pl.pallas_call(_cumsum_kernel, grid=(B,),
                          in_specs=[pl.BlockSpec((1, R, S), lambda b: (b, 0, 0))],
                          out_specs=pl.BlockSpec((1, R // 2, 2, S), lambda b: (b, 0, 0, 0)),
                          out_shape=jax.ShapeDtypeStruct((B, R // 2, 2, S), F32),
                          compiler_params=_cparams("parallel"), name="logf_cumsum")(x)


def _fox_kernel(qi_ref, kj_ref, qt_ref, k_ref, vt_ref, crow_ref, ccol_ref, o_ref, q_s, m_s, l_s, a_s, *,
                tile, key_chunk):
    qi = qi_ref[pl.program_id(2)]
    kj = kj_ref[pl.program_id(2)]
    hd = A_HEAD_DIM

    @pl.when(kj == 0)
    def _():
        qt = qt_ref[0, 0] * np.float32(hd ** -0.5)
        row = lax.broadcasted_iota(jnp.int32, qt.shape, 0)
        q_s[0] = jnp.where(row < hd, qt, 0.0).astype(BF16)
        q_s[1] = jnp.where(row >= hd, qt, 0.0).astype(BF16)
        m_s[...] = jnp.full(m_s.shape, NEG_INF, F32)
        l_s[...] = jnp.zeros(l_s.shape, F32)
        a_s[...] = jnp.zeros(a_s.shape, F32)

    def step(diagonal):
        cq = crow_ref[0, 0]
        join = lambda old, new, c0: new if c0 == 0 else jnp.concatenate([old[:, :c0], new], axis=1)
        for hh in range(2):
            rows = slice(hh * hd, (hh + 1) * hd)
            m_run, l_run, acc = m_s[hh], l_s[hh], a_s[rows, :]
            for c in range(0, tile, key_chunk):
                c0 = c if diagonal else 0
                kb = k_ref[c:c + key_chunk, :].astype(BF16)
                s = jnp.dot(kb, q_s[hh, :, c0:], preferred_element_type=F32)
                s = s + cq[hh:hh + 1, c0:] - ccol_ref[0, 0, c:c + key_chunk, hh:hh + 1]
                if diagonal:
                    causal = (lax.broadcasted_iota(jnp.int32, s.shape, 0)
                              <= lax.broadcasted_iota(jnp.int32, s.shape, 1))
                    s = jnp.where(causal, s, NEG_INF)
                m_old = m_run[:, c0:]
                m_new = jnp.maximum(m_old, jnp.max(s, 0, keepdims=True))
                alpha = jnp.exp(m_old - m_new)
                p = jnp.exp(s - m_new)
                l_new = alpha * l_run[:, c0:] + jnp.sum(p, 0, keepdims=True)
                vt = vt_ref[0, 0, rows, c:c + key_chunk].astype(BF16)
                a_new = alpha * acc[:, c0:] + jnp.dot(vt, p.astype(BF16), preferred_element_type=F32)
                m_run, l_run, acc = join(m_run, m_new, c0), join(l_run, l_new, c0), join(acc, a_new, c0)
            m_s[hh], l_s[hh] = m_run, l_run
            a_s[rows, :] = acc

    @pl.when(kj < qi)
    def _():
        step(False)

    @pl.when(kj == qi)
    def _():
        step(True)
        row = lax.broadcasted_iota(jnp.int32, (2 * hd, tile), 0)
        out_t = a_s[...] / jnp.where(row < hd, l_s[0], l_s[1])
        o_ref[...] = out_t.T


def _fox_prompt(y, qt, vt, c, *, k_col, layer, batch, seq, tile):
    T = y.shape[0]
    pairs = A_HEADS // 2
    nt = seq // tile
    kb0 = k_col // LANES
    crow = c
    ccol = jnp.swapaxes(crow, 2, 3)
    tri = [(i, j) for i in range(nt) for j in range(i + 1)]
    qi_tab = jnp.asarray([i for i, _ in tri], jnp.int32)
    kj_tab = jnp.asarray([j for _, j in tri], jnp.int32)
    grid_spec = pltpu.PrefetchScalarGridSpec(
        num_scalar_prefetch=2, grid=(batch, pairs, len(tri)),
        in_specs=[pl.BlockSpec((1, 1, LANES, tile), lambda b, p, t, qi, kj: (layer, b, p, qi[t])),
                  pl.BlockSpec((tile, LANES), lambda b, p, t, qi, kj: (b * nt + kj[t], kb0 + p)),
                  pl.BlockSpec((1, 1, LANES, tile), lambda b, p, t, qi, kj: (layer, b, p, kj[t])),
                  pl.BlockSpec((1, 1, 2, tile), lambda b, p, t, qi, kj: (b, p, 0, qi[t])),
                  pl.BlockSpec((1, 1, tile, 2), lambda b, p, t, qi, kj: (b, p, kj[t], 0))],
        out_specs=pl.BlockSpec((tile, LANES), lambda b, p, t, qi, kj: (b * nt + qi[t], p)),
        scratch_shapes=[pltpu.VMEM((2, LANES, tile), BF16), pltpu.VMEM((2, 1, tile), F32),
                        pltpu.VMEM((2, 1, tile), F32), pltpu.VMEM((LANES, tile), F32)])
    return pl.pallas_call(
        functools.partial(_fox_kernel, tile=tile, key_chunk=tile),
        grid_spec=grid_spec, out_shape=jax.ShapeDtypeStruct((T, A_WIDTH), F32),
        compiler_params=_cparams("parallel", "parallel", "arbitrary"),
        name="fox_prompt",
    )(qi_tab, kj_tab, qt, y, vt, crow, ccol)


def _proj_attn_kernel(pt_ref, x_ref, g_ref, wnn_ref, wg_ref, gb_ref, wqt_ref, wkt_ref, wvt_ref, *rest,
                      n_alias, n_nn, col_chunk, layer, group_pages, seqs_per_step):
    rest = rest[n_alias:]
    (q_ref, kn_ref, vn_ref, lfn_ref, kc_hbm, vc_hbm, lfc_hbm,
     y_ref, qt_out, kt_out, vt_out, gt_ref, o_ref,
     kbuf, vbuf, lfbuf, sem, m_s, l_s, c_s, a_s, q_s) = rest
    P = group_pages
    step = pl.program_id(0)
    n_steps = pl.num_programs(0)
    n_pages = pt_ref.shape[1]
    G = n_pages // P
    H, hd = A_HEADS, A_HEAD_DIM
    hsl = [slice(h * hd, (h + 1) * hd) for h in range(H)]

    def group_copies(slot, page_of):
        out = []
        for i in range(P):
            pid = page_of(i)
            out.append(pltpu.make_async_copy(kc_hbm.at[layer, pid], kbuf.at[slot, i], sem.at[0, slot]))
            out.append(pltpu.make_async_copy(vc_hbm.at[layer, pid], vbuf.at[slot, i], sem.at[1, slot]))
            out.append(pltpu.make_async_copy(lfc_hbm.at[layer, pid], lfbuf.at[slot, i], sem.at[2, slot]))
        return out

    def start_group(seq, g, slot):
        for cp in group_copies(slot, lambda i: pt_ref[seq, n_pages - 1 - (g * P + i)]):
            cp.start()

    def wait_group(slot):
        for cp in group_copies(slot, lambda i: 0):
            cp.wait()

    @pl.when(step == 0)
    def _():
        start_group(0, 0, 0)

    x = x_ref[...]
    xb = (x * lax.rsqrt(jnp.mean(x * x, -1, keepdims=True) + EPS) * g_ref[...]).astype(BF16)

    def nn_chunk(c):
        def run():
            y_ref[:, c:c + col_chunk] = jnp.dot(xb, wnn_ref[:, c:c + col_chunk], preferred_element_type=F32)
        return run

    def nt_chunk(w_ref, out_ref):
        def run():
            out_ref[0, 0] = _mm_nt(w_ref[...], xb)
        return run

    def gate_rows():
        gt_ref[0] = _log_sigmoid(_mm_nt(wg_ref[...], xb) + gb_ref[...])

    tasks = [nn_chunk(c) for c in range(0, n_nn, col_chunk)]
    tasks += [nt_chunk(wqt_ref, qt_out), nt_chunk(wkt_ref, kt_out), nt_chunk(wvt_ref, vt_out), gate_rows]
    n_groups = seqs_per_step * G
    later = (lax.broadcasted_iota(jnp.int32, (LANES, LANES), 0)
             > lax.broadcasted_iota(jnp.int32, (LANES, LANES), 1)).astype(F32)
    lane = lax.broadcasted_iota(jnp.int32, (hd, LANES), 1)

    for j in range(seqs_per_step):
        seq = step * seqs_per_step + j
        qcols = [q_ref[j, hsl[h], :] * np.float32(hd ** -0.5) for h in range(H)]
        for h in range(H):
            q_s[h] = jnp.broadcast_to(qcols[h], (hd, LANES))
        m_s[...] = jnp.concatenate(
            [jnp.sum(qcols[h] * kn_ref[j, hsl[h], :], axis=0, keepdims=True) for h in range(H)], axis=0)
        l_s[...] = jnp.ones(l_s.shape, F32)
        c_s[...] = lfn_ref[j]
        for h in range(H):
            a_s[h] = jnp.where(lane == 0, vn_ref[j, hsl[h], :], 0.0)
        for g in range(G):
            gi = j * G + g
            slot = gi % 2
            if gi + 1 < n_groups:
                nj, ng = divmod(gi + 1, G)
                start_group(step * seqs_per_step + nj, ng, 1 - slot)
            else:
                @pl.when(step + 1 < n_steps)
                def _():
                    start_group((step + 1) * seqs_per_step, 0, 1 - slot)
            for t in tasks[gi * len(tasks) // n_groups:(gi + 1) * len(tasks) // n_groups]:
                t()
            wait_group(slot)
            c = c_s[...]
            lf_all = jnp.concatenate([lfbuf[slot, i] for i in range(P)], axis=0)
            excl_all = jnp.dot(lf_all, later, precision=lax.Precision.HIGHEST, preferred_element_type=F32)
            tot_all = jnp.sum(lf_all, -1, keepdims=True)
            s_pages = []
            for i in range(P):
                rows = [jnp.sum(kbuf[slot, i, h] * q_s[h], axis=0, keepdims=True) for h in range(H)]
                s_pages.append(jnp.concatenate(rows, axis=0) + (c + excl_all[i * H:(i + 1) * H, :]))
                c = c + tot_all[i * H:(i + 1) * H, :]
            c_s[...] = c
            s_all = jnp.concatenate(s_pages, axis=1)
            m_old = m_s[...]
            m_new = jnp.maximum(m_old, jnp.max(s_all, -1, keepdims=True))
            alpha = jnp.exp(m_old - m_new)
            p_all = jnp.exp(s_all - m_new)
            l_s[...] = alpha * l_s[...] + jnp.sum(p_all, -1, keepdims=True)
            m_s[...] = m_new
            for h in range(H):
                acc = a_s[h] * alpha[h:h + 1, :]
                for i in range(P):
                    acc = acc + p_all[h:h + 1, i * LANES:(i + 1) * LANES] * vbuf[slot, i, h]
                a_s[h] = acc
        l = l_s[...]
        for h in range(H):
            o_ref[j, hsl[h], :] = jnp.sum(a_s[h], axis=1, keepdims=True) / l[h:h + 1, :]


def _proj_attn(x, g, wnn, wnts, wg, gb, q, k_new, v_new, lf_new, cache_kt, cache_vt, cache_lft, page_table, *,
               n_nn, batch, seq, tm, layer, n_layers, stacked, group_pages):
    T, D = x.shape
    DB, n_pages = page_table.shape
    n_steps = T // tm
    tps = seq // tm
    spb = DB // n_steps
    P = group_pages
    assert DB == spb * n_steps and n_pages % P == 0 and (spb * (n_pages // P)) % 2 == 0
    const = lambda i, pt: (0, 0)
    col = lambda a: a.reshape(DB, a.shape[1], 1)
    vec = lambda n: pl.BlockSpec((spb, n, 1), lambda i, pt: (i, 0, 0))
    hbm = pl.BlockSpec(memory_space=pl.ANY)
    in_specs = [pl.BlockSpec((tm, D), lambda i, pt: (i, 0)), pl.BlockSpec((1, D), const),
                pl.BlockSpec((D, n_nn), const), pl.BlockSpec((GATE_ROWS, D), const),
                pl.BlockSpec((GATE_ROWS, 1), const)]
    in_specs += [pl.BlockSpec(w.shape, const) for w in wnts]
    args = [x, g.reshape(1, D), wnn, wg, gb, *wnts]
    aliases = {}
    if stacked is not None:
        for k, buf in enumerate(stacked):
            aliases[1 + len(args)] = 1 + k
            in_specs.append(hbm)
            args.append(buf)
    in_specs += [vec(A_WIDTH), vec(A_WIDTH), vec(A_WIDTH), vec(A_HEADS), hbm, hbm, hbm]
    args += [col(q), col(k_new), col(v_new), col(lf_new), cache_kt, cache_vt, cache_lft]
    nt_shape = jax.ShapeDtypeStruct((n_layers, batch, A_WIDTH, seq), F32)
    nt_spec = pl.BlockSpec((1, 1, A_WIDTH, tm), lambda i, pt: (layer, i // tps, 0, i % tps))
    out_shape = [jax.ShapeDtypeStruct((T, n_nn), F32), nt_shape, nt_shape, nt_shape,
                 jax.ShapeDtypeStruct((batch, GATE_ROWS, seq), F32),
                 jax.ShapeDtypeStruct((DB, A_WIDTH, 1), F32)]
    out_specs = [pl.BlockSpec((tm, n_nn), lambda i, pt: (i, 0)), nt_spec, nt_spec, nt_spec,
                 pl.BlockSpec((1, GATE_ROWS, tm), lambda i, pt: (i // tps, 0, i % tps)),
                 vec(A_WIDTH)]
    page = (A_HEADS, A_HEAD_DIM, LANES)
    grid_spec = pltpu.PrefetchScalarGridSpec(
        num_scalar_prefetch=1, grid=(n_steps,), in_specs=in_specs, out_specs=out_specs,
        scratch_shapes=[pltpu.VMEM((2, P) + page, F32), pltpu.VMEM((2, P) + page, F32),
                        pltpu.VMEM((2, P, A_HEADS, LANES), F32),
                        pltpu.SemaphoreType.DMA((3, 2)),
                        pltpu.VMEM((A_HEADS, 1), F32), pltpu.VMEM((A_HEADS, 1), F32),
                        pltpu.VMEM((A_HEADS, 1), F32),
                        pltpu.VMEM(page, F32), pltpu.VMEM(page, F32)])
    outs = pl.pallas_call(
        functools.partial(_proj_attn_kernel, n_alias=len(aliases), n_nn=n_nn, col_chunk=512, layer=layer,
                          group_pages=P, seqs_per_step=spb),
        grid_spec=grid_spec, out_shape=out_shape, input_output_aliases=aliases,
        compiler_params=_cparams("arbitrary"), name="proj_decode_attn",
    )(page_table, *args)
    return (*outs[:5], outs[5].reshape(DB, A_WIDTH))


def _even_mix_kernel(att_ref, za_ref, u_ref, vb_ref, zb_ref, h_ref, lng_ref, lnb_ref, ws_ref, bs_ref,
                     wout_ref, *outs, decode):
    o_ref = outs[0]
    ya = att_ref[...] * _silu(za_ref[...])
    u = _gelu(u_ref[...])
    vf = _gelu(vb_ref[...])
    mu = jnp.mean(vf, -1, keepdims=True)
    var = jnp.mean((vf - mu) ** 2, -1, keepdims=True)
    vn = (vf - mu) * lax.rsqrt(var + EPS) * lng_ref[...] + lnb_ref[...]
    tm, bw = vn.shape
    if decode:
        outs[1][...] = vn
        mix = vn * ws_ref[...] + bs_ref[...]
    else:
        lane = lax.broadcasted_iota(jnp.int32, (B_CHUNK, LANES), 1)
        tri = (lax.broadcasted_iota(jnp.int32, (B_CHUNK, B_CHUNK), 0)
               >= lax.broadcasted_iota(jnp.int32, (B_CHUNK, B_CHUNK), 1))
        wtril = [jnp.where(tri, ws_ref[g], 0.0).astype(BF16) for g in range(B_GROUPS)]
        gpl = LANES // (bw // B_GROUPS)
        rows = []
        for c in range(0, tm, B_CHUNK):
            blocks = []
            for lb in range(bw // LANES):
                vblk = vn[c:c + B_CHUNK, lb * LANES:(lb + 1) * LANES].astype(BF16)
                y0 = jnp.dot(wtril[lb * gpl], vblk, preferred_element_type=F32)
                y1 = jnp.dot(wtril[lb * gpl + 1], vblk, preferred_element_type=F32)
                blocks.append(jnp.where(lane < LANES // gpl, y0, y1))
            rows.append(jnp.concatenate(blocks, axis=1) + bs_ref[...])
        mix = jnp.concatenate(rows, axis=0)
    yb = u * mix * _silu(zb_ref[...])
    aw = ya.shape[1]
    hn = h_ref[...] + _mm(ya, wout_ref[:aw, :]) + _mm(yb, wout_ref[aw:, :])
    o_ref[...] = hn


def _even_mix(att, y, h, ln_g, ln_b, ws, bs, wout, *, col0, tm, decode):
    T, D = h.shape
    aw = att.shape[1]
    const2 = lambda i: (0, 0)
    yblk = lambda k: pl.BlockSpec((tm, aw), lambda i: (i, col0 + k))
    ws_spec = (pl.BlockSpec(ws.shape, const2) if decode else pl.BlockSpec(ws.shape, lambda i: (0, 0, 0)))
    in_specs = [pl.BlockSpec((tm, aw), lambda i: (i, 0)), yblk(0), yblk(1), yblk(2), yblk(3),
                pl.BlockSpec((tm, D), lambda i: (i, 0)),
                pl.BlockSpec((1, aw), const2), pl.BlockSpec((1, aw), const2),
                ws_spec, pl.BlockSpec(bs.shape, const2), pl.BlockSpec(wout.shape, const2)]
    out_shape = [jax.ShapeDtypeStruct((T, D), F32)]
    out_specs = [pl.BlockSpec((tm, D), lambda i: (i, 0))]
    if decode:
        out_shape.append(jax.ShapeDtypeStruct((T, aw), F32))
        out_specs.append(pl.BlockSpec((tm, aw), lambda i: (i, 0)))
    return pl.pallas_call(
        functools.partial(_even_mix_kernel, decode=decode),
        grid=(T // tm,), in_specs=in_specs, out_specs=out_specs, out_shape=out_shape,
        compiler_params=_cparams("parallel"), name="even_mix",
    )(att, y, y, y, y, h, ln_g.reshape(1, aw), ln_b.reshape(1, aw), ws, bs, wout)


def _qkv_decode_kernel(xc_ref, prev_ref, cw_ref, cb_ref, wq_ref, wk_ref, wv_ref,
                       xconv_ref, q_ref, k_ref, v_ref, nb_ref):
    xc = xc_ref[...]
    hd = xc.shape[1] // C_HEADS
    acc = cb_ref[...] + cw_ref[CONV_W - 1:CONV_W, :] * xc
    for j in range(CONV_W - 1):
        acc = acc + cw_ref[j:j + 1, :] * prev_ref[j]
    for j in range(CONV_W - 2):
        nb_ref[j] = prev_ref[j + 1]
    nb_ref[CONV_W - 2] = xc
    xconv = _silu(acc)
    xconv_ref[...] = xconv
    for h in range(C_HEADS):
        sl = slice(h * hd, (h + 1) * hd)
        q_ref[:, sl] = _mm(xconv[:, sl], wq_ref[h]) * np.float32(hd ** -0.5)
        k_ref[:, sl] = _mm(xconv[:, sl], wk_ref[h])
        v_ref[:, sl] = _mm(xc[:, sl], wv_ref[h])


def _qkv_decode(y, prev, cw, cb, wq, wk, wv):
    T = y.shape[0]
    W = cw.shape[1]
    const2 = lambda i: (0, 0)
    const3 = lambda i: (0, 0, 0)
    row_blk = pl.BlockSpec((T, W), lambda i: (0, 0))
    row_out = jax.ShapeDtypeStruct((T, W), F32)
    return pl.pallas_call(
        _qkv_decode_kernel, grid=(1,),
        in_specs=[row_blk, pl.BlockSpec(prev.shape, const3), pl.BlockSpec(cw.shape, const2),
                  pl.BlockSpec((1, W), const2), pl.BlockSpec(wq.shape, const3), pl.BlockSpec(wk.shape, const3),
                  pl.BlockSpec(wv.shape, const3)],
        out_specs=[row_blk, row_blk, row_blk, row_blk, pl.BlockSpec(prev.shape, const3)],
        out_shape=[row_out, row_out, row_out, row_out, jax.ShapeDtypeStruct(prev.shape, F32)],
        compiler_params=_cparams("arbitrary"), name="conv_qkv_decode",
    )(y, prev, cw, cb.reshape(1, W), wq, wk, wv)


def _mlstm_decode_kernel(q_ref, k_ref, v_ref, g_ref, o_ref, z_ref, xconv_ref, ng_ref, skip_ref,
                         c0_ref, n0_ref, m0_ref, *rest, nb, n_alias):
    y_ref, c_out, n_out, m_out = rest[n_alias:]
    W = q_ref.shape[2]
    hd = W // C_HEADS
    first = lax.broadcasted_iota(jnp.int32, (8, hd), 0) == 0
    pad8 = lambda r: jnp.where(first, jnp.broadcast_to(r, (8, hd)), 0.0)
    for b in range(nb):
        ys = []
        for h in range(C_HEADS):
            sl = slice(h * hd, (h + 1) * hd)
            q, k, v = q_ref[b, :, sl], k_ref[b, :, sl], v_ref[b, :, sl]
            log_i, log_f = g_ref[b, h:h + 1, :], g_ref[b, C_HEADS + h:C_HEADS + h + 1, :]
            c0, n0, m0 = c0_ref[0, b, h], n0_ref[0, b, h:h + 1, :], m0_ref[0, b, h:h + 1, :]
            m_new = jnp.maximum(log_f + m0, log_i)
            a = jnp.exp(log_f + m0 - m_new)
            w = jnp.exp(log_i - m_new)
            sm = w * jnp.sum(q * k, -1, keepdims=True)
            num = sm * v + a * _mm(pad8(q), c0)[0:1, :]
            den = sm + a * jnp.sum(q * n0, -1, keepdims=True)
            hc = num / jnp.maximum(jnp.abs(den), jnp.exp(-m_new))
            ktv = lax.dot_general(pad8(k).astype(BF16), pad8(v).astype(BF16), (((0,), (0,)), ((), ())),
                                  preferred_element_type=F32)
            c_out[0, b, h] = a * c0 + w * ktv
            n_out[b, h:h + 1, :] = a * n0 + w * k
            m_out[b, h:h + 1, :] = m_new
            ys.append(_mlstm_gate(hc, o_ref[b, :, sl], z_ref[b, :, sl], xconv_ref[b, :, sl],
                                  ng_ref[:, sl], skip_ref[:, sl]))
        y_ref[b] = jnp.concatenate(ys, axis=1)


def _mlstm_decode(q, k, v, gates, y_in, xconv, ng, skip, state, c_stack, *, layer, n_layers, nb):
    batch, W = q.shape[0], q.shape[2]
    hd = W // C_HEADS
    const2 = lambda b: (0, 0)
    rowblk = lambda k: pl.BlockSpec((nb, 1, W), lambda b: (b, 0, k))
    c0, n0, m0 = state
    in_specs = [rowblk(0), rowblk(0), rowblk(0),
                pl.BlockSpec((nb, GATE_ROWS, 1), lambda b: (b, 0, 0)),
                rowblk(2), rowblk(1), rowblk(0),
                pl.BlockSpec((1, W), const2), pl.BlockSpec((1, W), const2),
                pl.BlockSpec((1, nb, C_HEADS, hd, hd), lambda b: (layer, b, 0, 0, 0)),
                pl.BlockSpec((1, nb, C_HEADS, hd), lambda b: (layer, b, 0, 0)),
                pl.BlockSpec((1, nb, C_HEADS, 1), lambda b: (layer, b, 0, 0))]
    args = [q, k, v, gates, y_in, y_in, xconv, ng.reshape(1, W), skip.reshape(1, W),
            c0, n0, m0.reshape(m0.shape + (1,))]
    aliases = {}
    if c_stack is not None:
        aliases[len(args)] = 1
        in_specs.append(pl.BlockSpec(memory_space=pl.ANY))
        args.append(c_stack)
    out_shape = [jax.ShapeDtypeStruct((batch, 1, W), F32),
                 jax.ShapeDtypeStruct((n_layers, batch, C_HEADS, hd, hd), F32),
                 jax.ShapeDtypeStruct((batch, C_HEADS, hd), F32),
                 jax.ShapeDtypeStruct((batch, C_HEADS, 1), F32)]
    out_specs = [pl.BlockSpec((nb, 1, W), lambda b: (b, 0, 0)),
                 pl.BlockSpec((1, nb, C_HEADS, hd, hd), lambda b: (layer, b, 0, 0, 0)),
                 pl.BlockSpec((nb, C_HEADS, hd), lambda b: (b, 0, 0)),
                 pl.BlockSpec((nb, C_HEADS, 1), lambda b: (b, 0, 0))]
    y, c_new, n_new, m_new = pl.pallas_call(
        functools.partial(_mlstm_decode_kernel, nb=nb, n_alias=len(aliases)),
        grid=(batch // nb,), in_specs=in_specs, out_specs=out_specs, out_shape=out_shape,
        input_output_aliases=aliases,
        compiler_params=_cparams("parallel"), name="mlstm_decode",
    )(*args)
    return y, c_new, n_new, m_new.reshape(batch, C_HEADS)


def _mlstm_head(qh, kth, vh, i_row, b_row, m_prev, caug, tri, one_col):
    L, hd = qh.shape
    g_row = i_row - b_row
    dm = jnp.where(tri, g_row, NEG_INF)
    mcol = jnp.maximum(m_prev, jnp.max(dm, -1, keepdims=True))
    wmat = jnp.exp(dm - mcol)
    a = jnp.exp(m_prev - mcol)
    sm = wmat * _mm(qh, kth)
    qc = _mm(qh, caug)
    num = _mm(sm, vh) + a * qc[:, :hd]
    den = jnp.sum(sm, -1, keepdims=True) + a * qc[:, hd:hd + 1]
    den = jnp.maximum(jnp.abs(den), jnp.exp(-(_row_to_col(b_row) + mcol)))
    m_last = mcol[L - 1:L, :]
    wl = jnp.exp(g_row - m_last)
    a_l = jnp.exp(m_prev - m_last)
    caug_new = a_l * caug + _mm(kth * wl, jnp.concatenate([vh, one_col], axis=1))
    return num / den, caug_new, b_row[:, L - 1:L] + m_last


def _mlstm_gate(hc, o, z, xconv, ng, skip):
    mu = jnp.mean(hc, -1, keepdims=True)
    var = jnp.mean((hc - mu) ** 2, -1, keepdims=True)
    hn = (hc - mu) * lax.rsqrt(var + EPS) * ng
    return (_sigmoid(o) * hn + skip * xconv) * _silu(z)


def _mlstm_prompt_kernel(xc_ref, halo_ref, z_ref, o_ref, g_ref, h_ref, cw_ref, cb_ref, wq_ref, wkt_ref, wv_ref,
                         ng_ref, skip_ref, wout_ref, *rest, nb, n_alias, final):
    if final:
        fg_ref = rest[0]
        rest = rest[1:]
    hout_ref, c_out, n_out, m_out, caug_s, m_s = rest[n_alias:]
    L, W = xc_ref.shape[1], xc_ref.shape[2]
    hd = W // C_HEADS
    ci = pl.program_id(1)

    @pl.when(ci == 0)
    def _():
        caug_s[...] = jnp.zeros(caug_s.shape, F32)
        m_s[...] = jnp.zeros(m_s.shape, F32)

    xcs, xconvs = [], []
    for b in range(nb):
        xc = xc_ref[b]
        halo = jnp.where(ci == 0, 0.0, halo_ref[b])
        xx = jnp.concatenate([halo, xc], axis=0)
        acc = cb_ref[...] + cw_ref[CONV_W - 1:CONV_W, :] * xc
        for k in range(1, CONV_W):
            acc = acc + cw_ref[CONV_W - 1 - k:CONV_W - k, :] * pltpu.roll(xx, k, 0)[8:]
        xcs.append(xc)
        xconvs.append(_silu(acc))
    xc_all = jnp.concatenate(xcs, axis=0)
    xconv_all = jnp.concatenate(xconvs, axis=0)
    tri = (lax.broadcasted_iota(jnp.int32, (L, L), 0) >= lax.broadcasted_iota(jnp.int32, (L, L), 1))
    one_col = (lax.broadcasted_iota(jnp.int32, (L, LANES), 1) == 0).astype(F32)
    ys = [[None] * C_HEADS for _ in range(nb)]
    for h in range(C_HEADS):
        sl = slice(h * hd, (h + 1) * hd)
        xh = xconv_all[:, sl]
        q_h = _mm(xh, wq_ref[h]) * np.float32(hd ** -0.5)
        kt_h = _mm_nt(wkt_ref[h], xh)
        v_h = _mm(xc_all[:, sl], wv_ref[h])
        for b in range(nb):
            r = slice(b * L, (b + 1) * L)
            gates = g_ref[b]
            hc, caug_new, m_new = _mlstm_head(
                q_h[r], kt_h[:, r], v_h[r], gates[h:h + 1, :], gates[C_HEADS + h:C_HEADS + h + 1, :],
                m_s[b, h][0:1, 0:1], caug_s[b, h], tri, one_col)
            caug_s[b, h] = caug_new
            m_s[b, h] = jnp.broadcast_to(m_new, (8, LANES))
            ys[b][h] = _mlstm_gate(hc, o_ref[b][:, sl], z_ref[b][:, sl], xconv_all[r, sl],
                                   ng_ref[:, sl], skip_ref[:, sl])
    y_all = jnp.concatenate([jnp.concatenate(ys[b], axis=1) for b in range(nb)], axis=0)
    hn = jnp.concatenate([h_ref[b] for b in range(nb)], axis=0) + _mm(y_all, wout_ref[...])
    if final:
        hn = hn * lax.rsqrt(jnp.mean(hn * hn, -1, keepdims=True) + EPS) * fg_ref[...]
    for b in range(nb):
        hout_ref[b] = hn[b * L:(b + 1) * L]

    @pl.when(ci == pl.num_programs(1) - 1)
    def _():
        for b in range(nb):
            for h in range(C_HEADS):
                c_out[0, b, h] = caug_s[b, h, :, :hd]
                n_out[b, h] = caug_s[b, h, :, hd:hd + 1]
                m_out[b, h:h + 1, :] = m_s[b, h][0:1, 0:1]


def _mlstm_prompt(y_in, gates, h, cw, cb, wq, wkt, wv, ng, skip, wout, final_g, c_stack, *, layer, n_layers, nb):
    batch, S = y_in.shape[0], y_in.shape[1]
    W, D = wout.shape
    hd = W // C_HEADS
    L = C_CHUNK
    const2 = lambda b, c: (0, 0)
    const3 = lambda b, c: (0, 0, 0)
    yblk = lambda k: pl.BlockSpec((nb, L, W), lambda b, c: (b, c, k))
    in_specs = [yblk(0),
                pl.BlockSpec((nb, 8, W), lambda b, c: (b, jnp.maximum(c * (L // 8) - 1, 0), 0)),
                yblk(1), yblk(2),
                pl.BlockSpec((nb, GATE_ROWS, L), lambda b, c: (b, 0, c)),
                pl.BlockSpec((nb, L, D), lambda b, c: (b, c, 0)),
                pl.BlockSpec(cw.shape, const2), pl.BlockSpec((1, W), const2),
                pl.BlockSpec(wq.shape, const3), pl.BlockSpec(wkt.shape, const3), pl.BlockSpec(wv.shape, const3),
                pl.BlockSpec((1, W), const2), pl.BlockSpec((1, W), const2), pl.BlockSpec(wout.shape, const2)]
    args = [y_in, y_in, y_in, y_in, gates, h, cw, cb.reshape(1, W), wq, wkt, wv,
            ng.reshape(1, W), skip.reshape(1, W), wout]
    if final_g is not None:
        in_specs.append(pl.BlockSpec((1, D), const2))
        args.append(final_g.reshape(1, D))
    aliases = {}
    if c_stack is not None:
        aliases[len(args)] = 1
        in_specs.append(pl.BlockSpec(memory_space=pl.ANY))
        args.append(c_stack)
    out_shape = [jax.ShapeDtypeStruct((batch, S, D), F32),
                 jax.ShapeDtypeStruct((n_layers, batch, C_HEADS, hd, hd), F32),
                 jax.ShapeDtypeStruct((batch, C_HEADS, hd, 1), F32),
                 jax.ShapeDtypeStruct((batch, C_HEADS, 1), F32)]
    out_specs = [pl.BlockSpec((nb, L, D), lambda b, c: (b, c, 0)),
                 pl.BlockSpec((1, nb, C_HEADS, hd, hd), lambda b, c: (layer, b, 0, 0, 0)),
                 pl.BlockSpec((nb, C_HEADS, hd, 1), lambda b, c: (b, 0, 0, 0)),
                 pl.BlockSpec((nb, C_HEADS, 1), lambda b, c: (b, 0, 0))]
    hout, c_new, n_new, m_new = pl.pallas_call(
        functools.partial(_mlstm_prompt_kernel, nb=nb, n_alias=len(aliases), final=final_g is not None),
        grid=(batch // nb, S // L), in_specs=in_specs, out_specs=out_specs, out_shape=out_shape,
        input_output_aliases=aliases,
        scratch_shapes=[pltpu.VMEM((nb, C_HEADS, hd, hd + LANES), F32), pltpu.VMEM((nb, C_HEADS, 8, LANES), F32)],
        compiler_params=_cparams("parallel", "arbitrary"), name="mlstm_layer",
    )(*args)
    return hout, c_new, n_new.reshape(batch, C_HEADS, hd), m_new.reshape(batch, C_HEADS)


def _out_proj_kernel(y_ref, w_ref, h_ref, *rest, final):
    hn = h_ref[...] + _mm(y_ref[...], w_ref[...])
    if final:
        fg_ref, o_ref = rest
        o_ref[...] = hn * lax.rsqrt(jnp.mean(hn * hn, -1, keepdims=True) + EPS) * fg_ref[...]
    else:
        rest[0][...] = hn


def _out_proj(y, w, h, final_g, *, tm):
    T, D = h.shape
    K = y.shape[1]
    const2 = lambda i: (0, 0)
    in_specs = [pl.BlockSpec((tm, K), lambda i: (i, 0)), pl.BlockSpec(w.shape, const2),
                pl.BlockSpec((tm, D), lambda i: (i, 0))]
    args = [y, w, h]
    if final_g is not None:
        in_specs.append(pl.BlockSpec((1, D), const2))
        args.append(final_g.reshape(1, D))
    return pl.pallas_call(
        functools.partial(_out_proj_kernel, final=final_g is not None),
        grid=(T // tm,), in_specs=in_specs, out_specs=pl.BlockSpec((tm, D), lambda i: (i, 0)),
        out_shape=jax.ShapeDtypeStruct((T, D), F32),
        compiler_params=_cparams("parallel"), name="out_proj",
    )(*args)


def kernel(x_prompt, x_sample, cache_k, cache_v, cache_logf, state_c, state_n, state_m, state_conv, page_table,
           norm_g, final_g, even_w_in, even_b_f, gmlp_ln_g, gmlp_ln_b, gmlp_w_s, gmlp_b_s, even_w_out,
           odd_w_in, odd_b_i, odd_b_f, conv_w, conv_b, mlstm_w_q, mlstm_w_k, mlstm_w_v, mlstm_norm_g,
           mlstm_skip, odd_w_out):
    B, S, D = x_prompt.shape
    DB = x_sample.shape[0]
    depth = norm_g.shape[0]
    AW = A_WIDTH
    BW = gmlp_ln_g.shape[1]
    CW = conv_w.shape[2]
    gdim = BW // B_GROUPS
    tm = 512
    assert AW == BW, "even-layer column blocks are addressed in units of one common width"

    hp = x_prompt.reshape(B * S, D)
    hs = x_sample.reshape(DB, D)
    cache_kt = jnp.transpose(cache_k, (0, 1, 3, 4, 2))
    cache_vt = jnp.transpose(cache_v, (0, 1, 3, 4, 2))
    cache_lft = jnp.transpose(cache_logf, (0, 1, 3, 2))

    lp, ks, vs, ls, chv = [], [], [], [], []
    np_, mp, bp, ns, ms, bs = [], [], [], [], [], []
    n_even, n_odd = (depth + 1) // 2, depth // 2
    assert depth % 2 == 0, "the final rmsnorm is fused into the last (mLSTM) layer's output projection"
    qkv_stack = cp_stack = cs_stack = None
    for l in range(depth):
        j = l // 2
        last = l == depth - 1
        if l % 2 == 0:
            w = even_w_in[j]
            o = 0
            parts = {}
            for name, n in (("q", AW), ("k", AW), ("v", AW), ("fg", A_HEADS), ("za", AW), ("u", BW), ("vb", BW), ("zb", BW)):
                parts[name] = w[:, o:o + n]
                o += n
            wnn_s = jnp.concatenate([parts[n] for n in ("za", "u", "vb", "zb", "k", "q", "v")], 1).astype(BF16)
            wqt = parts["q"].T.astype(BF16)
            wkt = parts["k"].T.astype(BF16)
            wvt = parts["v"].T.astype(BF16)
            wg = parts["fg"].T.astype(BF16)
            gb = even_b_f[j].reshape(GATE_ROWS, 1)
            wout = even_w_out[j].astype(BF16)
            bs_full = jnp.repeat(gmlp_b_s[j].T, gdim, axis=1)
            ws_dec = jnp.repeat(gmlp_w_s[j][:, 0, 0], gdim).reshape(1, BW)
            bs_dec = jnp.repeat(gmlp_b_s[j][:, 0], gdim).reshape(1, BW)

            ys, lf = _proj(hs, norm_g[l], wnn_s, [], wg, gb, n_lin=0, batch=1, seq=DB, tm=DB)
            lf_s = lf[0].T
            k_s, q_s, v_s = ys[:, 4 * AW:5 * AW], ys[:, 5 * AW:6 * AW], ys[:, 6 * AW:7 * AW]

            y, qt, kt_all, vt_all, lf, att_s = _proj_attn(
                hp, norm_g[l], wnn_s, [wqt, wkt, wvt], wg, gb, q_s, k_s, v_s, lf_s,
                cache_kt, cache_vt, cache_lft, page_table, n_nn=5 * AW, batch=B, seq=S, tm=tm,
                layer=j, n_layers=n_even, stacked=qkv_stack, group_pages=8)
            qkv_stack = [qt, kt_all, vt_all]
            c = _seq_cumsum(lf)
            att = _fox_prompt(y, qt, vt_all, c, k_col=4 * AW, layer=j, batch=B, seq=S, tile=512)
            hp = _even_mix(att, y, hp, gmlp_ln_g[j], gmlp_ln_b[j], gmlp_w_s[j], bs_full, wout,
                           col0=0, tm=tm, decode=False)[0]
            lp.append(jnp.transpose(lf, (0, 2, 1)))

            hs, vn_s = _even_mix(att_s, ys, hs, gmlp_ln_g[j], gmlp_ln_b[j], ws_dec, bs_dec, wout,
                                 col0=0, tm=DB, decode=True)
            ks.append(k_s.reshape(DB, 1, A_HEADS, A_HEAD_DIM))
            vs.append(v_s.reshape(DB, 1, A_HEADS, A_HEAD_DIM))
            ls.append(lf_s.reshape(DB, 1, A_HEADS))
            chv.append(vn_s.reshape(DB, 1, BW))
        else:
            w = odd_w_in[j]
            wnn = w[:, :3 * CW].astype(BF16)
            wg = w[:, 3 * CW:].T.astype(BF16)
            gb = jnp.concatenate([odd_b_i[j], odd_b_f[j]]).reshape(GATE_ROWS, 1)
            wq = mlstm_w_q[j].astype(BF16)
            wk = mlstm_w_k[j].astype(BF16)
            wkT = jnp.swapaxes(mlstm_w_k[j], 1, 2).astype(BF16)
            wv = mlstm_w_v[j].astype(BF16)
            wout = odd_w_out[j].astype(BF16)
            fin = final_g if last else None

            y, g = _proj(hp, norm_g[l], wnn, [], wg, gb, n_lin=C_HEADS, batch=B, seq=S, tm=tm, cum_chunk=C_CHUNK)
            hout, cp_stack, n1, m1 = _mlstm_prompt(
                y.reshape(B, S, 3 * CW), g, hp.reshape(B, S, D), conv_w[j], conv_b[j], wq, wkT, wv,
                mlstm_norm_g[j], mlstm_skip[j], wout, fin, cp_stack, layer=j, n_layers=n_odd, nb=2)
            hp = hout.reshape(B * S, D)
            np_.append(n1); mp.append(m1)
            bp.append(y.reshape(B, S, 3 * CW)[:, S - (CONV_W - 1):, :CW])

            y, g = _proj(hs, norm_g[l], wnn, [], wg, gb, n_lin=C_HEADS, batch=1, seq=DB, tm=DB)
            buf = jnp.transpose(state_conv[j], (1, 0, 2))
            xconv, q, k, v, nbuf = _qkv_decode(y, buf, conv_w[j], conv_b[j], wq, wk, wv)
            g_s = jnp.transpose(g, (2, 1, 0))
            r3 = lambda a: a.reshape(DB, 1, a.shape[1])
            ym, cs_stack, n2, m2 = _mlstm_decode(r3(q), r3(k), r3(v), g_s, r3(y), r3(xconv), mlstm_norm_g[j],
                                                 mlstm_skip[j], (state_c, state_n, state_m), cs_stack,
                                                 layer=j, n_layers=n_odd, nb=4)
            hs = _out_proj(ym.reshape(DB, CW), wout, hs, fin, tm=DB)
            ns.append(n2); ms.append(m2)
            bs.append(jnp.transpose(nbuf, (1, 0, 2)))
    y_prompt = hp.reshape(B, S, D)
    y_sample = hs.reshape(DB, 1, D)
    to_bshd = lambda t: jnp.transpose(t.reshape(n_even, B, A_HEADS, A_HEAD_DIM, S), (0, 1, 4, 2, 3))
    return (y_prompt, y_sample,
            to_bshd(qkv_stack[1]), to_bshd(qkv_stack[2]), jnp.stack(lp),
            jnp.stack(ks), jnp.stack(vs), jnp.stack(ls), jnp.stack(chv),
            cp_stack, jnp.stack(np_), jnp.stack(mp), jnp.stack(bp),
            cs_stack, jnp.stack(ns), jnp.stack(ms), jnp.stack(bs))
```

```python
import functools

import jax
import jax.numpy as jnp
import numpy as np
from jax import lax
from jax.experimental import pallas as pl
from jax.experimental.pallas import tpu as pltpu

F32 = jnp.float32
BF16 = jnp.bfloat16
EPS = 1e-6
NEG_INF = float("-inf")

A_HEADS = 8
A_HEAD_DIM = 64
A_WIDTH = A_HEADS * A_HEAD_DIM
B_GROUPS = 8
B_CHUNK = 128
C_HEADS = 4
C_CHUNK = 128
CONV_W = 4
LANES = 128
GATE_ROWS = 8
VMEM_LIMIT = 56 * 1024 * 1024


def _cparams(*sem):
    return pltpu.CompilerParams(dimension_semantics=sem, vmem_limit_bytes=VMEM_LIMIT)


def _mm(a, b):
    return jnp.dot(a.astype(BF16), b.astype(BF16), preferred_element_type=F32)


def _mm_nt(a, b):
    return lax.dot_general(a.astype(BF16), b.astype(BF16), (((1,), (1,)), ((), ())),
                           preferred_element_type=F32)


def _log_sigmoid(x):
    return jnp.minimum(x, 0.0) - jnp.log1p(jnp.exp(-jnp.abs(x)))


def _sigmoid(x):
    return 1.0 / (1.0 + jnp.exp(-x))


def _silu(x):
    return x * _sigmoid(x)


def _gelu(x):
    return 0.5 * x * (1.0 + lax.erf(x * np.float32(np.sqrt(0.5))))


def _row_to_col(r):
    n = r.shape[1]
    eye = lax.broadcasted_iota(jnp.int32, (n, n), 0) == lax.broadcasted_iota(jnp.int32, (n, n), 1)
    return jnp.sum(jnp.where(eye, r, 0.0), axis=1, keepdims=True)


def _proj_kernel(x_ref, g_ref, wnn_ref, wg_ref, gb_ref, *rest, n_nt, n_alias, n_lin, col_chunk, cum_chunk):
    wnt_refs = rest[:n_nt]
    rest = rest[n_nt + n_alias:]
    y_ref = rest[0]
    yt_refs = rest[1:1 + n_nt]
    gt_ref = rest[1 + n_nt]
    x = x_ref[...]
    xn = x * lax.rsqrt(jnp.mean(x * x, -1, keepdims=True) + EPS) * g_ref[...]
    xb = xn.astype(BF16)
    n_nn = y_ref.shape[1]
    for c in range(0, n_nn, col_chunk):
        y_ref[:, c:c + col_chunk] = jnp.dot(xb, wnn_ref[:, c:c + col_chunk], preferred_element_type=F32)
    for w_ref, o_ref in zip(wnt_refs, yt_refs):
        o_ref[0, 0] = _mm_nt(w_ref[...], xb)
    gt = _mm_nt(wg_ref[...], xb) + gb_ref[...]
    row = lax.broadcasted_iota(jnp.int32, gt.shape, 0)
    gt = jnp.where(row >= n_lin, _log_sigmoid(gt), gt)
    if cum_chunk is None:
        gt_ref[0] = gt
    else:
        upto = (lax.broadcasted_iota(jnp.int32, (cum_chunk, cum_chunk), 0)
                <= lax.broadcasted_iota(jnp.int32, (cum_chunk, cum_chunk), 1)).astype(F32)
        rowc = row[:, :cum_chunk]
        for c in range(0, gt.shape[1], cum_chunk):
            blk = gt[:, c:c + cum_chunk]
            cs = jnp.dot(blk, upto, precision=lax.Precision.HIGHEST, preferred_element_type=F32)
            gt_ref[0, :, c:c + cum_chunk] = jnp.where(rowc >= n_lin, cs, blk)


def _proj(x, g, wnn, wnts, wg, gb, *, n_lin, batch, seq, tm, n_nn=None, cum_chunk=None, layer=0, n_layers=1,
          stacked=None):
    T, D = x.shape
    tps = seq // tm
    n_nn = wnn.shape[1] if n_nn is None else n_nn
    n_nt = len(wnts)
    const = lambda i: (0, 0)
    tok_t = lambda i: (i // tps, 0, i % tps)
    in_specs = [pl.BlockSpec((tm, D), lambda i: (i, 0)),
                pl.BlockSpec((1, D), const),
                pl.BlockSpec((D, n_nn), const),
                pl.BlockSpec((GATE_ROWS, D), const),
                pl.BlockSpec((GATE_ROWS, 1), const)]
    in_specs += [pl.BlockSpec(w.shape, const) for w in wnts]
    args = [x, g.reshape(1, D), wnn, wg, gb, *wnts]
    aliases = {}
    if stacked is not None:
        for k, buf in enumerate(stacked):
            aliases[len(args)] = 1 + k
            in_specs.append(pl.BlockSpec(memory_space=pl.ANY))
            args.append(buf)
    out_shape = [jax.ShapeDtypeStruct((T, n_nn), F32)]
    out_specs = [pl.BlockSpec((tm, n_nn), lambda i: (i, 0))]
    for w in wnts:
        out_shape.append(jax.ShapeDtypeStruct((n_layers, batch, w.shape[0], seq), F32))
        out_specs.append(pl.BlockSpec((1, 1, w.shape[0], tm), lambda i: (layer, i // tps, 0, i % tps)))
    out_shape.append(jax.ShapeDtypeStruct((batch, GATE_ROWS, seq), F32))
    out_specs.append(pl.BlockSpec((1, GATE_ROWS, tm), tok_t))
    return pl.pallas_call(
        functools.partial(_proj_kernel, n_nt=n_nt, n_alias=len(aliases), n_lin=n_lin, col_chunk=512,
                          cum_chunk=cum_chunk),
        grid=(T // tm,), in_specs=in_specs, out_specs=out_specs, out_shape=out_shape,
        input_output_aliases=aliases,
        compiler_params=_cparams("parallel"), name="norm_proj",
    )(*args)


def _cumsum_kernel(x_ref, o_ref):
    S = x_ref.shape[2]
    upto = (lax.broadcasted_iota(jnp.int32, (LANES, LANES), 0)
            <= lax.broadcasted_iota(jnp.int32, (LANES, LANES), 1)).astype(F32)
    carry = jnp.zeros((GATE_ROWS, 1), F32)
    for c in range(0, S, LANES):
        inc = jnp.dot(x_ref[0, :, c:c + LANES], upto, precision=lax.Precision.HIGHEST,
                      preferred_element_type=F32) + carry
        for p in range(GATE_ROWS // 2):
            o_ref[0, p, :, c:c + LANES] = inc[2 * p:2 * p + 2, :]
        carry = inc[:, LANES - 1:LANES]


def _seq_cumsum(x):
    B, R, S = x.shape
    return pl.pallas_call(_cumsum_kernel, grid=(B,),
                          in_specs=[pl.BlockSpec((1, R, S), lambda b: (b, 0, 0))],
                          out_specs=pl.BlockSpec((1, R // 2, 2, S), lambda b: (b, 0, 0, 0)),
                          out_shape=jax.ShapeDtypeStruct((B, R // 2, 2, S), F32),
                          compiler_params=_cparams("parallel"), name="logf_cumsum")(x)


def _fox_kernel(qi_ref, kj_ref, pt_ref, qt_ref, k_ref, vt_ref, crow_ref, ccol_ref,
                dq_ref, dkn_ref, dvn_ref, dlf_ref, kc_hbm, vc_hbm, lfc_hbm, o_ref, do_ref,
                q_s, m_s, l_s, a_s, *decode_scratch, tile, key_chunk, layer, group_pages):
    qi = qi_ref[pl.program_id(2)]
    kj = kj_ref[pl.program_id(2)]
    hd = A_HEAD_DIM
    step = (pl.program_id(0) * pl.num_programs(1) + pl.program_id(1)) * pl.num_programs(2) + pl.program_id(2)
    seq_groups = pt_ref.shape[1] // group_pages
    n_groups = pt_ref.shape[0] * seq_groups
    start_group, wait_group, init_seq, compute_group, finish_seq = _paged_decode_ops(
        pt_ref, dq_ref, dkn_ref, dvn_ref, dlf_ref, kc_hbm, vc_hbm, lfc_hbm, do_ref, *decode_scratch,
        layer=layer, group_pages=group_pages)
    slot = step % 2

    @pl.when(step == 0)
    def _():
        start_group(0, 0, 0)

    @pl.when(step + 1 < n_groups)
    def _():
        start_group((step + 1) // seq_groups, (step + 1) % seq_groups, 1 - slot)

    @pl.when(kj == 0)
    def _():
        qt = qt_ref[0, 0] * np.float32(hd ** -0.5)
        row = lax.broadcasted_iota(jnp.int32, qt.shape, 0)
        q_s[0] = jnp.where(row < hd, qt, 0.0).astype(BF16)
        q_s[1] = jnp.where(row >= hd, qt, 0.0).astype(BF16)
        m_s[...] = jnp.full(m_s.shape, NEG_INF, F32)
        l_s[...] = jnp.zeros(l_s.shape, F32)
        a_s[...] = jnp.zeros(a_s.shape, F32)

    def attend(diagonal):
        cq = crow_ref[0, 0]
        join = lambda old, new, c0: new if c0 == 0 else jnp.concatenate([old[:, :c0], new], axis=1)
        for hh in range(2):
            rows = slice(hh * hd, (hh + 1) * hd)
            m_run, l_run, acc = m_s[hh], l_s[hh], a_s[rows, :]
            for c in range(0, tile, key_chunk):
                c0 = c if diagonal else 0
                kb = k_ref[c:c + key_chunk, :].astype(BF16)
                s = jnp.dot(kb, q_s[hh, :, c0:], preferred_element_type=F32)
                s = s + cq[hh:hh + 1, c0:] - ccol_ref[0, 0, c:c + key_chunk, hh:hh + 1]
                if diagonal:
                    causal = (lax.broadcasted_iota(jnp.int32, s.shape, 0)
                              <= lax.broadcasted_iota(jnp.int32, s.shape, 1))
                    s = jnp.where(causal, s, NEG_INF)
                m_old = m_run[:, c0:]
                m_new = jnp.maximum(m_old, jnp.max(s, 0, keepdims=True))
                alpha = jnp.exp(m_old - m_new)
                p = jnp.exp(s - m_new)
                l_new = alpha * l_run[:, c0:] + jnp.sum(p, 0, keepdims=True)
                vt = vt_ref[0, 0, rows, c:c + key_chunk].astype(BF16)
                a_new = alpha * acc[:, c0:] + jnp.dot(vt, p.astype(BF16), preferred_element_type=F32)
                m_run, l_run, acc = join(m_run, m_new, c0), join(l_run, l_new, c0), join(acc, a_new, c0)
            m_s[hh], l_s[hh] = m_run, l_run
            a_s[rows, :] = acc

    @pl.when(kj < qi)
    def _():
        attend(False)

    @pl.when(kj == qi)
    def _():
        attend(True)
        row = lax.broadcasted_iota(jnp.int32, (2 * hd, tile), 0)
        out_t = a_s[...] / jnp.where(row < hd, l_s[0], l_s[1])
        o_ref[...] = out_t.T

    @pl.when(step < n_groups)
    def _():
        @pl.when(step % seq_groups == 0)
        def _():
            init_seq()

        wait_group(slot)
        compute_group(slot)

        @pl.when(step % seq_groups == seq_groups - 1)
        def _():
            finish_seq()


def _fox_attention(y, qt, vt, c, q, k_new, v_new, lf_new, cache_kt, cache_vt, cache_lft, page_table, *,
                   k_col, layer, batch, seq, tile, group_pages):
    T = y.shape[0]
    DB, n_pages = page_table.shape
    pairs = A_HEADS // 2
    nt = seq // tile
    kb0 = k_col // LANES
    crow = c
    ccol = jnp.swapaxes(crow, 2, 3)
    tri = [(i, j) for i in range(nt) for j in range(i + 1)]
    qi_tab = jnp.asarray([i for i, _ in tri], jnp.int32)
    kj_tab = jnp.asarray([j for _, j in tri], jnp.int32)
    n_tri = len(tri)
    P = group_pages
    seq_groups = n_pages // P
    assert n_pages % P == 0 and DB * seq_groups <= batch * pairs * n_tri, "one page group per grid step"
    col = lambda a: a.reshape(DB, a.shape[1], 1)
    dec_seq = lambda b, p, t: jnp.minimum(((b * pairs + p) * n_tri + t) // seq_groups, DB - 1)
    vec = lambda n: pl.BlockSpec((1, n, 1), lambda b, p, t, qi, kj, pt: (dec_seq(b, p, t), 0, 0))
    hbm = pl.BlockSpec(memory_space=pl.ANY)
    page = (A_HEADS, A_HEAD_DIM, LANES)
    grid_spec = pltpu.PrefetchScalarGridSpec(
        num_scalar_prefetch=3, grid=(batch, pairs, n_tri),
        in_specs=[pl.BlockSpec((1, 1, LANES, tile), lambda b, p, t, qi, kj, pt: (layer, b, p, qi[t])),
                  pl.BlockSpec((tile, LANES), lambda b, p, t, qi, kj, pt: (b * nt + kj[t], kb0 + p)),
                  pl.BlockSpec((1, 1, LANES, tile), lambda b, p, t, qi, kj, pt: (layer, b, p, kj[t])),
                  pl.BlockSpec((1, 1, 2, tile), lambda b, p, t, qi, kj, pt: (b, p, 0, qi[t])),
                  pl.BlockSpec((1, 1, tile, 2), lambda b, p, t, qi, kj, pt: (b, p, kj[t], 0)),
                  vec(A_WIDTH), vec(A_WIDTH), vec(A_WIDTH), vec(A_HEADS), hbm, hbm, hbm],
        out_specs=[pl.BlockSpec((tile, LANES), lambda b, p, t, qi, kj, pt: (b * nt + qi[t], p)),
                   vec(A_WIDTH)],
        scratch_shapes=[pltpu.VMEM((2, LANES, tile), BF16), pltpu.VMEM((2, 1, tile), F32),
                        pltpu.VMEM((2, 1, tile), F32), pltpu.VMEM((LANES, tile), F32),
                        pltpu.VMEM((2, P) + page, F32), pltpu.VMEM((2, P) + page, F32),
                        pltpu.VMEM((2, P, A_HEADS, LANES), F32),
                        pltpu.SemaphoreType.DMA((3, 2)),
                        pltpu.VMEM((A_HEADS, 1), F32), pltpu.VMEM((A_HEADS, 1), F32),
                        pltpu.VMEM((A_HEADS, 1), F32),
                        pltpu.VMEM(page, F32), pltpu.VMEM(page, F32)])
    att, att_dec = pl.pallas_call(
        functools.partial(_fox_kernel, tile=tile, key_chunk=tile, layer=layer, group_pages=P),
        grid_spec=grid_spec,
        out_shape=[jax.ShapeDtypeStruct((T, A_WIDTH), F32), jax.ShapeDtypeStruct((DB, A_WIDTH, 1), F32)],
        compiler_params=_cparams("arbitrary", "arbitrary", "arbitrary"),
        name="fox_attention",
    )(qi_tab, kj_tab, page_table, qt, y, vt, crow, ccol, col(q), col(k_new), col(v_new), col(lf_new),
      cache_kt, cache_vt, cache_lft)
    return att, att_dec.reshape(DB, A_WIDTH)


def _paged_decode_ops(pt_ref, q_ref, kn_ref, vn_ref, lfn_ref, kc_hbm, vc_hbm, lfc_hbm, o_ref,
                      kbuf, vbuf, lfbuf, sem, m_s, l_s, c_s, a_s, q_s, *, layer, group_pages):
    P = group_pages
    n_pages = pt_ref.shape[1]
    H, hd = A_HEADS, A_HEAD_DIM
    hsl = [slice(h * hd, (h + 1) * hd) for h in range(H)]

    def group_copies(slot, page_of):
        out = []
        for i in range(P):
            pid = page_of(i)
            out.append(pltpu.make_async_copy(kc_hbm.at[layer, pid], kbuf.at[slot, i], sem.at[0, slot]))
            out.append(pltpu.make_async_copy(vc_hbm.at[layer, pid], vbuf.at[slot, i], sem.at[1, slot]))
            out.append(pltpu.make_async_copy(lfc_hbm.at[layer, pid], lfbuf.at[slot, i], sem.at[2, slot]))
        return out

    def start_group(seq, g, slot):
        for cp in group_copies(slot, lambda i: pt_ref[seq, n_pages - 1 - (g * P + i)]):
            cp.start()

    def wait_group(slot):
        for cp in group_copies(slot, lambda i: 0):
            cp.wait()

    later = (lax.broadcasted_iota(jnp.int32, (LANES, LANES), 0)
             > lax.broadcasted_iota(jnp.int32, (LANES, LANES), 1)).astype(F32)
    lane = lax.broadcasted_iota(jnp.int32, (hd, LANES), 1)

    def init_seq():
        qcols = [q_ref[0, hsl[h], :] * np.float32(hd ** -0.5) for h in range(H)]
        for h in range(H):
            q_s[h] = jnp.broadcast_to(qcols[h], (hd, LANES))
        m_s[...] = jnp.concatenate(
            [jnp.sum(qcols[h] * kn_ref[0, hsl[h], :], axis=0, keepdims=True) for h in range(H)], axis=0)
        l_s[...] = jnp.ones(l_s.shape, F32)
        c_s[...] = lfn_ref[0]
        for h in range(H):
            a_s[h] = jnp.where(lane == 0, vn_ref[0, hsl[h], :], 0.0)

    def compute_group(slot):
        c = c_s[...]
        lf_all = jnp.concatenate([lfbuf[slot, i] for i in range(P)], axis=0)
        excl_all = jnp.dot(lf_all, later, precision=lax.Precision.HIGHEST, preferred_element_type=F32)
        tot_all = jnp.sum(lf_all, -1, keepdims=True)
        s_pages = []
        for i in range(P):
            rows = [jnp.sum(kbuf[slot, i, h] * q_s[h], axis=0, keepdims=True) for h in range(H)]
            s_pages.append(jnp.concatenate(rows, axis=0) + (c + excl_all[i * H:(i + 1) * H, :]))
            c = c + tot_all[i * H:(i + 1) * H, :]
        c_s[...] = c
        s_all = jnp.concatenate(s_pages, axis=1)
        m_old = m_s[...]
        m_new = jnp.maximum(m_old, jnp.max(s_all, -1, keepdims=True))
        alpha = jnp.exp(m_old - m_new)
        p_all = jnp.exp(s_all - m_new)
        l_s[...] = alpha * l_s[...] + jnp.sum(p_all, -1, keepdims=True)
        m_s[...] = m_new
        for h in range(H):
            acc = a_s[h] * alpha[h:h + 1, :]
            for i in range(P):
                acc = acc + p_all[h:h + 1, i * LANES:(i + 1) * LANES] * vbuf[slot, i, h]
            a_s[h] = acc

    def finish_seq():
        l = l_s[...]
        for h in range(H):
            o_ref[0, hsl[h], :] = jnp.sum(a_s[h], axis=1, keepdims=True) / l[h:h + 1, :]

    return start_group, wait_group, init_seq, compute_group, finish_seq


def _even_mix_kernel(att_ref, za_ref, u_ref, vb_ref, zb_ref, h_ref, lng_ref, lnb_ref, ws_ref, bs_ref,
                     wout_ref, *outs, decode):
    o_ref = outs[0]
    ya = att_ref[...] * _silu(za_ref[...])
    u = _gelu(u_ref[...])
    vf = _gelu(vb_ref[...])
    mu = jnp.mean(vf, -1, keepdims=True)
    var = jnp.mean((vf - mu) ** 2, -1, keepdims=True)
    vn = (vf - mu) * lax.rsqrt(var + EPS) * lng_ref[...] + lnb_ref[...]
    tm, bw = vn.shape
    if decode:
        outs[1][...] = vn
        mix = vn * ws_ref[...] + bs_ref[...]
    else:
        lane = lax.broadcasted_iota(jnp.int32, (B_CHUNK, LANES), 1)
        tri = (lax.broadcasted_iota(jnp.int32, (B_CHUNK, B_CHUNK), 0)
               >= lax.broadcasted_iota(jnp.int32, (B_CHUNK, B_CHUNK), 1))
        wtril = [jnp.where(tri, ws_ref[g], 0.0).astype(BF16) for g in range(B_GROUPS)]
        gpl = LANES // (bw // B_GROUPS)
        rows = []
        for c in range(0, tm, B_CHUNK):
            blocks = []
            for lb in range(bw // LANES):
                vblk = vn[c:c + B_CHUNK, lb * LANES:(lb + 1) * LANES].astype(BF16)
                y0 = jnp.dot(wtril[lb * gpl], vblk, preferred_element_type=F32)
                y1 = jnp.dot(wtril[lb * gpl + 1], vblk, preferred_element_type=F32)
                blocks.append(jnp.where(lane < LANES // gpl, y0, y1))
            rows.append(jnp.concatenate(blocks, axis=1) + bs_ref[...])
        mix = jnp.concatenate(rows, axis=0)
    yb = u * mix * _silu(zb_ref[...])
    aw = ya.shape[1]
    hn = h_ref[...] + _mm(ya, wout_ref[:aw, :]) + _mm(yb, wout_ref[aw:, :])
    o_ref[...] = hn


def _even_mix(att, y, h, ln_g, ln_b, ws, bs, wout, *, col0, tm, decode):
    T, D = h.shape
    aw = att.shape[1]
    const2 = lambda i: (0, 0)
    yblk = lambda k: pl.BlockSpec((tm, aw), lambda i: (i, col0 + k))
    ws_spec = (pl.BlockSpec(ws.shape, const2) if decode else pl.BlockSpec(ws.shape, lambda i: (0, 0, 0)))
    in_specs = [pl.BlockSpec((tm, aw), lambda i: (i, 0)), yblk(0), yblk(1), yblk(2), yblk(3),
                pl.BlockSpec((tm, D), lambda i: (i, 0)),
                pl.BlockSpec((1, aw), const2), pl.BlockSpec((1, aw), const2),
                ws_spec, pl.BlockSpec(bs.shape, const2), pl.BlockSpec(wout.shape, const2)]
    out_shape = [jax.ShapeDtypeStruct((T, D), F32)]
    out_specs = [pl.BlockSpec((tm, D), lambda i: (i, 0))]
    if decode:
        out_shape.append(jax.ShapeDtypeStruct((T, aw), F32))
        out_specs.append(pl.BlockSpec((tm, aw), lambda i: (i, 0)))
    return pl.pallas_call(
        functools.partial(_even_mix_kernel, decode=decode),
        grid=(T // tm,), in_specs=in_specs, out_specs=out_specs, out_shape=out_shape,
        compiler_params=_cparams("parallel"), name="even_mix",
    )(att, y, y, y, y, h, ln_g.reshape(1, aw), ln_b.reshape(1, aw), ws, bs, wout)


def _qkv_decode_kernel(xc_ref, prev_ref, cw_ref, cb_ref, wq_ref, wk_ref, wv_ref,
                       xconv_ref, q_ref, k_ref, v_ref, nb_ref):
    xc = xc_ref[...]
    hd = xc.shape[1] // C_HEADS
    acc = cb_ref[...] + cw_ref[CONV_W - 1:CONV_W, :] * xc
    for j in range(CONV_W - 1):
        acc = acc + cw_ref[j:j + 1, :] * prev_ref[j]
    for j in range(CONV_W - 2):
        nb_ref[j] = prev_ref[j + 1]
    nb_ref[CONV_W - 2] = xc
    xconv = _silu(acc)
    xconv_ref[...] = xconv
    for h in range(C_HEADS):
        sl = slice(h * hd, (h + 1) * hd)
        q_ref[:, sl] = _mm(xconv[:, sl], wq_ref[h]) * np.float32(hd ** -0.5)
        k_ref[:, sl] = _mm(xconv[:, sl], wk_ref[h])
        v_ref[:, sl] = _mm(xc[:, sl], wv_ref[h])


def _qkv_decode(y, prev, cw, cb, wq, wk, wv):
    T = y.shape[0]
    W = cw.shape[1]
    const2 = lambda i: (0, 0)
    const3 = lambda i: (0, 0, 0)
    row_blk = pl.BlockSpec((T, W), lambda i: (0, 0))
    row_out = jax.ShapeDtypeStruct((T, W), F32)
    return pl.pallas_call(
        _qkv_decode_kernel, grid=(1,),
        in_specs=[row_blk, pl.BlockSpec(prev.shape, const3), pl.BlockSpec(cw.shape, const2),
                  pl.BlockSpec((1, W), const2), pl.BlockSpec(wq.shape, const3), pl.BlockSpec(wk.shape, const3),
                  pl.BlockSpec(wv.shape, const3)],
        out_specs=[row_blk, row_blk, row_blk, row_blk, pl.BlockSpec(prev.shape, const3)],
        out_shape=[row_out, row_out, row_out, row_out, jax.ShapeDtypeStruct(prev.shape, F32)],
        compiler_params=_cparams("arbitrary"), name="conv_qkv_decode",
    )(y, prev, cw, cb.reshape(1, W), wq, wk, wv)


def _mlstm_decode_kernel(q_ref, k_ref, v_ref, g_ref, o_ref, z_ref, xconv_ref, ng_ref, skip_ref,
                         c0_ref, n0_ref, m0_ref, *rest, nb, n_alias):
    y_ref, c_out, n_out, m_out = rest[n_alias:]
    W = q_ref.shape[2]
    hd = W // C_HEADS
    first = lax.broadcasted_iota(jnp.int32, (8, hd), 0) == 0
    pad8 = lambda r: jnp.where(first, jnp.broadcast_to(r, (8, hd)), 0.0)
    for b in range(nb):
        ys = []
        for h in range(C_HEADS):
            sl = slice(h * hd, (h + 1) * hd)
            q, k, v = q_ref[b, :, sl], k_ref[b, :, sl], v_ref[b, :, sl]
            log_i, log_f = g_ref[b, h:h + 1, :], g_ref[b, C_HEADS + h:C_HEADS + h + 1, :]
            c0, n0, m0 = c0_ref[0, b, h], n0_ref[0, b, h:h + 1, :], m0_ref[0, b, h:h + 1, :]
            m_new = jnp.maximum(log_f + m0, log_i)
            a = jnp.exp(log_f + m0 - m_new)
            w = jnp.exp(log_i - m_new)
            sm = w * jnp.sum(q * k, -1, keepdims=True)
            num = sm * v + a * _mm(pad8(q), c0)[0:1, :]
            den = sm + a * jnp.sum(q * n0, -1, keepdims=True)
            hc = num / jnp.maximum(jnp.abs(den), jnp.exp(-m_new))
            ktv = lax.dot_general(pad8(k).astype(BF16), pad8(v).astype(BF16), (((0,), (0,)), ((), ())),
                                  preferred_element_type=F32)
            c_out[0, b, h] = a * c0 + w * ktv
            n_out[b, h:h + 1, :] = a * n0 + w * k
            m_out[b, h:h + 1, :] = m_new
            ys.append(_mlstm_gate(hc, o_ref[b, :, sl], z_ref[b, :, sl], xconv_ref[b, :, sl],
                                  ng_ref[:, sl], skip_ref[:, sl]))
        y_ref[b] = jnp.concatenate(ys, axis=1)


def _mlstm_decode(q, k, v, gates, y_in, xconv, ng, skip, state, c_stack, *, layer, n_layers, nb):
    batch, W = q.shape[0], q.shape[2]
    hd = W // C_HEADS
    const2 = lambda b: (0, 0)
    rowblk = lambda k: pl.BlockSpec((nb, 1, W), lambda b: (b, 0, k))
    c0, n0, m0 = state
    in_specs = [rowblk(0), rowblk(0), rowblk(0),
                pl.BlockSpec((nb, GATE_ROWS, 1), lambda b: (b, 0, 0)),
                rowblk(2), rowblk(1), rowblk(0),
                pl.BlockSpec((1, W), const2), pl.BlockSpec((1, W), const2),
                pl.BlockSpec((1, nb, C_HEADS, hd, hd), lambda b: (layer, b, 0, 0, 0)),
                pl.BlockSpec((1, nb, C_HEADS, hd), lambda b: (layer, b, 0, 0)),
                pl.BlockSpec((1, nb, C_HEADS, 1), lambda b: (layer, b, 0, 0))]
    args = [q, k, v, gates, y_in, y_in, xconv, ng.reshape(1, W), skip.reshape(1, W),
            c0, n0, m0.reshape(m0.shape + (1,))]
    aliases = {}
    if c_stack is not None:
        aliases[len(args)] = 1
        in_specs.append(pl.BlockSpec(memory_space=pl.ANY))
        args.append(c_stack)
    out_shape = [jax.ShapeDtypeStruct((batch, 1, W), F32),
                 jax.ShapeDtypeStruct((n_layers, batch, C_HEADS, hd, hd), F32),
                 jax.ShapeDtypeStruct((batch, C_HEADS, hd), F32),
                 jax.ShapeDtypeStruct((batch, C_HEADS, 1), F32)]
    out_specs = [pl.BlockSpec((nb, 1, W), lambda b: (b, 0, 0)),
                 pl.BlockSpec((1, nb, C_HEADS, hd, hd), lambda b: (layer, b, 0, 0, 0)),
                 pl.BlockSpec((nb, C_HEADS, hd), lambda b: (b, 0, 0)),
                 pl.BlockSpec((nb, C_HEADS, 1), lambda b: (b, 0, 0))]
    y, c_new, n_new, m_new = pl.pallas_call(
        functools.partial(_mlstm_decode_kernel, nb=nb, n_alias=len(aliases)),
        grid=(batch // nb,), in_specs=in_specs, out_specs=out_specs, out_shape=out_shape,
        input_output_aliases=aliases,
        compiler_params=_cparams("parallel"), name="mlstm_decode",
    )(*args)
    return y, c_new, n_new, m_new.reshape(batch, C_HEADS)


def _mlstm_head(qh, kth, vh, i_row, b_row, m_prev, caug, tri, one_col):
    L, hd = qh.shape
    g_row = i_row - b_row
    dm = jnp.where(tri, g_row, NEG_INF)
    mcol = jnp.maximum(m_prev, jnp.max(dm, -1, keepdims=True))
    wmat = jnp.exp(dm - mcol)
    a = jnp.exp(m_prev - mcol)
    sm = wmat * _mm(qh, kth)
    qc = _mm(qh, caug)
    num = _mm(sm, vh) + a * qc[:, :hd]
    den = jnp.sum(sm, -1, keepdims=True) + a * qc[:, hd:hd + 1]
    den = jnp.maximum(jnp.abs(den), jnp.exp(-(_row_to_col(b_row) + mcol)))
    m_last = mcol[L - 1:L, :]
    wl = jnp.exp(g_row - m_last)
    a_l = jnp.exp(m_prev - m_last)
    caug_new = a_l * caug + _mm(kth * wl, jnp.concatenate([vh, one_col], axis=1))
    return num / den, caug_new, b_row[:, L - 1:L] + m_last


def _mlstm_gate(hc, o, z, xconv, ng, skip):
    mu = jnp.mean(hc, -1, keepdims=True)
    var = jnp.mean((hc - mu) ** 2, -1, keepdims=True)
    hn = (hc - mu) * lax.rsqrt(var + EPS) * ng
    return (_sigmoid(o) * hn + skip * xconv) * _silu(z)


def _mlstm_prompt_kernel(xc_ref, halo_ref, z_ref, o_ref, g_ref, h_ref, cw_ref, cb_ref, wq_ref, wkt_ref, wv_ref,
                         ng_ref, skip_ref, wout_ref, *rest, nb, n_alias, final):
    if final:
        fg_ref = rest[0]
        rest = rest[1:]
    hout_ref, c_out, n_out, m_out, caug_s, m_s = rest[n_alias:]
    L, W = xc_ref.shape[1], xc_ref.shape[2]
    hd = W // C_HEADS
    ci = pl.program_id(1)

    @pl.when(ci == 0)
    def _():
        caug_s[...] = jnp.zeros(caug_s.shape, F32)
        m_s[...] = jnp.zeros(m_s.shape, F32)

    xcs, xconvs = [], []
    for b in range(nb):
        xc = xc_ref[b]
        halo = jnp.where(ci == 0, 0.0, halo_ref[b])
        xx = jnp.concatenate([halo, xc], axis=0)
        acc = cb_ref[...] + cw_ref[CONV_W - 1:CONV_W, :] * xc
        for k in range(1, CONV_W):
            acc = acc + cw_ref[CONV_W - 1 - k:CONV_W - k, :] * pltpu.roll(xx, k, 0)[8:]
        xcs.append(xc)
        xconvs.append(_silu(acc))
    xc_all = jnp.concatenate(xcs, axis=0)
    xconv_all = jnp.concatenate(xconvs, axis=0)
    tri = (lax.broadcasted_iota(jnp.int32, (L, L), 0) >= lax.broadcasted_iota(jnp.int32, (L, L), 1))
    one_col = (lax.broadcasted_iota(jnp.int32, (L, LANES), 1) == 0).astype(F32)
    ys = [[None] * C_HEADS for _ in range(nb)]
    for h in range(C_HEADS):
        sl = slice(h * hd, (h + 1) * hd)
        xh = xconv_all[:, sl]
        q_h = _mm(xh, wq_ref[h]) * np.float32(hd ** -0.5)
        kt_h = _mm_nt(wkt_ref[h], xh)
        v_h = _mm(xc_all[:, sl], wv_ref[h])
        for b in range(nb):
            r = slice(b * L, (b + 1) * L)
            gates = g_ref[b]
            hc, caug_new, m_new = _mlstm_head(
                q_h[r], kt_h[:, r], v_h[r], gates[h:h + 1, :], gates[C_HEADS + h:C_HEADS + h + 1, :],
                m_s[b, h][0:1, 0:1], caug_s[b, h], tri, one_col)
            caug_s[b, h] = caug_new
            m_s[b, h] = jnp.broadcast_to(m_new, (8, LANES))
            ys[b][h] = _mlstm_gate(hc, o_ref[b][:, sl], z_ref[b][:, sl], xconv_all[r, sl],
                                   ng_ref[:, sl], skip_ref[:, sl])
    y_all = jnp.concatenate([jnp.concatenate(ys[b], axis=1) for b in range(nb)], axis=0)
    hn = jnp.concatenate([h_ref[b] for b in range(nb)], axis=0) + _mm(y_all, wout_ref[...])
    if final:
        hn = hn * lax.rsqrt(jnp.mean(hn * hn, -1, keepdims=True) + EPS) * fg_ref[...]
    for b in range(nb):
        hout_ref[b] = hn[b * L:(b + 1) * L]

    @pl.when(ci == pl.num_programs(1) - 1)
    def _():
        for b in range(nb):
            for h in range(C_HEADS):
                c_out[0, b, h] = caug_s[b, h, :, :hd]
                n_out[b, h] = caug_s[b, h, :, hd:hd + 1]
                m_out[b, h:h + 1, :] = m_s[b, h][0:1, 0:1]


def _mlstm_prompt(y_in, gates, h, cw, cb, wq, wkt, wv, ng, skip, wout, final_g, c_stack, *, layer, n_layers, nb):
    batch, S = y_in.shape[0], y_in.shape[1]
    W, D = wout.shape
    hd = W // C_HEADS
    L = C_CHUNK
    const2 = lambda b, c: (0, 0)
    const3 = lambda b, c: (0, 0, 0)
    yblk = lambda k: pl.BlockSpec((nb, L, W), lambda b, c: (b, c, k))
    in_specs = [yblk(0),
                pl.BlockSpec((nb, 8, W), lambda b, c: (b, jnp.maximum(c * (L // 8) - 1, 0), 0)),
                yblk(1), yblk(2),
                pl.BlockSpec((nb, GATE_ROWS, L), lambda b, c: (b, 0, c)),
                pl.BlockSpec((nb, L, D), lambda b, c: (b, c, 0)),
                pl.BlockSpec(cw.shape, const2), pl.BlockSpec((1, W), const2),
                pl.BlockSpec(wq.shape, const3), pl.BlockSpec(wkt.shape, const3), pl.BlockSpec(wv.shape, const3),
                pl.BlockSpec((1, W), const2), pl.BlockSpec((1, W), const2), pl.BlockSpec(wout.shape, const2)]
    args = [y_in, y_in, y_in, y_in, gates, h, cw, cb.reshape(1, W), wq, wkt, wv,
            ng.reshape(1, W), skip.reshape(1, W), wout]
    if final_g is not None:
        in_specs.append(pl.BlockSpec((1, D), const2))
        args.append(final_g.reshape(1, D))
    aliases = {}
    if c_stack is not None:
        aliases[len(args)] = 1
        in_specs.append(pl.BlockSpec(memory_space=pl.ANY))
        args.append(c_stack)
    out_shape = [jax.ShapeDtypeStruct((batch, S, D), F32),
                 jax.ShapeDtypeStruct((n_layers, batch, C_HEADS, hd, hd), F32),
                 jax.ShapeDtypeStruct((batch, C_HEADS, hd, 1), F32),
                 jax.ShapeDtypeStruct((batch, C_HEADS, 1), F32)]
    out_specs = [pl.BlockSpec((nb, L, D), lambda b, c: (b, c, 0)),
                 pl.BlockSpec((1, nb, C_HEADS, hd, hd), lambda b, c: (layer, b, 0, 0, 0)),
                 pl.BlockSpec((nb, C_HEADS, hd, 1), lambda b, c: (b, 0, 0, 0)),
                 pl.BlockSpec((nb, C_HEADS, 1), lambda b, c: (b, 0, 0))]
    hout, c_new, n_new, m_new = pl.pallas_call(
        functools.partial(_mlstm_prompt_kernel, nb=nb, n_alias=len(aliases), final=final_g is not None),
        grid=(batch // nb, S // L), in_specs=in_specs, out_specs=out_specs, out_shape=out_shape,
        input_output_aliases=aliases,
        scratch_shapes=[pltpu.VMEM((nb, C_HEADS, hd, hd + LANES), F32), pltpu.VMEM((nb, C_HEADS, 8, LANES), F32)],
        compiler_params=_cparams("parallel", "arbitrary"), name="mlstm_layer",
    )(*args)
    return hout, c_new, n_new.reshape(batch, C_HEADS, hd), m_new.reshape(batch, C_HEADS)


def _out_proj_kernel(y_ref, w_ref, h_ref, *rest, final):
    hn = h_ref[...] + _mm(y_ref[...], w_ref[...])
    if final:
        fg_ref, o_ref = rest
        o_ref[...] = hn * lax.rsqrt(jnp.mean(hn * hn, -1, keepdims=True) + EPS) * fg_ref[...]
    else:
        rest[0][...] = hn


def _out_proj(y, w, h, final_g, *, tm):
    T, D = h.shape
    K = y.shape[1]
    const2 = lambda i: (0, 0)
    in_specs = [pl.BlockSpec((tm, K), lambda i: (i, 0)), pl.BlockSpec(w.shape, const2),
                pl.BlockSpec((tm, D), lambda i: (i, 0))]
    args = [y, w, h]
    if final_g is not None:
        in_specs.append(pl.BlockSpec((1, D), const2))
        args.append(final_g.reshape(1, D))
    return pl.pallas_call(
        functools.partial(_out_proj_kernel, final=final_g is not None),
        grid=(T // tm,), in_specs=in_specs, out_specs=pl.BlockSpec((tm, D), lambda i: (i, 0)),
        out_shape=jax.ShapeDtypeStruct((T, D), F32),
        compiler_params=_cparams("parallel"), name="out_proj",
    )(*args)


def kernel(x_prompt, x_sample, cache_k, cache_v, cache_logf, state_c, state_n, state_m, state_conv, page_table,
           norm_g, final_g, even_w_in, even_b_f, gmlp_ln_g, gmlp_ln_b, gmlp_w_s, gmlp_b_s, even_w_out,
           odd_w_in, odd_b_i, odd_b_f, conv_w, conv_b, mlstm_w_q, mlstm_w_k, mlstm_w_v, mlstm_norm_g,
           mlstm_skip, odd_w_out):
    B, S, D = x_prompt.shape
    DB = x_sample.shape[0]
    depth = norm_g.shape[0]
    AW = A_WIDTH
    BW = gmlp_ln_g.shape[1]
    CW = conv_w.shape[2]
    gdim = BW // B_GROUPS
    tm = 512
    assert AW == BW, "even-layer column blocks are addressed in units of one common width"

    hp = x_prompt.reshape(B * S, D)
    hs = x_sample.reshape(DB, D)
    cache_kt = jnp.transpose(cache_k, (0, 1, 3, 4, 2))
    cache_vt = jnp.transpose(cache_v, (0, 1, 3, 4, 2))
    cache_lft = jnp.transpose(cache_logf, (0, 1, 3, 2))

    lp, ks, vs, ls, chv = [], [], [], [], []
    np_, mp, bp, ns, ms, bs = [], [], [], [], [], []
    n_even, n_odd = (depth + 1) // 2, depth // 2
    assert depth % 2 == 0, "the final rmsnorm is fused into the last (mLSTM) layer's output projection"
    qkv_stack = cp_stack = cs_stack = None
    for l in range(depth):
        j = l // 2
        last = l == depth - 1
        if l % 2 == 0:
            w = even_w_in[j]
            o = 0
            parts = {}
            for name, n in (("q", AW), ("k", AW), ("v", AW), ("fg", A_HEADS), ("za", AW), ("u", BW), ("vb", BW), ("zb", BW)):
                parts[name] = w[:, o:o + n]
                o += n
            wnn_s = jnp.concatenate([parts[n] for n in ("za", "u", "vb", "zb", "k", "q", "v")], 1).astype(BF16)
            wqt = parts["q"].T.astype(BF16)
            wkt = parts["k"].T.astype(BF16)
            wvt = parts["v"].T.astype(BF16)
            wg = parts["fg"].T.astype(BF16)
            gb = even_b_f[j].reshape(GATE_ROWS, 1)
            wout = even_w_out[j].astype(BF16)
            bs_full = jnp.repeat(gmlp_b_s[j].T, gdim, axis=1)
            ws_dec = jnp.repeat(gmlp_w_s[j][:, 0, 0], gdim).reshape(1, BW)
            bs_dec = jnp.repeat(gmlp_b_s[j][:, 0], gdim).reshape(1, BW)

            ys, lf = _proj(hs, norm_g[l], wnn_s, [], wg, gb, n_lin=0, batch=1, seq=DB, tm=DB)
            lf_s = lf[0].T
            k_s, q_s, v_s = ys[:, 4 * AW:5 * AW], ys[:, 5 * AW:6 * AW], ys[:, 6 * AW:7 * AW]
            y, qt, kt_all, vt_all, lf = _proj(hp, norm_g[l], wnn_s, [wqt, wkt, wvt], wg, gb, n_lin=0, n_nn=5 * AW,
                                              batch=B, seq=S, tm=tm, layer=j, n_layers=n_even, stacked=qkv_stack)
            qkv_stack = [qt, kt_all, vt_all]
            c = _seq_cumsum(lf)

            att, att_s = _fox_attention(y, qt, vt_all, c, q_s, k_s, v_s, lf_s, cache_kt, cache_vt, cache_lft,
                                        page_table, k_col=4 * AW, layer=j, batch=B, seq=S, tile=512,
                                        group_pages=8)
            hp = _even_mix(att, y, hp, gmlp_ln_g[j], gmlp_ln_b[j], gmlp_w_s[j], bs_full, wout,
                           col0=0, tm=tm, decode=False)[0]
            lp.append(jnp.transpose(lf, (0, 2, 1)))

            hs, vn_s = _even_mix(att_s, ys, hs, gmlp_ln_g[j], gmlp_ln_b[j], ws_dec, bs_dec, wout,
                                 col0=0, tm=DB, decode=True)
            ks.append(k_s.reshape(DB, 1, A_HEADS, A_HEAD_DIM))
            vs.append(v_s.reshape(DB, 1, A_HEADS, A_HEAD_DIM))
            ls.append(lf_s.reshape(DB, 1, A_HEADS))
            chv.append(vn_s.reshape(DB, 1, BW))
        else:
            w = odd_w_in[j]
            wnn = w[:, :3 * CW].astype(BF16)
            wg = w[:, 3 * CW:].T.astype(BF16)
            gb = jnp.concatenate([odd_b_i[j], odd_b_f[j]]).reshape(GATE_ROWS, 1)
            wq = mlstm_w_q[j].astype(BF16)
            wk = mlstm_w_k[j].astype(BF16)
            wkT = jnp.swapaxes(mlstm_w_k[j], 1, 2).astype(BF16)
            wv = mlstm_w_v[j].astype(BF16)
            wout = odd_w_out[j].astype(BF16)
            fin = final_g if last else None

            y, g = _proj(hp, norm_g[l], wnn, [], wg, gb, n_lin=C_HEADS, batch=B, seq=S, tm=tm, cum_chunk=C_CHUNK)
            hout, cp_stack, n1, m1 = _mlstm_prompt(
                y.reshape(B, S, 3 * CW), g, hp.reshape(B, S, D), conv_w[j], conv_b[j], wq, wkT, wv,
                mlstm_norm_g[j], mlstm_skip[j], wout, fin, cp_stack, layer=j, n_layers=n_odd, nb=2)
            hp = hout.reshape(B * S, D)
            np_.append(n1); mp.append(m1)
            bp.append(y.reshape(B, S, 3 * CW)[:, S - (CONV_W - 1):, :CW])

            y, g = _proj(hs, norm_g[l], wnn, [], wg, gb, n_lin=C_HEADS, batch=1, seq=DB, tm=DB)
            buf = jnp.transpose(state_conv[j], (1, 0, 2))
            xconv, q, k, v, nbuf = _qkv_decode(y, buf, conv_w[j], conv_b[j], wq, wk, wv)
            g_s = jnp.transpose(g, (2, 1, 0))
            r3 = lambda a: a.reshape(DB, 1, a.shape[1])
            ym, cs_stack, n2, m2 = _mlstm_decode(r3(q), r3(k), r3(v), g_s, r3(y), r3(xconv), mlstm_norm_g[j],
                                                 mlstm_skip[j], (state_c, state_n, state_m), cs_stack,
                                                 layer=j, n_layers=n_odd, nb=4)
            hs = _out_proj(ym.reshape(DB, CW), wout, hs, fin, tm=DB)
            ns.append(n2); ms.append(m2)
            bs.append(jnp.transpose(nbuf, (1, 0, 2)))
    y_prompt = hp.reshape(B, S, D)
    y_sample = hs.reshape(DB, 1, D)
    to_bshd = lambda t: jnp.transpose(t.reshape(n_even, B, A_HEADS, A_HEAD_DIM, S), (0, 1, 4, 2, 3))
    return (y_prompt, y_sample,
            to_bshd(qkv_stack[1]), to_bshd(qkv_stack[2]), jnp.stack(lp),
            jnp.stack(ks), jnp.stack(vs), jnp.stack(ls), jnp.stack(chv),
            cp_stack, jnp.stack(np_), jnp.stack(mp), jnp.stack(bp),
            cs_stack, jnp.stack(ns), jnp.stack(ms), jnp.stack(bs))
```

```python
import functools

import jax
import jax.numpy as jnp
import numpy as np
from jax import lax
from jax.experimental import pallas as pl
from jax.experimental.pallas import tpu as pltpu

F32 = jnp.float32
BF16 = jnp.bfloat16
EPS = 1e-6
NEG_INF = float("-inf")
LOG2E = 1.4426950408889634

A_HEADS = 8
A_HEAD_DIM = 64
A_WIDTH = A_HEADS * A_HEAD_DIM
B_GROUPS = 8
B_CHUNK = 128
C_HEADS = 4
C_CHUNK = 128
CONV_W = 4
LANES = 128
GATE_ROWS = 8
VMEM_LIMIT = 56 * 1024 * 1024


def _cparams(*sem):
    return pltpu.CompilerParams(dimension_semantics=sem, vmem_limit_bytes=VMEM_LIMIT)


def _mm(a, b):
    return jnp.dot(a.astype(BF16), b.astype(BF16), preferred_element_type=F32)


def _mm_nt(a, b):
    return lax.dot_general(a.astype(BF16), b.astype(BF16), (((1,), (1,)), ((), ())),
                           preferred_element_type=F32)


def _log_sigmoid(x):
    return jnp.minimum(x, 0.0) - jnp.log1p(jnp.exp(-jnp.abs(x)))


def _sigmoid(x):
    return 1.0 / (1.0 + jnp.exp(-x))


def _silu(x):
    return x * _sigmoid(x)


def _gelu(x):
    return 0.5 * x * (1.0 + lax.erf(x * np.float32(np.sqrt(0.5))))


def _row_to_col(r):
    n = r.shape[1]
    eye = lax.broadcasted_iota(jnp.int32, (n, n), 0) == lax.broadcasted_iota(jnp.int32, (n, n), 1)
    return jnp.sum(jnp.where(eye, r, 0.0), axis=1, keepdims=True)


def _proj_kernel(x_ref, g_ref, wnn_ref, wg_ref, gb_ref, *rest, n_nt, n_alias, n_lin, col_chunk, cum_chunk):
    wnt_refs = rest[:n_nt]
    rest = rest[n_nt + n_alias:]
    y_ref = rest[0]
    yt_refs = rest[1:1 + n_nt]
    gt_ref = rest[1 + n_nt]
    x = x_ref[...]
    xn = x * lax.rsqrt(jnp.mean(x * x, -1, keepdims=True) + EPS) * g_ref[...]
    xb = xn.astype(BF16)
    n_nn = y_ref.shape[1]
    for c in range(0, n_nn, col_chunk):
        y_ref[:, c:c + col_chunk] = jnp.dot(xb, wnn_ref[:, c:c + col_chunk], preferred_element_type=F32)
    for w_ref, o_ref in zip(wnt_refs, yt_refs):
        o_ref[0, 0] = _mm_nt(w_ref[...], xb)
    gt = _mm_nt(wg_ref[...], xb) + gb_ref[...]
    row = lax.broadcasted_iota(jnp.int32, gt.shape, 0)
    gt = jnp.where(row >= n_lin, _log_sigmoid(gt), gt)
    if cum_chunk is None:
        gt_ref[0] = gt
    else:
        upto = (lax.broadcasted_iota(jnp.int32, (cum_chunk, cum_chunk), 0)
                <= lax.broadcasted_iota(jnp.int32, (cum_chunk, cum_chunk), 1)).astype(F32)
        rowc = row[:, :cum_chunk]
        for c in range(0, gt.shape[1], cum_chunk):
            blk = gt[:, c:c + cum_chunk]
            cs = jnp.dot(blk, upto, precision=lax.Precision.HIGHEST, preferred_element_type=F32)
            gt_ref[0, :, c:c + cum_chunk] = jnp.where(rowc >= n_lin, cs, blk)


def _proj(x, g, wnn, wnts, wg, gb, *, n_lin, batch, seq, tm, n_nn=None, cum_chunk=None, layer=0, n_layers=1,
          stacked=None):
    T, D = x.shape
    tps = seq // tm
    n_nn = wnn.shape[1] if n_nn is None else n_nn
    n_nt = len(wnts)
    const = lambda i: (0, 0)
    tok_t = lambda i: (i // tps, 0, i % tps)
    in_specs = [pl.BlockSpec((tm, D), lambda i: (i, 0)),
                pl.BlockSpec((1, D), const),
                pl.BlockSpec((D, n_nn), const),
                pl.BlockSpec((GATE_ROWS, D), const),
                pl.BlockSpec((GATE_ROWS, 1), const)]
    in_specs += [pl.BlockSpec(w.shape, const) for w in wnts]
    args = [x, g.reshape(1, D), wnn, wg, gb, *wnts]
    aliases = {}
    if stacked is not None:
        for k, buf in enumerate(stacked):
            aliases[len(args)] = 1 + k
            in_specs.append(pl.BlockSpec(memory_space=pl.ANY))
            args.append(buf)
    out_shape = [jax.ShapeDtypeStruct((T, n_nn), F32)]
    out_specs = [pl.BlockSpec((tm, n_nn), lambda i: (i, 0))]
    for w in wnts:
        out_shape.append(jax.ShapeDtypeStruct((n_layers, batch, w.shape[0], seq), F32))
        out_specs.append(pl.BlockSpec((1, 1, w.shape[0], tm), lambda i: (layer, i // tps, 0, i % tps)))
    out_shape.append(jax.ShapeDtypeStruct((batch, GATE_ROWS, seq), F32))
    out_specs.append(pl.BlockSpec((1, GATE_ROWS, tm), tok_t))
    return pl.pallas_call(
        functools.partial(_proj_kernel, n_nt=n_nt, n_alias=len(aliases), n_lin=n_lin, col_chunk=512,
                          cum_chunk=cum_chunk),
        grid=(T // tm,), in_specs=in_specs, out_specs=out_specs, out_shape=out_shape,
        input_output_aliases=aliases,
        compiler_params=_cparams("parallel"), name="norm_proj",
    )(*args)


def _cumsum_kernel(x_ref, row_ref, col_ref):
    S = x_ref.shape[2]
    upto = (lax.broadcasted_iota(jnp.int32, (LANES, LANES), 0)
            <= lax.broadcasted_iota(jnp.int32, (LANES, LANES), 1)).astype(F32)
    carry = jnp.zeros((GATE_ROWS, 1), F32)
    for c in range(0, S, LANES):
        inc = jnp.dot(x_ref[0, :, c:c + LANES], upto, precision=lax.Precision.HIGHEST,
                      preferred_element_type=F32) + carry
        carry = inc[:, LANES - 1:LANES]
        inc2 = inc * np.float32(LOG2E)
        inc2_t = jnp.concatenate([inc2, jnp.zeros((LANES - GATE_ROWS, LANES), F32)], axis=0).T
        for p in range(GATE_ROWS // 2):
            row_ref[0, p, :, c:c + LANES] = inc2[2 * p:2 * p + 2, :]
            col_ref[0, p, c:c + LANES, :] = inc2_t[:, 2 * p:2 * p + 2]


def _seq_cumsum(x):
    B, R, S = x.shape
    return pl.pallas_call(_cumsum_kernel, grid=(B,),
                          in_specs=[pl.BlockSpec((1, R, S), lambda b: (b, 0, 0))],
                          out_specs=[pl.BlockSpec((1, R // 2, 2, S), lambda b: (b, 0, 0, 0)),
                                     pl.BlockSpec((1, R // 2, S, 2), lambda b: (b, 0, 0, 0))],
                          out_shape=[jax.ShapeDtypeStruct((B, R // 2, 2, S), F32),
                                     jax.ShapeDtypeStruct((B, R // 2, S, 2), F32)],
                          compiler_params=_cparams("parallel"), name="logf_cumsum")(x)


def _fox_kernel(qi_ref, kj_ref, pt_ref, qt_ref, k_ref, vt_ref, crow_ref, ccol_ref,
                dq_ref, dkn_ref, dvn_ref, dlf_ref, kc_hbm, vc_hbm, lfc_hbm, o_ref, do_ref,
                q_s, m_s, l_s, a_s, *decode_scratch, tile, layer, group_pages):
    qi = qi_ref[pl.program_id(2)]
    kj = kj_ref[pl.program_id(2)]
    hd = A_HEAD_DIM
    step = (pl.program_id(0) * pl.num_programs(1) + pl.program_id(1)) * pl.num_programs(2) + pl.program_id(2)
    seq_groups = pt_ref.shape[1] // group_pages
    n_groups = pt_ref.shape[0] * seq_groups
    start_group, wait_group, init_seq, compute_group, finish_seq = _paged_decode_ops(
        pt_ref, dq_ref, dkn_ref, dvn_ref, dlf_ref, kc_hbm, vc_hbm, lfc_hbm, do_ref, *decode_scratch,
        layer=layer, group_pages=group_pages)
    slot = step % 2

    @pl.when(step == 0)
    def _():
        start_group(0, 0, 0)

    @pl.when(step + 1 < n_groups)
    def _():
        start_group((step + 1) // seq_groups, (step + 1) % seq_groups, 1 - slot)

    @pl.when(kj == 0)
    def _():
        qt = qt_ref[0, 0] * np.float32(hd ** -0.5 * LOG2E)
        row = lax.broadcasted_iota(jnp.int32, qt.shape, 0)
        q_s[0] = jnp.where(row < hd, qt, 0.0).astype(BF16)
        q_s[1] = jnp.where(row >= hd, qt, 0.0).astype(BF16)
        m_s[...] = jnp.full(m_s.shape, NEG_INF, F32)
        l_s[...] = jnp.zeros(l_s.shape, F32)
        a_s[...] = jnp.zeros(a_s.shape, F32)

    def attend(diagonal):
        kb = k_ref[...].astype(BF16)
        cq = crow_ref[0, 0]
        ck = ccol_ref[0, 0]
        ones = jnp.ones((16, tile), BF16)
        if diagonal:
            causal = (lax.broadcasted_iota(jnp.int32, (tile, tile), 0)
                      <= lax.broadcasted_iota(jnp.int32, (tile, tile), 1))
        for hh in range(2):
            rows = slice(hh * hd, (hh + 1) * hd)
            s = jnp.dot(kb, q_s[hh], preferred_element_type=F32)
            s = s + cq[hh:hh + 1, :] - ck[:, hh:hh + 1]
            if diagonal:
                s = jnp.where(causal, s, NEG_INF)
            m_old = m_s[hh]
            m_new = jnp.maximum(m_old, jnp.max(s, 0, keepdims=True))
            alpha = jnp.exp2(m_old - m_new)
            p = jnp.exp2(s - m_new).astype(BF16)
            vt = jnp.concatenate([vt_ref[0, 0, rows, :].astype(BF16), ones], axis=0)
            pv = jnp.dot(vt, p, preferred_element_type=F32)
            a_s[rows, :] = alpha * a_s[rows, :] + pv[:hd]
            l_s[hh] = alpha * l_s[hh] + pv[hd:hd + 1]
            m_s[hh] = m_new

    def prompt_tile(diagonal):
        attend(diagonal)
        if diagonal:
            row = lax.broadcasted_iota(jnp.int32, (2 * hd, tile), 0)
            out_t = a_s[...] / jnp.where(row < hd, l_s[0], l_s[1])
            o_ref[...] = out_t.T

    has_group = step < n_groups

    @pl.when(has_group & (step % seq_groups == 0))
    def _():
        init_seq()

    for diagonal in (False, True):
        on_tile = (kj == qi) if diagonal else (kj < qi)

        @pl.when(on_tile & has_group)
        def _():
            wait_group(slot)
            prompt_tile(diagonal)
            compute_group(slot)

        @pl.when(on_tile & jnp.logical_not(has_group))
        def _():
            prompt_tile(diagonal)

    @pl.when(has_group & (step % seq_groups == seq_groups - 1))
    def _():
        finish_seq()


def _fox_attention(y, qt, vt, c, q, k_new, v_new, lf_new, cache_kt, cache_vt, cache_lft, page_table, *,
                   k_col, layer, batch, seq, tile, group_pages):
    T = y.shape[0]
    DB, n_pages = page_table.shape
    pairs = A_HEADS // 2
    nt = seq // tile
    kb0 = k_col // LANES
    crow, ccol = c
    tri = [(i, j) for i in range(nt) for j in range(i + 1)]
    qi_tab = jnp.asarray([i for i, _ in tri], jnp.int32)
    kj_tab = jnp.asarray([j for _, j in tri], jnp.int32)
    n_tri = len(tri)
    P = group_pages
    seq_groups = n_pages // P
    assert n_pages % P == 0 and DB * seq_groups <= batch * pairs * n_tri, "one page group per grid step"
    col = lambda a: a.reshape(DB, a.shape[1], 1)
    dec_seq = lambda b, p, t: jnp.minimum(((b * pairs + p) * n_tri + t) // seq_groups, DB - 1)
    vec = lambda n: pl.BlockSpec((1, n, 1), lambda b, p, t, qi, kj, pt: (dec_seq(b, p, t), 0, 0))
    hbm = pl.BlockSpec(memory_space=pl.ANY)
    page = (A_HEADS, A_HEAD_DIM, LANES)
    grid_spec = pltpu.PrefetchScalarGridSpec(
        num_scalar_prefetch=3, grid=(batch, pairs, n_tri),
        in_specs=[pl.BlockSpec((1, 1, LANES, tile), lambda b, p, t, qi, kj, pt: (layer, b, p, qi[t])),
                  pl.BlockSpec((tile, LANES), lambda b, p, t, qi, kj, pt: (b * nt + kj[t], kb0 + p)),
                  pl.BlockSpec((1, 1, LANES, tile), lambda b, p, t, qi, kj, pt: (layer, b, p, kj[t])),
                  pl.BlockSpec((1, 1, 2, tile), lambda b, p, t, qi, kj, pt: (b, p, 0, qi[t])),
                  pl.BlockSpec((1, 1, tile, 2), lambda b, p, t, qi, kj, pt: (b, p, kj[t], 0)),
                  vec(A_WIDTH), vec(A_WIDTH), vec(A_WIDTH), vec(A_HEADS), hbm, hbm, hbm],
        out_specs=[pl.BlockSpec((tile, LANES), lambda b, p, t, qi, kj, pt: (b * nt + qi[t], p)),
                   vec(A_WIDTH)],
        scratch_shapes=[pltpu.VMEM((2, LANES, tile), BF16), pltpu.VMEM((2, 1, tile), F32),
                        pltpu.VMEM((2, 1, tile), F32), pltpu.VMEM((LANES, tile), F32),
                        pltpu.VMEM((2, P) + page, F32), pltpu.VMEM((2, P) + page, F32),
                        pltpu.VMEM((2, P, A_HEADS, LANES), F32),
                        pltpu.SemaphoreType.DMA((3, 2)),
                        pltpu.VMEM((A_HEADS, 1), F32), pltpu.VMEM((A_HEADS, 1), F32),
                        pltpu.VMEM((A_HEADS, 1), F32),
                        pltpu.VMEM(page, F32), pltpu.VMEM(page, F32)])
    att, att_dec = pl.pallas_call(
        functools.partial(_fox_kernel, tile=tile, layer=layer, group_pages=P),
        grid_spec=grid_spec,
        out_shape=[jax.ShapeDtypeStruct((T, A_WIDTH), F32), jax.ShapeDtypeStruct((DB, A_WIDTH, 1), F32)],
        compiler_params=_cparams("arbitrary", "arbitrary", "arbitrary"),
        name="fox_attention",
    )(qi_tab, kj_tab, page_table, qt, y, vt, crow, ccol, col(q), col(k_new), col(v_new), col(lf_new),
      cache_kt, cache_vt, cache_lft)
    return att, att_dec.reshape(DB, A_WIDTH)


def _paged_decode_ops(pt_ref, q_ref, kn_ref, vn_ref, lfn_ref, kc_hbm, vc_hbm, lfc_hbm, o_ref,
                      kbuf, vbuf, lfbuf, sem, m_s, l_s, c_s, a_s, q_s, *, layer, group_pages):
    P = group_pages
    n_pages = pt_ref.shape[1]
    H, hd = A_HEADS, A_HEAD_DIM
    hsl = [slice(h * hd, (h + 1) * hd) for h in range(H)]

    def group_copies(slot, page_of):
        out = []
        for i in range(P):
            pid = page_of(i)
            out.append(pltpu.make_async_copy(kc_hbm.at[layer, pid], kbuf.at[slot, i], sem.at[0, slot]))
            out.append(pltpu.make_async_copy(vc_hbm.at[layer, pid], vbuf.at[slot, i], sem.at[1, slot]))
            out.append(pltpu.make_async_copy(lfc_hbm.at[layer, pid], lfbuf.at[slot, i], sem.at[2, slot]))
        return out

    def start_group(seq, g, slot):
        for cp in group_copies(slot, lambda i: pt_ref[seq, n_pages - 1 - (g * P + i)]):
            cp.start()

    def wait_group(slot):
        for cp in group_copies(slot, lambda i: 0):
            cp.wait()

    later = (lax.broadcasted_iota(jnp.int32, (LANES, LANES), 0)
             > lax.broadcasted_iota(jnp.int32, (LANES, LANES), 1)).astype(F32)
    lane = lax.broadcasted_iota(jnp.int32, (hd, LANES), 1)

    def init_seq():
        qcols = [q_ref[0, hsl[h], :] * np.float32(hd ** -0.5) for h in range(H)]
        for h in range(H):
            q_s[h] = jnp.broadcast_to(qcols[h], (hd, LANES))
        m_s[...] = jnp.concatenate(
            [jnp.sum(qcols[h] * kn_ref[0, hsl[h], :], axis=0, keepdims=True) for h in range(H)], axis=0)
        l_s[...] = jnp.ones(l_s.shape, F32)
        c_s[...] = lfn_ref[0]
        for h in range(H):
            a_s[h] = jnp.where(lane == 0, vn_ref[0, hsl[h], :], 0.0)

    def compute_group(slot):
        c = c_s[...]
        lf_all = jnp.concatenate([lfbuf[slot, i] for i in range(P)], axis=0)
        excl_all = jnp.dot(lf_all, later, precision=lax.Precision.HIGHEST, preferred_element_type=F32)
        tot_all = jnp.sum(lf_all, -1, keepdims=True)
        s_pages = []
        for i in range(P):
            rows = [jnp.sum(kbuf[slot, i, h] * q_s[h], axis=0, keepdims=True) for h in range(H)]
            s_pages.append(jnp.concatenate(rows, axis=0) + (c + excl_all[i * H:(i + 1) * H, :]))
            c = c + tot_all[i * H:(i + 1) * H, :]
        c_s[...] = c
        s_all = jnp.concatenate(s_pages, axis=1)
        m_old = m_s[...]
        m_new = jnp.maximum(m_old, jnp.max(s_all, -1, keepdims=True))
        alpha = jnp.exp(m_old - m_new)
        p_all = jnp.exp(s_all - m_new)
        l_s[...] = alpha * l_s[...] + jnp.sum(p_all, -1, keepdims=True)
        m_s[...] = m_new
        for h in range(H):
            acc = a_s[h] * alpha[h:h + 1, :]
            for i in range(P):
                acc = acc + p_all[h:h + 1, i * LANES:(i + 1) * LANES] * vbuf[slot, i, h]
            a_s[h] = acc

    def finish_seq():
        l = l_s[...]
        for h in range(H):
            o_ref[0, hsl[h], :] = jnp.sum(a_s[h], axis=1, keepdims=True) / l[h:h + 1, :]

    return start_group, wait_group, init_seq, compute_group, finish_seq


def _even_mix_kernel(att_ref, za_ref, u_ref, vb_ref, zb_ref, h_ref, lng_ref, lnb_ref, ws_ref, bs_ref,
                     wout_ref, *outs, decode):
    o_ref = outs[0]
    ya = att_ref[...] * _silu(za_ref[...])
    u = _gelu(u_ref[...])
    vf = _gelu(vb_ref[...])
    mu = jnp.mean(vf, -1, keepdims=True)
    var = jnp.mean((vf - mu) ** 2, -1, keepdims=True)
    vn = (vf - mu) * lax.rsqrt(var + EPS) * lng_ref[...] + lnb_ref[...]
    tm, bw = vn.shape
    if decode:
        outs[1][...] = vn
        mix = vn * ws_ref[...] + bs_ref[...]
    else:
        lane = lax.broadcasted_iota(jnp.int32, (B_CHUNK, LANES), 1)
        tri = (lax.broadcasted_iota(jnp.int32, (B_CHUNK, B_CHUNK), 0)
               >= lax.broadcasted_iota(jnp.int32, (B_CHUNK, B_CHUNK), 1))
        wtril = [jnp.where(tri, ws_ref[g], 0.0).astype(BF16) for g in range(B_GROUPS)]
        gpl = LANES // (bw // B_GROUPS)
        rows = []
        for c in range(0, tm, B_CHUNK):
            blocks = []
            for lb in range(bw // LANES):
                vblk = vn[c:c + B_CHUNK, lb * LANES:(lb + 1) * LANES].astype(BF16)
                y0 = jnp.dot(wtril[lb * gpl], vblk, preferred_element_type=F32)
                y1 = jnp.dot(wtril[lb * gpl + 1], vblk, preferred_element_type=F32)
                blocks.append(jnp.where(lane < LANES // gpl, y0, y1))
            rows.append(jnp.concatenate(blocks, axis=1) + bs_ref[...])
        mix = jnp.concatenate(rows, axis=0)
    yb = u * mix * _silu(zb_ref[...])
    aw = ya.shape[1]
    hn = h_ref[...] + _mm(ya, wout_ref[:aw, :]) + _mm(yb, wout_ref[aw:, :])
    o_ref[...] = hn


def _even_mix(att, y, h, ln_g, ln_b, ws, bs, wout, *, col0, tm, decode):
    T, D = h.shape
    aw = att.shape[1]
    const2 = lambda i: (0, 0)
    yblk = lambda k: pl.BlockSpec((tm, aw), lambda i: (i, col0 + k))
    ws_spec = (pl.BlockSpec(ws.shape, const2) if decode else pl.BlockSpec(ws.shape, lambda i: (0, 0, 0)))
    in_specs = [pl.BlockSpec((tm, aw), lambda i: (i, 0)), yblk(0), yblk(1), yblk(2), yblk(3),
                pl.BlockSpec((tm, D), lambda i: (i, 0)),
                pl.BlockSpec((1, aw), const2), pl.BlockSpec((1, aw), const2),
                ws_spec, pl.BlockSpec(bs.shape, const2), pl.BlockSpec(wout.shape, const2)]
    out_shape = [jax.ShapeDtypeStruct((T, D), F32)]
    out_specs = [pl.BlockSpec((tm, D), lambda i: (i, 0))]
    if decode:
        out_shape.append(jax.ShapeDtypeStruct((T, aw), F32))
        out_specs.append(pl.BlockSpec((tm, aw), lambda i: (i, 0)))
    return pl.pallas_call(
        functools.partial(_even_mix_kernel, decode=decode),
        grid=(T // tm,), in_specs=in_specs, out_specs=out_specs, out_shape=out_shape,
        compiler_params=_cparams("parallel"), name="even_mix",
    )(att, y, y, y, y, h, ln_g.reshape(1, aw), ln_b.reshape(1, aw), ws, bs, wout)


def _qkv_decode_kernel(xc_ref, prev_ref, cw_ref, cb_ref, wq_ref, wk_ref, wv_ref,
                       xconv_ref, q_ref, k_ref, v_ref, nb_ref):
    xc = xc_ref[...]
    hd = xc.shape[1] // C_HEADS
    acc = cb_ref[...] + cw_ref[CONV_W - 1:CONV_W, :] * xc
    for j in range(CONV_W - 1):
        acc = acc + cw_ref[j:j + 1, :] * prev_ref[j]
    for j in range(CONV_W - 2):
        nb_ref[j] = prev_ref[j + 1]
    nb_ref[CONV_W - 2] = xc
    xconv = _silu(acc)
    xconv_ref[...] = xconv
    for h in range(C_HEADS):
        sl = slice(h * hd, (h + 1) * hd)
        q_ref[:, sl] = _mm(xconv[:, sl], wq_ref[h]) * np.float32(hd ** -0.5)
        k_ref[:, sl] = _mm(xconv[:, sl], wk_ref[h])
        v_ref[:, sl] = _mm(xc[:, sl], wv_ref[h])


def _qkv_decode(y, prev, cw, cb, wq, wk, wv):
    T = y.shape[0]
    W = cw.shape[1]
    const2 = lambda i: (0, 0)
    const3 = lambda i: (0, 0, 0)
    row_blk = pl.BlockSpec((T, W), lambda i: (0, 0))
    row_out = jax.ShapeDtypeStruct((T, W), F32)
    return pl.pallas_call(
        _qkv_decode_kernel, grid=(1,),
        in_specs=[row_blk, pl.BlockSpec(prev.shape, const3), pl.BlockSpec(cw.shape, const2),
                  pl.BlockSpec((1, W), const2), pl.BlockSpec(wq.shape, const3), pl.BlockSpec(wk.shape, const3),
                  pl.BlockSpec(wv.shape, const3)],
        out_specs=[row_blk, row_blk, row_blk, row_blk, pl.BlockSpec(prev.shape, const3)],
        out_shape=[row_out, row_out, row_out, row_out, jax.ShapeDtypeStruct(prev.shape, F32)],
        compiler_params=_cparams("arbitrary"), name="conv_qkv_decode",
    )(y, prev, cw, cb.reshape(1, W), wq, wk, wv)


def _mlstm_decode_kernel(q_ref, k_ref, v_ref, g_ref, o_ref, z_ref, xconv_ref, ng_ref, skip_ref,
                         c0_ref, n0_ref, m0_ref, *rest, nb, n_alias):
    y_ref, c_out, n_out, m_out = rest[n_alias:]
    W = q_ref.shape[2]
    hd = W // C_HEADS
    first = lax.broadcasted_iota(jnp.int32, (8, hd), 0) == 0
    pad8 = lambda r: jnp.where(first, jnp.broadcast_to(r, (8, hd)), 0.0)
    for b in range(nb):
        ys = []
        for h in range(C_HEADS):
            sl = slice(h * hd, (h + 1) * hd)
            q, k, v = q_ref[b, :, sl], k_ref[b, :, sl], v_ref[b, :, sl]
            log_i, log_f = g_ref[b, h:h + 1, :], g_ref[b, C_HEADS + h:C_HEADS + h + 1, :]
            c0, n0, m0 = c0_ref[0, b, h], n0_ref[0, b, h:h + 1, :], m0_ref[0, b, h:h + 1, :]
            m_new = jnp.maximum(log_f + m0, log_i)
            a = jnp.exp(log_f + m0 - m_new)
            w = jnp.exp(log_i - m_new)
            sm = w * jnp.sum(q * k, -1, keepdims=True)
            num = sm * v + a * _mm(pad8(q), c0)[0:1, :]
            den = sm + a * jnp.sum(q * n0, -1, keepdims=True)
            hc = num / jnp.maximum(jnp.abs(den), jnp.exp(-m_new))
            ktv = lax.dot_general(pad8(k).astype(BF16), pad8(v).astype(BF16), (((0,), (0,)), ((), ())),
                                  preferred_element_type=F32)
            c_out[0, b, h] = a * c0 + w * ktv
            n_out[b, h:h + 1, :] = a * n0 + w * k
            m_out[b, h:h + 1, :] = m_new
            ys.append(_mlstm_gate(hc, o_ref[b, :, sl], z_ref[b, :, sl], xconv_ref[b, :, sl],
                                  ng_ref[:, sl], skip_ref[:, sl]))
        y_ref[b] = jnp.concatenate(ys, axis=1)


def _mlstm_decode(q, k, v, gates, y_in, xconv, ng, skip, state, c_stack, *, layer, n_layers, nb):
    batch, W = q.shape[0], q.shape[2]
    hd = W // C_HEADS
    const2 = lambda b: (0, 0)
    rowblk = lambda k: pl.BlockSpec((nb, 1, W), lambda b: (b, 0, k))
    c0, n0, m0 = state
    in_specs = [rowblk(0), rowblk(0), rowblk(0),
                pl.BlockSpec((nb, GATE_ROWS, 1), lambda b: (b, 0, 0)),
                rowblk(2), rowblk(1), rowblk(0),
                pl.BlockSpec((1, W), const2), pl.BlockSpec((1, W), const2),
                pl.BlockSpec((1, nb, C_HEADS, hd, hd), lambda b: (layer, b, 0, 0, 0)),
                pl.BlockSpec((1, nb, C_HEADS, hd), lambda b: (layer, b, 0, 0)),
                pl.BlockSpec((1, nb, C_HEADS, 1), lambda b: (layer, b, 0, 0))]
    args = [q, k, v, gates, y_in, y_in, xconv, ng.reshape(1, W), skip.reshape(1, W),
            c0, n0, m0.reshape(m0.shape + (1,))]
    aliases = {}
    if c_stack is not None:
        aliases[len(args)] = 1
        in_specs.append(pl.BlockSpec(memory_space=pl.ANY))
        args.append(c_stack)
    out_shape = [jax.ShapeDtypeStruct((batch, 1, W), F32),
                 jax.ShapeDtypeStruct((n_layers, batch, C_HEADS, hd, hd), F32),
                 jax.ShapeDtypeStruct((batch, C_HEADS, hd), F32),
                 jax.ShapeDtypeStruct((batch, C_HEADS, 1), F32)]
    out_specs = [pl.BlockSpec((nb, 1, W), lambda b: (b, 0, 0)),
                 pl.BlockSpec((1, nb, C_HEADS, hd, hd), lambda b: (layer, b, 0, 0, 0)),
                 pl.BlockSpec((nb, C_HEADS, hd), lambda b: (b, 0, 0)),
                 pl.BlockSpec((nb, C_HEADS, 1), lambda b: (b, 0, 0))]
    y, c_new, n_new, m_new = pl.pallas_call(
        functools.partial(_mlstm_decode_kernel, nb=nb, n_alias=len(aliases)),
        grid=(batch // nb,), in_specs=in_specs, out_specs=out_specs, out_shape=out_shape,
        input_output_aliases=aliases,
        compiler_params=_cparams("parallel"), name="mlstm_decode",
    )(*args)
    return y, c_new, n_new, m_new.reshape(batch, C_HEADS)


def _mlstm_head(qh, kth, vh, i_row, b_row, m_prev, caug, tri, one_col):
    L, hd = qh.shape
    g_row = i_row - b_row
    dm = jnp.where(tri, g_row, NEG_INF)
    mcol = jnp.maximum(m_prev, jnp.max(dm, -1, keepdims=True))
    wmat = jnp.exp(dm - mcol)
    a = jnp.exp(m_prev - mcol)
    sm = wmat * _mm(qh, kth)
    qc = _mm(qh, caug)
    num = _mm(sm, vh) + a * qc[:, :hd]
    den = jnp.sum(sm, -1, keepdims=True) + a * qc[:, hd:hd + 1]
    den = jnp.maximum(jnp.abs(den), jnp.exp(-(_row_to_col(b_row) + mcol)))
    m_last = mcol[L - 1:L, :]
    wl = jnp.exp(g_row - m_last)
    a_l = jnp.exp(m_prev - m_last)
    caug_new = a_l * caug + _mm(kth * wl, jnp.concatenate([vh, one_col], axis=1))
    return num / den, caug_new, b_row[:, L - 1:L] + m_last


def _mlstm_gate(hc, o, z, xconv, ng, skip):
    mu = jnp.mean(hc, -1, keepdims=True)
    var = jnp.mean((hc - mu) ** 2, -1, keepdims=True)
    hn = (hc - mu) * lax.rsqrt(var + EPS) * ng
    return (_sigmoid(o) * hn + skip * xconv) * _silu(z)


def _mlstm_prompt_kernel(xc_ref, halo_ref, z_ref, o_ref, g_ref, h_ref, cw_ref, cb_ref, wq_ref, wkt_ref, wv_ref,
                         ng_ref, skip_ref, wout_ref, *rest, nb, n_alias, final):
    if final:
        fg_ref = rest[0]
        rest = rest[1:]
    hout_ref, c_out, n_out, m_out, caug_s, m_s = rest[n_alias:]
    L, W = xc_ref.shape[1], xc_ref.shape[2]
    hd = W // C_HEADS
    ci = pl.program_id(1)

    @pl.when(ci == 0)
    def _():
        caug_s[...] = jnp.zeros(caug_s.shape, F32)
        m_s[...] = jnp.zeros(m_s.shape, F32)

    xcs, xconvs = [], []
    for b in range(nb):
        xc = xc_ref[b]
        halo = jnp.where(ci == 0, 0.0, halo_ref[b])
        xx = jnp.concatenate([halo, xc], axis=0)
        acc = cb_ref[...] + cw_ref[CONV_W - 1:CONV_W, :] * xc
        for k in range(1, CONV_W):
            acc = acc + cw_ref[CONV_W - 1 - k:CONV_W - k, :] * pltpu.roll(xx, k, 0)[8:]
        xcs.append(xc)
        xconvs.append(_silu(acc))
    xc_all = jnp.concatenate(xcs, axis=0)
    xconv_all = jnp.concatenate(xconvs, axis=0)
    tri = (lax.broadcasted_iota(jnp.int32, (L, L), 0) >= lax.broadcasted_iota(jnp.int32, (L, L), 1))
    one_col = (lax.broadcasted_iota(jnp.int32, (L, LANES), 1) == 0).astype(F32)
    ys = [[None] * C_HEADS for _ in range(nb)]
    for h in range(C_HEADS):
        sl = slice(h * hd, (h + 1) * hd)
        xh = xconv_all[:, sl]
        q_h = _mm(xh, wq_ref[h]) * np.float32(hd ** -0.5)
        kt_h = _mm_nt(wkt_ref[h], xh)
        v_h = _mm(xc_all[:, sl], wv_ref[h])
        for b in range(nb):
            r = slice(b * L, (b + 1) * L)
            gates = g_ref[b]
            hc, caug_new, m_new = _mlstm_head(
                q_h[r], kt_h[:, r], v_h[r], gates[h:h + 1, :], gates[C_HEADS + h:C_HEADS + h + 1, :],
                m_s[b, h][0:1, 0:1], caug_s[b, h], tri, one_col)
            caug_s[b, h] = caug_new
            m_s[b, h] = jnp.broadcast_to(m_new, (8, LANES))
            ys[b][h] = _mlstm_gate(hc, o_ref[b][:, sl], z_ref[b][:, sl], xconv_all[r, sl],
                                   ng_ref[:, sl], skip_ref[:, sl])
    y_all = jnp.concatenate([jnp.concatenate(ys[b], axis=1) for b in range(nb)], axis=0)
    hn = jnp.concatenate([h_ref[b] for b in range(nb)], axis=0) + _mm(y_all, wout_ref[...])
    if final:
        hn = hn * lax.rsqrt(jnp.mean(hn * hn, -1, keepdims=True) + EPS) * fg_ref[...]
    for b in range(nb):
        hout_ref[b] = hn[b * L:(b + 1) * L]

    @pl.when(ci == pl.num_programs(1) - 1)
    def _():
        for b in range(nb):
            for h in range(C_HEADS):
                c_out[0, b, h] = caug_s[b, h, :, :hd]
                n_out[b, h] = caug_s[b, h, :, hd:hd + 1]
                m_out[b, h:h + 1, :] = m_s[b, h][0:1, 0:1]


def _mlstm_prompt(y_in, gates, h, cw, cb, wq, wkt, wv, ng, skip, wout, final_g, c_stack, *, layer, n_layers, nb):
    batch, S = y_in.shape[0], y_in.shape[1]
    W, D = wout.shape
    hd = W // C_HEADS
    L = C_CHUNK
    const2 = lambda b, c: (0, 0)
    const3 = lambda b, c: (0, 0, 0)
    yblk = lambda k: pl.BlockSpec((nb, L, W), lambda b, c: (b, c, k))
    in_specs = [yblk(0),
                pl.BlockSpec((nb, 8, W), lambda b, c: (b, jnp.maximum(c * (L // 8) - 1, 0), 0)),
                yblk(1), yblk(2),
                pl.BlockSpec((nb, GATE_ROWS, L), lambda b, c: (b, 0, c)),
                pl.BlockSpec((nb, L, D), lambda b, c: (b, c, 0)),
                pl.BlockSpec(cw.shape, const2), pl.BlockSpec((1, W), const2),
                pl.BlockSpec(wq.shape, const3), pl.BlockSpec(wkt.shape, const3), pl.BlockSpec(wv.shape, const3),
                pl.BlockSpec((1, W), const2), pl.BlockSpec((1, W), const2), pl.BlockSpec(wout.shape, const2)]
    args = [y_in, y_in, y_in, y_in, gates, h, cw, cb.reshape(1, W), wq, wkt, wv,
            ng.reshape(1, W), skip.reshape(1, W), wout]
    if final_g is not None:
        in_specs.append(pl.BlockSpec((1, D), const2))
        args.append(final_g.reshape(1, D))
    aliases = {}
    if c_stack is not None:
        aliases[len(args)] = 1
        in_specs.append(pl.BlockSpec(memory_space=pl.ANY))
        args.append(c_stack)
    out_shape = [jax.ShapeDtypeStruct((batch, S, D), F32),
                 jax.ShapeDtypeStruct((n_layers, batch, C_HEADS, hd, hd), F32),
                 jax.ShapeDtypeStruct((batch, C_HEADS, hd, 1), F32),
                 jax.ShapeDtypeStruct((batch, C_HEADS, 1), F32)]
    out_specs = [pl.BlockSpec((nb, L, D), lambda b, c: (b, c, 0)),
                 pl.BlockSpec((1, nb, C_HEADS, hd, hd), lambda b, c: (layer, b, 0, 0, 0)),
                 pl.BlockSpec((nb, C_HEADS, hd, 1), lambda b, c: (b, 0, 0, 0)),
                 pl.BlockSpec((nb, C_HEADS, 1), lambda b, c: (b, 0, 0))]
    hout, c_new, n_new, m_new = pl.pallas_call(
        functools.partial(_mlstm_prompt_kernel, nb=nb, n_alias=len(aliases), final=final_g is not None),
        grid=(batch // nb, S // L), in_specs=in_specs, out_specs=out_specs, out_shape=out_shape,
        input_output_aliases=aliases,
        scratch_shapes=[pltpu.VMEM((nb, C_HEADS, hd, hd + LANES), F32), pltpu.VMEM((nb, C_HEADS, 8, LANES), F32)],
        compiler_params=_cparams("parallel", "arbitrary"), name="mlstm_layer",
    )(*args)
    return hout, c_new, n_new.reshape(batch, C_HEADS, hd), m_new.reshape(batch, C_HEADS)


def _out_proj_kernel(y_ref, w_ref, h_ref, *rest, final):
    hn = h_ref[...] + _mm(y_ref[...], w_ref[...])
    if final:
        fg_ref, o_ref = rest
        o_ref[...] = hn * lax.rsqrt(jnp.mean(hn * hn, -1, keepdims=True) + EPS) * fg_ref[...]
    else:
        rest[0][...] = hn


def _out_proj(y, w, h, final_g, *, tm):
    T, D = h.shape
    K = y.shape[1]
    const2 = lambda i: (0, 0)
    in_specs = [pl.BlockSpec((tm, K), lambda i: (i, 0)), pl.BlockSpec(w.shape, const2),
                pl.BlockSpec((tm, D), lambda i: (i, 0))]
    args = [y, w, h]
    if final_g is not None:
        in_specs.append(pl.BlockSpec((1, D), const2))
        args.append(final_g.reshape(1, D))
    return pl.pallas_call(
        functools.partial(_out_proj_kernel, final=final_g is not None),
        grid=(T // tm,), in_specs=in_specs, out_specs=pl.BlockSpec((tm, D), lambda i: (i, 0)),
        out_shape=jax.ShapeDtypeStruct((T, D), F32),
        compiler_params=_cparams("parallel"), name="out_proj",
    )(*args)


def kernel(x_prompt, x_sample, cache_k, cache_v, cache_logf, state_c, state_n, state_m, state_conv, page_table,
           norm_g, final_g, even_w_in, even_b_f, gmlp_ln_g, gmlp_ln_b, gmlp_w_s, gmlp_b_s, even_w_out,
           odd_w_in, odd_b_i, odd_b_f, conv_w, conv_b, mlstm_w_q, mlstm_w_k, mlstm_w_v, mlstm_norm_g,
           mlstm_skip, odd_w_out):
    B, S, D = x_prompt.shape
    DB = x_sample.shape[0]
    depth = norm_g.shape[0]
    AW = A_WIDTH
    BW = gmlp_ln_g.shape[1]
    CW = conv_w.shape[2]
    gdim = BW // B_GROUPS
    tm = 512
    assert AW == BW, "even-layer column blocks are addressed in units of one common width"

    hp = x_prompt.reshape(B * S, D)
    hs = x_sample.reshape(DB, D)
    cache_kt = jnp.transpose(cache_k, (0, 1, 3, 4, 2))
    cache_vt = jnp.transpose(cache_v, (0, 1, 3, 4, 2))
    cache_lft = jnp.transpose(cache_logf, (0, 1, 3, 2))

    lp, ks, vs, ls, chv = [], [], [], [], []
    np_, mp, bp, ns, ms, bs = [], [], [], [], [], []
    n_even, n_odd = (depth + 1) // 2, depth // 2
    assert depth % 2 == 0, "the final rmsnorm is fused into the last (mLSTM) layer's output projection"
    qkv_stack = cp_stack = cs_stack = None
    for l in range(depth):
        j = l // 2
        last = l == depth - 1
        if l % 2 == 0:
            w = even_w_in[j]
            o = 0
            parts = {}
            for name, n in (("q", AW), ("k", AW), ("v", AW), ("fg", A_HEADS), ("za", AW), ("u", BW), ("vb", BW), ("zb", BW)):
                parts[name] = w[:, o:o + n]
                o += n
            wnn_s = jnp.concatenate([parts[n] for n in ("za", "u", "vb", "zb", "k", "q", "v")], 1).astype(BF16)
            wqt = parts["q"].T.astype(BF16)
            wkt = parts["k"].T.astype(BF16)
            wvt = parts["v"].T.astype(BF16)
            wg = parts["fg"].T.astype(BF16)
            gb = even_b_f[j].reshape(GATE_ROWS, 1)
            wout = even_w_out[j].astype(BF16)
            bs_full = jnp.repeat(gmlp_b_s[j].T, gdim, axis=1)
            ws_dec = jnp.repeat(gmlp_w_s[j][:, 0, 0], gdim).reshape(1, BW)
            bs_dec = jnp.repeat(gmlp_b_s[j][:, 0], gdim).reshape(1, BW)

            ys, lf = _proj(hs, norm_g[l], wnn_s, [], wg, gb, n_lin=0, batch=1, seq=DB, tm=DB)
            lf_s = lf[0].T
            k_s, q_s, v_s = ys[:, 4 * AW:5 * AW], ys[:, 5 * AW:6 * AW], ys[:, 6 * AW:7 * AW]
            y, qt, kt_all, vt_all, lf = _proj(hp, norm_g[l], wnn_s, [wqt, wkt, wvt], wg, gb, n_lin=0, n_nn=5 * AW,
                                              batch=B, seq=S, tm=tm, layer=j, n_layers=n_even, stacked=qkv_stack)
            qkv_stack = [qt, kt_all, vt_all]
            c = _seq_cumsum(lf)

            att, att_s = _fox_attention(y, qt, vt_all, c, q_s, k_s, v_s, lf_s, cache_kt, cache_vt, cache_lft,
                                        page_table, k_col=4 * AW, layer=j, batch=B, seq=S, tile=512,
                                        group_pages=8)
            hp = _even_mix(att, y, hp, gmlp_ln_g[j], gmlp_ln_b[j], gmlp_w_s[j], bs_full, wout,
                           col0=0, tm=tm, decode=False)[0]
            lp.append(jnp.transpose(lf, (0, 2, 1)))

            hs, vn_s = _even_mix(att_s, ys, hs, gmlp_ln_g[j], gmlp_ln_b[j], ws_dec, bs_dec, wout,
                                 col0=0, tm=DB, decode=True)
            ks.append(k_s.reshape(DB, 1, A_HEADS, A_HEAD_DIM))
            vs.append(v_s.reshape(DB, 1, A_HEADS, A_HEAD_DIM))
            ls.append(lf_s.reshape(DB, 1, A_HEADS))
            chv.append(vn_s.reshape(DB, 1, BW))
        else:
            w = odd_w_in[j]
            wnn = w[:, :3 * CW].astype(BF16)
            wg = w[:, 3 * CW:].T.astype(BF16)
            gb = jnp.concatenate([odd_b_i[j], odd_b_f[j]]).reshape(GATE_ROWS, 1)
            wq = mlstm_w_q[j].astype(BF16)
            wk = mlstm_w_k[j].astype(BF16)
            wkT = jnp.swapaxes(mlstm_w_k[j], 1, 2).astype(BF16)
            wv = mlstm_w_v[j].astype(BF16)
            wout = odd_w_out[j].astype(BF16)
            fin = final_g if last else None

            y, g = _proj(hp, norm_g[l], wnn, [], wg, gb, n_lin=C_HEADS, batch=B, seq=S, tm=tm, cum_chunk=C_CHUNK)
            hout, cp_stack, n1, m1 = _mlstm_prompt(
                y.reshape(B, S, 3 * CW), g, hp.reshape(B, S, D), conv_w[j], conv_b[j], wq, wkT, wv,
                mlstm_norm_g[j], mlstm_skip[j], wout, fin, cp_stack, layer=j, n_layers=n_odd, nb=2)
            hp = hout.reshape(B * S, D)
            np_.append(n1); mp.append(m1)
            bp.append(y.reshape(B, S, 3 * CW)[:, S - (CONV_W - 1):, :CW])

            y, g = _proj(hs, norm_g[l], wnn, [], wg, gb, n_lin=C_HEADS, batch=1, seq=DB, tm=DB)
            buf = jnp.transpose(state_conv[j], (1, 0, 2))
            xconv, q, k, v, nbuf = _qkv_decode(y, buf, conv_w[j], conv_b[j], wq, wk, wv)
            g_s = jnp.transpose(g, (2, 1, 0))
            r3 = lambda a: a.reshape(DB, 1, a.shape[1])
            ym, cs_stack, n2, m2 = _mlstm_decode(r3(q), r3(k), r3(v), g_s, r3(y), r3(xconv), mlstm_norm_g[j],
                                                 mlstm_skip[j], (state_c, state_n, state_m), cs_stack,
                                                 layer=j, n_layers=n_odd, nb=4)
            hs = _out_proj(ym.reshape(DB, CW), wout, hs, fin, tm=DB)
            ns.append(n2); ms.append(m2)
            bs.append(jnp.transpose(nbuf, (1, 0, 2)))
    y_prompt = hp.reshape(B, S, D)
    y_sample = hs.reshape(DB, 1, D)
    to_bshd = lambda t: jnp.transpose(t.reshape(n_even, B, A_HEADS, A_HEAD_DIM, S), (0, 1, 4, 2, 3))
    return (y_prompt, y_sample,
            to_bshd(qkv_stack[1]), to_bshd(qkv_stack[2]), jnp.stack(lp),
            jnp.stack(ks), jnp.stack(vs), jnp.stack(ls), jnp.stack(chv),
            cp_stack, jnp.stack(np_), jnp.stack(mp), jnp.stack(bp),
            cs_stack, jnp.stack(ns), jnp.stack(ms), jnp.stack(bs))
```

```python
import functools

import jax
import jax.numpy as jnp
import numpy as np
from jax import lax
from jax.experimental import pallas as pl
from jax.experimental.pallas import tpu as pltpu

F32 = jnp.float32
BF16 = jnp.bfloat16
EPS = 1e-6
NEG_INF = float("-inf")
LOG2E = 1.4426950408889634

A_HEADS = 8
A_HEAD_DIM = 64
A_WIDTH = A_HEADS * A_HEAD_DIM
B_GROUPS = 8
B_CHUNK = 128
C_HEADS = 4
C_CHUNK = 128
CONV_W = 4
LANES = 128
GATE_ROWS = 8
VMEM_LIMIT = 56 * 1024 * 1024


def _cparams(*sem):
    return pltpu.CompilerParams(dimension_semantics=sem, vmem_limit_bytes=VMEM_LIMIT)


def _mm(a, b):
    return jnp.dot(a.astype(BF16), b.astype(BF16), preferred_element_type=F32)


def _mm_nt(a, b):
    return lax.dot_general(a.astype(BF16), b.astype(BF16), (((1,), (1,)), ((), ())),
                           preferred_element_type=F32)


def _log_sigmoid(x):
    return jnp.minimum(x, 0.0) - jnp.log1p(jnp.exp(-jnp.abs(x)))


def _sigmoid(x):
    return 1.0 / (1.0 + jnp.exp(-x))


def _silu(x):
    return x * _sigmoid(x)


def _gelu(x):
    return 0.5 * x * (1.0 + lax.erf(x * np.float32(np.sqrt(0.5))))


def _row_to_col(r):
    n = r.shape[1]
    eye = lax.broadcasted_iota(jnp.int32, (n, n), 0) == lax.broadcasted_iota(jnp.int32, (n, n), 1)
    return jnp.sum(jnp.where(eye, r, 0.0), axis=1, keepdims=True)


def _proj_kernel(x_ref, g_ref, wnn_ref, wg_ref, gb_ref, *rest, n_nt, n_alias, n_lin, col_chunk, cum_chunk):
    wnt_refs = rest[:n_nt]
    rest = rest[n_nt + n_alias:]
    y_ref = rest[0]
    yt_refs = rest[1:1 + n_nt]
    gt_ref = rest[1 + n_nt]
    x = x_ref[...]
    xn = x * lax.rsqrt(jnp.mean(x * x, -1, keepdims=True) + EPS) * g_ref[...]
    xb = xn.astype(BF16)
    n_nn = y_ref.shape[1]
    for c in range(0, n_nn, col_chunk):
        y_ref[:, c:c + col_chunk] = jnp.dot(xb, wnn_ref[:, c:c + col_chunk], preferred_element_type=F32)
    for w_ref, o_ref in zip(wnt_refs, yt_refs):
        o_ref[0, 0] = _mm_nt(w_ref[...], xb)
    gt = _mm_nt(wg_ref[...], xb) + gb_ref[...]
    row = lax.broadcasted_iota(jnp.int32, gt.shape, 0)
    gt = jnp.where(row >= n_lin, _log_sigmoid(gt), gt)
    if cum_chunk is None:
        gt_ref[0] = gt
    else:
        upto = (lax.broadcasted_iota(jnp.int32, (cum_chunk, cum_chunk), 0)
                <= lax.broadcasted_iota(jnp.int32, (cum_chunk, cum_chunk), 1)).astype(F32)
        rowc = row[:, :cum_chunk]
        for c in range(0, gt.shape[1], cum_chunk):
            blk = gt[:, c:c + cum_chunk]
            cs = jnp.dot(blk, upto, precision=lax.Precision.HIGHEST, preferred_element_type=F32)
            gt_ref[0, :, c:c + cum_chunk] = jnp.where(rowc >= n_lin, cs, blk)


def _proj(x, g, wnn, wnts, wg, gb, *, n_lin, batch, seq, tm, n_nn=None, cum_chunk=None, layer=0, n_layers=1,
          stacked=None):
    T, D = x.shape
    tps = seq // tm
    n_nn = wnn.shape[1] if n_nn is None else n_nn
    n_nt = len(wnts)
    const = lambda i: (0, 0)
    tok_t = lambda i: (i // tps, 0, i % tps)
    in_specs = [pl.BlockSpec((tm, D), lambda i: (i, 0)),
                pl.BlockSpec((1, D), const),
                pl.BlockSpec((D, n_nn), const),
                pl.BlockSpec((GATE_ROWS, D), const),
                pl.BlockSpec((GATE_ROWS, 1), const)]
    in_specs += [pl.BlockSpec(w.shape, const) for w in wnts]
    args = [x, g.reshape(1, D), wnn, wg, gb, *wnts]
    aliases = {}
    if stacked is not None:
        for k, buf in enumerate(stacked):
            aliases[len(args)] = 1 + k
            in_specs.append(pl.BlockSpec(memory_space=pl.ANY))
            args.append(buf)
    out_shape = [jax.ShapeDtypeStruct((T, n_nn), F32)]
    out_specs = [pl.BlockSpec((tm, n_nn), lambda i: (i, 0))]
    for w in wnts:
        out_shape.append(jax.ShapeDtypeStruct((n_layers, batch, w.shape[0], seq), F32))
        out_specs.append(pl.BlockSpec((1, 1, w.shape[0], tm), lambda i: (layer, i // tps, 0, i % tps)))
    out_shape.append(jax.ShapeDtypeStruct((batch, GATE_ROWS, seq), F32))
    out_specs.append(pl.BlockSpec((1, GATE_ROWS, tm), tok_t))
    return pl.pallas_call(
        functools.partial(_proj_kernel, n_nt=n_nt, n_alias=len(aliases), n_lin=n_lin, col_chunk=512,
                          cum_chunk=cum_chunk),
        grid=(T // tm,), in_specs=in_specs, out_specs=out_specs, out_shape=out_shape,
        input_output_aliases=aliases,
        compiler_params=_cparams("parallel"), name="norm_proj",
    )(*args)


def _cumsum_kernel(x_ref, row_ref, col_ref):
    S = x_ref.shape[2]
    upto = (lax.broadcasted_iota(jnp.int32, (LANES, LANES), 0)
            <= lax.broadcasted_iota(jnp.int32, (LANES, LANES), 1)).astype(F32)
    carry = jnp.zeros((GATE_ROWS, 1), F32)
    for c in range(0, S, LANES):
        inc = jnp.dot(x_ref[0, :, c:c + LANES], upto, precision=lax.Precision.HIGHEST,
                      preferred_element_type=F32) + carry
        carry = inc[:, LANES - 1:LANES]
        inc2 = inc * np.float32(LOG2E)
        inc2_t = jnp.concatenate([inc2, jnp.zeros((LANES - GATE_ROWS, LANES), F32)], axis=0).T
        for p in range(GATE_ROWS // 2):
            row_ref[0, p, :, c:c + LANES] = inc2[2 * p:2 * p + 2, :]
            col_ref[0, p, c:c + LANES, :] = inc2_t[:, 2 * p:2 * p + 2]


def _seq_cumsum(x):
    B, R, S = x.shape
    return pl.pallas_call(_cumsum_kernel, grid=(B,),
                          in_specs=[pl.BlockSpec((1, R, S), lambda b: (b, 0, 0))],
                          out_specs=[pl.BlockSpec((1, R // 2, 2, S), lambda b: (b, 0, 0, 0)),
                                     pl.BlockSpec((1, R // 2, S, 2), lambda b: (b, 0, 0, 0))],
                          out_shape=[jax.ShapeDtypeStruct((B, R // 2, 2, S), F32),
                                     jax.ShapeDtypeStruct((B, R // 2, S, 2), F32)],
                          compiler_params=_cparams("parallel"), name="logf_cumsum")(x)


def _fox_kernel(qi_ref, kj_ref, pt_ref, qt_ref, k_ref, vt_ref, crow_ref, ccol_ref,
                dq_ref, dkn_ref, dvn_ref, dlf_ref, kc_hbm, vc_hbm, lfc_hbm, o_ref, do_ref,
                q_s, m_s, l_s, a_s, *decode_scratch, tile, layer, group_pages):
    qi = qi_ref[pl.program_id(2)]
    kj = kj_ref[pl.program_id(2)]
    hd = A_HEAD_DIM
    step = (pl.program_id(0) * pl.num_programs(1) + pl.program_id(1)) * pl.num_programs(2) + pl.program_id(2)
    seq_groups = pt_ref.shape[1] // group_pages
    n_groups = pt_ref.shape[0] * seq_groups
    start_group, wait_group, init_seq, compute_group, finish_seq = _paged_decode_ops(
        pt_ref, dq_ref, dkn_ref, dvn_ref, dlf_ref, kc_hbm, vc_hbm, lfc_hbm, do_ref, *decode_scratch,
        layer=layer, group_pages=group_pages)
    n_slots = decode_scratch[0].shape[0]
    ahead = n_slots - 1
    slot = step % n_slots

    @pl.when(step == 0)
    def _():
        for g0 in range(ahead):
            start_group(g0 // seq_groups, g0 % seq_groups, g0)

    @pl.when(step + ahead < n_groups)
    def _():
        start_group((step + ahead) // seq_groups, (step + ahead) % seq_groups, (step + ahead) % n_slots)

    @pl.when(kj == 0)
    def _():
        qt = qt_ref[0, 0] * np.float32(hd ** -0.5 * LOG2E)
        row = lax.broadcasted_iota(jnp.int32, qt.shape, 0)
        q_s[0] = jnp.where(row < hd, qt, 0.0).astype(BF16)
        q_s[1] = jnp.where(row >= hd, qt, 0.0).astype(BF16)
        m_s[...] = jnp.full(m_s.shape, NEG_INF, F32)
        l_s[...] = jnp.zeros(l_s.shape, F32)
        a_s[...] = jnp.zeros(a_s.shape, F32)

    def attend(diagonal):
        kb = k_ref[...].astype(BF16)
        cq = crow_ref[0, 0]
        ck = ccol_ref[0, 0]
        ones = jnp.ones((16, tile), BF16)
        if diagonal:
            causal = (lax.broadcasted_iota(jnp.int32, (tile, tile), 0)
                      <= lax.broadcasted_iota(jnp.int32, (tile, tile), 1))
        for hh in range(2):
            rows = slice(hh * hd, (hh + 1) * hd)
            s = jnp.dot(kb, q_s[hh], preferred_element_type=F32)
            s = s + cq[hh:hh + 1, :] - ck[:, hh:hh + 1]
            if diagonal:
                s = jnp.where(causal, s, NEG_INF)
            m_old = m_s[hh]
            m_new = jnp.maximum(m_old, jnp.max(s, 0, keepdims=True))
            alpha = jnp.exp2(m_old - m_new)
            p = jnp.exp2(s - m_new).astype(BF16)
            vt = jnp.concatenate([vt_ref[0, 0, rows, :].astype(BF16), ones], axis=0)
            pv = jnp.dot(vt, p, preferred_element_type=F32)
            a_s[rows, :] = alpha * a_s[rows, :] + pv[:hd]
            l_s[hh] = alpha * l_s[hh] + pv[hd:hd + 1]
            m_s[hh] = m_new

    def prompt_tile(diagonal):
        attend(diagonal)
        if diagonal:
            row = lax.broadcasted_iota(jnp.int32, (2 * hd, tile), 0)
            out_t = a_s[...] / jnp.where(row < hd, l_s[0], l_s[1])
            o_ref[...] = out_t.T

    has_group = step < n_groups

    @pl.when(has_group & (step % seq_groups == 0))
    def _():
        init_seq()

    for diagonal in (False, True):
        on_tile = (kj == qi) if diagonal else (kj < qi)

        @pl.when(on_tile & has_group)
        def _():
            wait_group(slot)
            prompt_tile(diagonal)
            compute_group(slot)

        @pl.when(on_tile & jnp.logical_not(has_group))
        def _():
            prompt_tile(diagonal)

    @pl.when(has_group & (step % seq_groups == seq_groups - 1))
    def _():
        finish_seq()


def _fox_attention(y, qt, vt, c, q, k_new, v_new, lf_new, cache_kt, cache_vt, cache_lft, page_table, *,
                   k_col, layer, batch, seq, tile, group_pages):
    T = y.shape[0]
    DB, n_pages = page_table.shape
    pairs = A_HEADS // 2
    nt = seq // tile
    kb0 = k_col // LANES
    crow, ccol = c
    tri = [(i, j) for i in range(nt) for j in range(i + 1)]
    qi_tab = jnp.asarray([i for i, _ in tri], jnp.int32)
    kj_tab = jnp.asarray([j for _, j in tri], jnp.int32)
    n_tri = len(tri)
    P = group_pages
    n_slots = 3
    seq_groups = n_pages // P
    assert n_pages % P == 0 and n_slots - 1 <= DB * seq_groups <= batch * pairs * n_tri, \
        "one page group per grid step"
    col = lambda a: a.reshape(DB, a.shape[1], 1)
    dec_seq = lambda b, p, t: jnp.minimum(((b * pairs + p) * n_tri + t) // seq_groups, DB - 1)
    vec = lambda n: pl.BlockSpec((1, n, 1), lambda b, p, t, qi, kj, pt: (dec_seq(b, p, t), 0, 0))
    hbm = pl.BlockSpec(memory_space=pl.ANY)
    page = (A_HEADS, A_HEAD_DIM, LANES)
    grid_spec = pltpu.PrefetchScalarGridSpec(
        num_scalar_prefetch=3, grid=(batch, pairs, n_tri),
        in_specs=[pl.BlockSpec((1, 1, LANES, tile), lambda b, p, t, qi, kj, pt: (layer, b, p, qi[t])),
                  pl.BlockSpec((tile, LANES), lambda b, p, t, qi, kj, pt: (b * nt + kj[t], kb0 + p)),
                  pl.BlockSpec((1, 1, LANES, tile), lambda b, p, t, qi, kj, pt: (layer, b, p, kj[t])),
                  pl.BlockSpec((1, 1, 2, tile), lambda b, p, t, qi, kj, pt: (b, p, 0, qi[t])),
                  pl.BlockSpec((1, 1, tile, 2), lambda b, p, t, qi, kj, pt: (b, p, kj[t], 0)),
                  vec(A_WIDTH), vec(A_WIDTH), vec(A_WIDTH), vec(A_HEADS), hbm, hbm, hbm],
        out_specs=[pl.BlockSpec((tile, LANES), lambda b, p, t, qi, kj, pt: (b * nt + qi[t], p)),
                   vec(A_WIDTH)],
        scratch_shapes=[pltpu.VMEM((2, LANES, tile), BF16), pltpu.VMEM((2, 1, tile), F32),
                        pltpu.VMEM((2, 1, tile), F32), pltpu.VMEM((LANES, tile), F32),
                        pltpu.VMEM((n_slots, P) + page, F32), pltpu.VMEM((n_slots, P) + page, F32),
                        pltpu.VMEM((n_slots, P, A_HEADS, LANES), F32),
                        pltpu.SemaphoreType.DMA((3, n_slots)),
                        pltpu.VMEM((A_HEADS, 1), F32), pltpu.VMEM((A_HEADS, 1), F32),
                        pltpu.VMEM((A_HEADS, 1), F32),
                        pltpu.VMEM(page, F32), pltpu.VMEM(page, F32)])
    att, att_dec = pl.pallas_call(
        functools.partial(_fox_kernel, tile=tile, layer=layer, group_pages=P),
        grid_spec=grid_spec,
        out_shape=[jax.ShapeDtypeStruct((T, A_WIDTH), F32), jax.ShapeDtypeStruct((DB, A_WIDTH, 1), F32)],
        compiler_params=_cparams("arbitrary", "arbitrary", "arbitrary"),
        name="fox_attention",
    )(qi_tab, kj_tab, page_table, qt, y, vt, crow, ccol, col(q), col(k_new), col(v_new), col(lf_new),
      cache_kt, cache_vt, cache_lft)
    return att, att_dec.reshape(DB, A_WIDTH)


def _paged_decode_ops(pt_ref, q_ref, kn_ref, vn_ref, lfn_ref, kc_hbm, vc_hbm, lfc_hbm, o_ref,
                      kbuf, vbuf, lfbuf, sem, m_s, l_s, c_s, a_s, q_s, *, layer, group_pages):
    P = group_pages
    n_pages = pt_ref.shape[1]
    H, hd = A_HEADS, A_HEAD_DIM
    hsl = [slice(h * hd, (h + 1) * hd) for h in range(H)]

    def group_copies(slot, page_of):
        out = []
        for i in range(P):
            pid = page_of(i)
            out.append(pltpu.make_async_copy(kc_hbm.at[layer, pid], kbuf.at[slot, i], sem.at[0, slot]))
            out.append(pltpu.make_async_copy(vc_hbm.at[layer, pid], vbuf.at[slot, i], sem.at[1, slot]))
            out.append(pltpu.make_async_copy(lfc_hbm.at[layer, pid], lfbuf.at[slot, i], sem.at[2, slot]))
        return out

    def start_group(seq, g, slot):
        for cp in group_copies(slot, lambda i: pt_ref[seq, n_pages - 1 - (g * P + i)]):
            cp.start()

    def wait_group(slot):
        for cp in group_copies(slot, lambda i: 0):
            cp.wait()

    later = (lax.broadcasted_iota(jnp.int32, (LANES, LANES), 0)
             > lax.broadcasted_iota(jnp.int32, (LANES, LANES), 1)).astype(F32)
    lane = lax.broadcasted_iota(jnp.int32, (hd, LANES), 1)

    def init_seq():
        qcols = [q_ref[0, hsl[h], :] * np.float32(hd ** -0.5) for h in range(H)]
        for h in range(H):
            q_s[h] = jnp.broadcast_to(qcols[h], (hd, LANES))
        m_s[...] = jnp.concatenate(
            [jnp.sum(qcols[h] * kn_ref[0, hsl[h], :], axis=0, keepdims=True) for h in range(H)], axis=0)
        l_s[...] = jnp.ones(l_s.shape, F32)
        c_s[...] = lfn_ref[0]
        for h in range(H):
            a_s[h] = jnp.where(lane == 0, vn_ref[0, hsl[h], :], 0.0)

    def compute_group(slot):
        c = c_s[...]
        lf_all = jnp.concatenate([lfbuf[slot, i] for i in range(P)], axis=0)
        excl_all = jnp.dot(lf_all, later, precision=lax.Precision.HIGHEST, preferred_element_type=F32)
        tot_all = jnp.sum(lf_all, -1, keepdims=True)
        s_pages = []
        for i in range(P):
            rows = [jnp.sum(kbuf[slot, i, h] * q_s[h], axis=0, keepdims=True) for h in range(H)]
            s_pages.append(jnp.concatenate(rows, axis=0) + (c + excl_all[i * H:(i + 1) * H, :]))
            c = c + tot_all[i * H:(i + 1) * H, :]
        c_s[...] = c
        s_all = jnp.concatenate(s_pages, axis=1)
        m_old = m_s[...]
        m_new = jnp.maximum(m_old, jnp.max(s_all, -1, keepdims=True))
        alpha = jnp.exp(m_old - m_new)
        p_all = jnp.exp(s_all - m_new)
        l_s[...] = alpha * l_s[...] + jnp.sum(p_all, -1, keepdims=True)
        m_s[...] = m_new
        for h in range(H):
            acc = a_s[h] * alpha[h:h + 1, :]
            for i in range(P):
                acc = acc + p_all[h:h + 1, i * LANES:(i + 1) * LANES] * vbuf[slot, i, h]
            a_s[h] = acc

    def finish_seq():
        l = l_s[...]
        for h in range(H):
            o_ref[0, hsl[h], :] = jnp.sum(a_s[h], axis=1, keepdims=True) / l[h:h + 1, :]

    return start_group, wait_group, init_seq, compute_group, finish_seq


def _even_mix_kernel(att_ref, za_ref, u_ref, vb_ref, zb_ref, h_ref, lng_ref, lnb_ref, ws_ref, bs_ref,
                     wout_ref, *outs, decode):
    o_ref = outs[0]
    ya = att_ref[...] * _silu(za_ref[...])
    u = _gelu(u_ref[...])
    vf = _gelu(vb_ref[...])
    mu = jnp.mean(vf, -1, keepdims=True)
    var = jnp.mean((vf - mu) ** 2, -1, keepdims=True)
    vn = (vf - mu) * lax.rsqrt(var + EPS) * lng_ref[...] + lnb_ref[...]
    tm, bw = vn.shape
    if decode:
        outs[1][...] = vn
        mix = vn * ws_ref[...] + bs_ref[...]
    else:
        lane = lax.broadcasted_iota(jnp.int32, (B_CHUNK, LANES), 1)
        tri = (lax.broadcasted_iota(jnp.int32, (B_CHUNK, B_CHUNK), 0)
               >= lax.broadcasted_iota(jnp.int32, (B_CHUNK, B_CHUNK), 1))
        wtril = [jnp.where(tri, ws_ref[g], 0.0).astype(BF16) for g in range(B_GROUPS)]
        gpl = LANES // (bw // B_GROUPS)
        rows = []
        for c in range(0, tm, B_CHUNK):
            blocks = []
            for lb in range(bw // LANES):
                vblk = vn[c:c + B_CHUNK, lb * LANES:(lb + 1) * LANES].astype(BF16)
                y0 = jnp.dot(wtril[lb * gpl], vblk, preferred_element_type=F32)
                y1 = jnp.dot(wtril[lb * gpl + 1], vblk, preferred_element_type=F32)
                blocks.append(jnp.where(lane < LANES // gpl, y0, y1))
            rows.append(jnp.concatenate(blocks, axis=1) + bs_ref[...])
        mix = jnp.concatenate(rows, axis=0)
    yb = u * mix * _silu(zb_ref[...])
    aw = ya.shape[1]
    hn = h_ref[...] + _mm(ya, wout_ref[:aw, :]) + _mm(yb, wout_ref[aw:, :])
    o_ref[...] = hn


def _even_mix(att, y, h, ln_g, ln_b, ws, bs, wout, *, col0, tm, decode):
    T, D = h.shape
    aw = att.shape[1]
    const2 = lambda i: (0, 0)
    yblk = lambda k: pl.BlockSpec((tm, aw), lambda i: (i, col0 + k))
    ws_spec = (pl.BlockSpec(ws.shape, const2) if decode else pl.BlockSpec(ws.shape, lambda i: (0, 0, 0)))
    in_specs = [pl.BlockSpec((tm, aw), lambda i: (i, 0)), yblk(0), yblk(1), yblk(2), yblk(3),
                pl.BlockSpec((tm, D), lambda i: (i, 0)),
                pl.BlockSpec((1, aw), const2), pl.BlockSpec((1, aw), const2),
                ws_spec, pl.BlockSpec(bs.shape, const2), pl.BlockSpec(wout.shape, const2)]
    out_shape = [jax.ShapeDtypeStruct((T, D), F32)]
    out_specs = [pl.BlockSpec((tm, D), lambda i: (i, 0))]
    if decode:
        out_shape.append(jax.ShapeDtypeStruct((T, aw), F32))
        out_specs.append(pl.BlockSpec((tm, aw), lambda i: (i, 0)))
    return pl.pallas_call(
        functools.partial(_even_mix_kernel, decode=decode),
        grid=(T // tm,), in_specs=in_specs, out_specs=out_specs, out_shape=out_shape,
        compiler_params=_cparams("parallel"), name="even_mix",
    )(att, y, y, y, y, h, ln_g.reshape(1, aw), ln_b.reshape(1, aw), ws, bs, wout)


def _qkv_decode_kernel(xc_ref, prev_ref, cw_ref, cb_ref, wq_ref, wk_ref, wv_ref,
                       xconv_ref, q_ref, k_ref, v_ref, nb_ref):
    xc = xc_ref[...]
    hd = xc.shape[1] // C_HEADS
    acc = cb_ref[...] + cw_ref[CONV_W - 1:CONV_W, :] * xc
    for j in range(CONV_W - 1):
        acc = acc + cw_ref[j:j + 1, :] * prev_ref[j]
    for j in range(CONV_W - 2):
        nb_ref[j] = prev_ref[j + 1]
    nb_ref[CONV_W - 2] = xc
    xconv = _silu(acc)
    xconv_ref[...] = xconv
    for h in range(C_HEADS):
        sl = slice(h * hd, (h + 1) * hd)
        q_ref[:, sl] = _mm(xconv[:, sl], wq_ref[h]) * np.float32(hd ** -0.5)
        k_ref[:, sl] = _mm(xconv[:, sl], wk_ref[h])
        v_ref[:, sl] = _mm(xc[:, sl], wv_ref[h])


def _qkv_decode(y, prev, cw, cb, wq, wk, wv):
    T = y.shape[0]
    W = cw.shape[1]
    const2 = lambda i: (0, 0)
    const3 = lambda i: (0, 0, 0)
    row_blk = pl.BlockSpec((T, W), lambda i: (0, 0))
    row_out = jax.ShapeDtypeStruct((T, W), F32)
    return pl.pallas_call(
        _qkv_decode_kernel, grid=(1,),
        in_specs=[row_blk, pl.BlockSpec(prev.shape, const3), pl.BlockSpec(cw.shape, const2),
                  pl.BlockSpec((1, W), const2), pl.BlockSpec(wq.shape, const3), pl.BlockSpec(wk.shape, const3),
                  pl.BlockSpec(wv.shape, const3)],
        out_specs=[row_blk, row_blk, row_blk, row_blk, pl.BlockSpec(prev.shape, const3)],
        out_shape=[row_out, row_out, row_out, row_out, jax.ShapeDtypeStruct(prev.shape, F32)],
        compiler_params=_cparams("arbitrary"), name="conv_qkv_decode",
    )(y, prev, cw, cb.reshape(1, W), wq, wk, wv)


def _mlstm_decode_kernel(q_ref, k_ref, v_ref, g_ref, o_ref, z_ref, xconv_ref, ng_ref, skip_ref,
                         c0_ref, n0_ref, m0_ref, *rest, nb, n_alias):
    y_ref, c_out, n_out, m_out = rest[n_alias:]
    W = q_ref.shape[2]
    hd = W // C_HEADS
    first = lax.broadcasted_iota(jnp.int32, (8, hd), 0) == 0
    pad8 = lambda r: jnp.where(first, jnp.broadcast_to(r, (8, hd)), 0.0)
    for b in range(nb):
        ys = []
        for h in range(C_HEADS):
            sl = slice(h * hd, (h + 1) * hd)
            q, k, v = q_ref[b, :, sl], k_ref[b, :, sl], v_ref[b, :, sl]
            log_i, log_f = g_ref[b, h:h + 1, :], g_ref[b, C_HEADS + h:C_HEADS + h + 1, :]
            c0, n0, m0 = c0_ref[0, b, h], n0_ref[0, b, h:h + 1, :], m0_ref[0, b, h:h + 1, :]
            m_new = jnp.maximum(log_f + m0, log_i)
            a = jnp.exp(log_f + m0 - m_new)
            w = jnp.exp(log_i - m_new)
            sm = w * jnp.sum(q * k, -1, keepdims=True)
            num = sm * v + a * _mm(pad8(q), c0)[0:1, :]
            den = sm + a * jnp.sum(q * n0, -1, keepdims=True)
            hc = num / jnp.maximum(jnp.abs(den), jnp.exp(-m_new))
            ktv = lax.dot_general(pad8(k).astype(BF16), pad8(v).astype(BF16), (((0,), (0,)), ((), ())),
                                  preferred_element_type=F32)
            c_out[0, b, h] = a * c0 + w * ktv
            n_out[b, h:h + 1, :] = a * n0 + w * k
            m_out[b, h:h + 1, :] = m_new
            ys.append(_mlstm_gate(hc, o_ref[b, :, sl], z_ref[b, :, sl], xconv_ref[b, :, sl],
                                  ng_ref[:, sl], skip_ref[:, sl]))
        y_ref[b] = jnp.concatenate(ys, axis=1)


def _mlstm_decode(q, k, v, gates, y_in, xconv, ng, skip, state, c_stack, *, layer, n_layers, nb):
    batch, W = q.shape[0], q.shape[2]
    hd = W // C_HEADS
    const2 = lambda b: (0, 0)
    rowblk = lambda k: pl.BlockSpec((nb, 1, W), lambda b: (b, 0, k))
    c0, n0, m0 = state
    in_specs = [rowblk(0), rowblk(0), rowblk(0),
                pl.BlockSpec((nb, GATE_ROWS, 1), lambda b: (b, 0, 0)),
                rowblk(2), rowblk(1), rowblk(0),
                pl.BlockSpec((1, W), const2), pl.BlockSpec((1, W), const2),
                pl.BlockSpec((1, nb, C_HEADS, hd, hd), lambda b: (layer, b, 0, 0, 0)),
                pl.BlockSpec((1, nb, C_HEADS, hd), lambda b: (layer, b, 0, 0)),
                pl.BlockSpec((1, nb, C_HEADS, 1), lambda b: (layer, b, 0, 0))]
    args = [q, k, v, gates, y_in, y_in, xconv, ng.reshape(1, W), skip.reshape(1, W),
            c0, n0, m0.reshape(m0.shape + (1,))]
    aliases = {}
    if c_stack is not None:
        aliases[len(args)] = 1
        in_specs.append(pl.BlockSpec(memory_space=pl.ANY))
        args.append(c_stack)
    out_shape = [jax.ShapeDtypeStruct((batch, 1, W), F32),
                 jax.ShapeDtypeStruct((n_layers, batch, C_HEADS, hd, hd), F32),
                 jax.ShapeDtypeStruct((batch, C_HEADS, hd), F32),
                 jax.ShapeDtypeStruct((batch, C_HEADS, 1), F32)]
    out_specs = [pl.BlockSpec((nb, 1, W), lambda b: (b, 0, 0)),
                 pl.BlockSpec((1, nb, C_HEADS, hd, hd), lambda b: (layer, b, 0, 0, 0)),
                 pl.BlockSpec((nb, C_HEADS, hd), lambda b: (b, 0, 0)),
                 pl.BlockSpec((nb, C_HEADS, 1), lambda b: (b, 0, 0))]
    y, c_new, n_new, m_new = pl.pallas_call(
        functools.partial(_mlstm_decode_kernel, nb=nb, n_alias=len(aliases)),
        grid=(batch // nb,), in_specs=in_specs, out_specs=out_specs, out_shape=out_shape,
        input_output_aliases=aliases,
        compiler_params=_cparams("parallel"), name="mlstm_decode",
    )(*args)
    return y, c_new, n_new, m_new.reshape(batch, C_HEADS)


def _mlstm_head(qh, kth, vh, i_row, b_row, m_prev, caug, tri, one_col):
    L, hd = qh.shape
    g_row = i_row - b_row
    dm = jnp.where(tri, g_row, NEG_INF)
    mcol = jnp.maximum(m_prev, jnp.max(dm, -1, keepdims=True))
    wmat = jnp.exp(dm - mcol)
    a = jnp.exp(m_prev - mcol)
    sm = wmat * _mm(qh, kth)
    qc = _mm(qh, caug)
    num = _mm(sm, vh) + a * qc[:, :hd]
    den = jnp.sum(sm, -1, keepdims=True) + a * qc[:, hd:hd + 1]
    den = jnp.maximum(jnp.abs(den), jnp.exp(-(_row_to_col(b_row) + mcol)))
    m_last = mcol[L - 1:L, :]
    wl = jnp.exp(g_row - m_last)
    a_l = jnp.exp(m_prev - m_last)
    caug_new = a_l * caug + _mm(kth * wl, jnp.concatenate([vh, one_col], axis=1))
    return num / den, caug_new, b_row[:, L - 1:L] + m_last


def _mlstm_gate(hc, o, z, xconv, ng, skip):
    mu = jnp.mean(hc, -1, keepdims=True)
    var = jnp.mean((hc - mu) ** 2, -1, keepdims=True)
    hn = (hc - mu) * lax.rsqrt(var + EPS) * ng
    return (_sigmoid(o) * hn + skip * xconv) * _silu(z)


def _mlstm_prompt_kernel(xc_ref, halo_ref, z_ref, o_ref, g_ref, h_ref, cw_ref, cb_ref, wq_ref, wkt_ref, wv_ref,
                         ng_ref, skip_ref, wout_ref, *rest, nb, n_alias, final):
    if final:
        fg_ref = rest[0]
        rest = rest[1:]
    hout_ref, c_out, n_out, m_out, caug_s, m_s = rest[n_alias:]
    L, W = xc_ref.shape[1], xc_ref.shape[2]
    hd = W // C_HEADS
    ci = pl.program_id(1)

    @pl.when(ci == 0)
    def _():
        caug_s[...] = jnp.zeros(caug_s.shape, F32)
        m_s[...] = jnp.zeros(m_s.shape, F32)

    xcs, xconvs = [], []
    for b in range(nb):
        xc = xc_ref[b]
        halo = jnp.where(ci == 0, 0.0, halo_ref[b])
        xx = jnp.concatenate([halo, xc], axis=0)
        acc = cb_ref[...] + cw_ref[CONV_W - 1:CONV_W, :] * xc
        for k in range(1, CONV_W):
            acc = acc + cw_ref[CONV_W - 1 - k:CONV_W - k, :] * pltpu.roll(xx, k, 0)[8:]
        xcs.append(xc)
        xconvs.append(_silu(acc))
    xc_all = jnp.concatenate(xcs, axis=0)
    xconv_all = jnp.concatenate(xconvs, axis=0)
    tri = (lax.broadcasted_iota(jnp.int32, (L, L), 0) >= lax.broadcasted_iota(jnp.int32, (L, L), 1))
    one_col = (lax.broadcasted_iota(jnp.int32, (L, LANES), 1) == 0).astype(F32)
    ys = [[None] * C_HEADS for _ in range(nb)]
    for h in range(C_HEADS):
        sl = slice(h * hd, (h + 1) * hd)
        xh = xconv_all[:, sl]
        q_h = _mm(xh, wq_ref[h]) * np.float32(hd ** -0.5)
        kt_h = _mm_nt(wkt_ref[h], xh)
        v_h = _mm(xc_all[:, sl], wv_ref[h])
        for b in range(nb):
            r = slice(b * L, (b + 1) * L)
            gates = g_ref[b]
            hc, caug_new, m_new = _mlstm_head(
                q_h[r], kt_h[:, r], v_h[r], gates[h:h + 1, :], gates[C_HEADS + h:C_HEADS + h + 1, :],
                m_s[b, h][0:1, 0:1], caug_s[b, h], tri, one_col)
            caug_s[b, h] = caug_new
            m_s[b, h] = jnp.broadcast_to(m_new, (8, LANES))
            ys[b][h] = _mlstm_gate(hc, o_ref[b][:, sl], z_ref[b][:, sl], xconv_all[r, sl],
                                   ng_ref[:, sl], skip_ref[:, sl])
    y_all = jnp.concatenate([jnp.concatenate(ys[b], axis=1) for b in range(nb)], axis=0)
    hn = jnp.concatenate([h_ref[b] for b in range(nb)], axis=0) + _mm(y_all, wout_ref[...])
    if final:
        hn = hn * lax.rsqrt(jnp.mean(hn * hn, -1, keepdims=True) + EPS) * fg_ref[...]
    for b in range(nb):
        hout_ref[b] = hn[b * L:(b + 1) * L]

    @pl.when(ci == pl.num_programs(1) - 1)
    def _():
        for b in range(nb):
            for h in range(C_HEADS):
                c_out[0, b, h] = caug_s[b, h, :, :hd]
                n_out[b, h] = caug_s[b, h, :, hd:hd + 1]
                m_out[b, h:h + 1, :] = m_s[b, h][0:1, 0:1]


def _mlstm_prompt(y_in, gates, h, cw, cb, wq, wkt, wv, ng, skip, wout, final_g, c_stack, *, layer, n_layers, nb):
    batch, S = y_in.shape[0], y_in.shape[1]
    W, D = wout.shape
    hd = W // C_HEADS
    L = C_CHUNK
    const2 = lambda b, c: (0, 0)
    const3 = lambda b, c: (0, 0, 0)
    yblk = lambda k: pl.BlockSpec((nb, L, W), lambda b, c: (b, c, k))
    in_specs = [yblk(0),
                pl.BlockSpec((nb, 8, W), lambda b, c: (b, jnp.maximum(c * (L // 8) - 1, 0), 0)),
                yblk(1), yblk(2),
                pl.BlockSpec((nb, GATE_ROWS, L), lambda b, c: (b, 0, c)),
                pl.BlockSpec((nb, L, D), lambda b, c: (b, c, 0)),
                pl.BlockSpec(cw.shape, const2), pl.BlockSpec((1, W), const2),
                pl.BlockSpec(wq.shape, const3), pl.BlockSpec(wkt.shape, const3), pl.BlockSpec(wv.shape, const3),
                pl.BlockSpec((1, W), const2), pl.BlockSpec((1, W), const2), pl.BlockSpec(wout.shape, const2)]
    args = [y_in, y_in, y_in, y_in, gates, h, cw, cb.reshape(1, W), wq, wkt, wv,
            ng.reshape(1, W), skip.reshape(1, W), wout]
    if final_g is not None:
        in_specs.append(pl.BlockSpec((1, D), const2))
        args.append(final_g.reshape(1, D))
    aliases = {}
    if c_stack is not None:
        aliases[len(args)] = 1
        in_specs.append(pl.BlockSpec(memory_space=pl.ANY))
        args.append(c_stack)
    out_shape = [jax.ShapeDtypeStruct((batch, S, D), F32),
                 jax.ShapeDtypeStruct((n_layers, batch, C_HEADS, hd, hd), F32),
                 jax.ShapeDtypeStruct((batch, C_HEADS, hd, 1), F32),
                 jax.ShapeDtypeStruct((batch, C_HEADS, 1), F32)]
    out_specs = [pl.BlockSpec((nb, L, D), lambda b, c: (b, c, 0)),
                 pl.BlockSpec((1, nb, C_HEADS, hd, hd), lambda b, c: (layer, b, 0, 0, 0)),
                 pl.BlockSpec((nb, C_HEADS, hd, 1), lambda b, c: (b, 0, 0, 0)),
                 pl.BlockSpec((nb, C_HEADS, 1), lambda b, c: (b, 0, 0))]
    hout, c_new, n_new, m_new = pl.pallas_call(
        functools.partial(_mlstm_prompt_kernel, nb=nb, n_alias=len(aliases), final=final_g is not None),
        grid=(batch // nb, S // L), in_specs=in_specs, out_specs=out_specs, out_shape=out_shape,
        input_output_aliases=aliases,
        scratch_shapes=[pltpu.VMEM((nb, C_HEADS, hd, hd + LANES), F32), pltpu.VMEM((nb, C_HEADS, 8, LANES), F32)],
        compiler_params=_cparams("parallel", "arbitrary"), name="mlstm_layer",
    )(*args)
    return hout, c_new, n_new.reshape(batch, C_HEADS, hd), m_new.reshape(batch, C_HEADS)


def _out_proj_kernel(y_ref, w_ref, h_ref, *rest, final):
    hn = h_ref[...] + _mm(y_ref[...], w_ref[...])
    if final:
        fg_ref, o_ref = rest
        o_ref[...] = hn * lax.rsqrt(jnp.mean(hn * hn, -1, keepdims=True) + EPS) * fg_ref[...]
    else:
        rest[0][...] = hn


def _out_proj(y, w, h, final_g, *, tm):
    T, D = h.shape
    K = y.shape[1]
    const2 = lambda i: (0, 0)
    in_specs = [pl.BlockSpec((tm, K), lambda i: (i, 0)), pl.BlockSpec(w.shape, const2),
                pl.BlockSpec((tm, D), lambda i: (i, 0))]
    args = [y, w, h]
    if final_g is not None:
        in_specs.append(pl.BlockSpec((1, D), const2))
        args.append(final_g.reshape(1, D))
    return pl.pallas_call(
        functools.partial(_out_proj_kernel, final=final_g is not None),
        grid=(T // tm,), in_specs=in_specs, out_specs=pl.BlockSpec((tm, D), lambda i: (i, 0)),
        out_shape=jax.ShapeDtypeStruct((T, D), F32),
        compiler_params=_cparams("parallel"), name="out_proj",
    )(*args)


def kernel(x_prompt, x_sample, cache_k, cache_v, cache_logf, state_c, state_n, state_m, state_conv, page_table,
           norm_g, final_g, even_w_in, even_b_f, gmlp_ln_g, gmlp_ln_b, gmlp_w_s, gmlp_b_s, even_w_out,
           odd_w_in, odd_b_i, odd_b_f, conv_w, conv_b, mlstm_w_q, mlstm_w_k, mlstm_w_v, mlstm_norm_g,
           mlstm_skip, odd_w_out):
    B, S, D = x_prompt.shape
    DB = x_sample.shape[0]
    depth = norm_g.shape[0]
    AW = A_WIDTH
    BW = gmlp_ln_g.shape[1]
    CW = conv_w.shape[2]
    gdim = BW // B_GROUPS
    tm = 512
    assert AW == BW, "even-layer column blocks are addressed in units of one common width"

    hp = x_prompt.reshape(B * S, D)
    hs = x_sample.reshape(DB, D)
    cache_kt = jnp.transpose(cache_k, (0, 1, 3, 4, 2))
    cache_vt = jnp.transpose(cache_v, (0, 1, 3, 4, 2))
    cache_lft = jnp.transpose(cache_logf, (0, 1, 3, 2))

    lp, ks, vs, ls, chv = [], [], [], [], []
    np_, mp, bp, ns, ms, bs = [], [], [], [], [], []
    n_even, n_odd = (depth + 1) // 2, depth // 2
    assert depth % 2 == 0, "the final rmsnorm is fused into the last (mLSTM) layer's output projection"
    qkv_stack = cp_stack = cs_stack = None
    for l in range(depth):
        j = l // 2
        last = l == depth - 1
        if l % 2 == 0:
            w = even_w_in[j]
            o = 0
            parts = {}
            for name, n in (("q", AW), ("k", AW), ("v", AW), ("fg", A_HEADS), ("za", AW), ("u", BW), ("vb", BW), ("zb", BW)):
                parts[name] = w[:, o:o + n]
                o += n
            wnn_s = jnp.concatenate([parts[n] for n in ("za", "u", "vb", "zb", "k", "q", "v")], 1).astype(BF16)
            wqt = parts["q"].T.astype(BF16)
            wkt = parts["k"].T.astype(BF16)
            wvt = parts["v"].T.astype(BF16)
            wg = parts["fg"].T.astype(BF16)
            gb = even_b_f[j].reshape(GATE_ROWS, 1)
            wout = even_w_out[j].astype(BF16)
            bs_full = jnp.repeat(gmlp_b_s[j].T, gdim, axis=1)
            ws_dec = jnp.repeat(gmlp_w_s[j][:, 0, 0], gdim).reshape(1, BW)
            bs_dec = jnp.repeat(gmlp_b_s[j][:, 0], gdim).reshape(1, BW)

            ys, lf = _proj(hs, norm_g[l], wnn_s, [], wg, gb, n_lin=0, batch=1, seq=DB, tm=DB)
            lf_s = lf[0].T
            k_s, q_s, v_s = ys[:, 4 * AW:5 * AW], ys[:, 5 * AW:6 * AW], ys[:, 6 * AW:7 * AW]
            y, qt, kt_all, vt_all, lf = _proj(hp, norm_g[l], wnn_s, [wqt, wkt, wvt], wg, gb, n_lin=0, n_nn=5 * AW,
                                              batch=B, seq=S, tm=tm, layer=j, n_layers=n_even, stacked=qkv_stack)
            qkv_stack = [qt, kt_all, vt_all]
            c = _seq_cumsum(lf)

            att, att_s = _fox_attention(y, qt, vt_all, c, q_s, k_s, v_s, lf_s, cache_kt, cache_vt, cache_lft,
                                        page_table, k_col=4 * AW, layer=j, batch=B, seq=S, tile=512,
                                        group_pages=8)
            hp = _even_mix(att, y, hp, gmlp_ln_g[j], gmlp_ln_b[j], gmlp_w_s[j], bs_full, wout,
                           col0=0, tm=tm, decode=False)[0]
            lp.append(jnp.transpose(lf, (0, 2, 1)))

            hs, vn_s = _even_mix(att_s, ys, hs, gmlp_ln_g[j], gmlp_ln_b[j], ws_dec, bs_dec, wout,
                                 col0=0, tm=DB, decode=True)
            ks.append(k_s.reshape(DB, 1, A_HEADS, A_HEAD_DIM))
            vs.append(v_s.reshape(DB, 1, A_HEADS, A_HEAD_DIM))
            ls.append(lf_s.reshape(DB, 1, A_HEADS))
            chv.append(vn_s.reshape(DB, 1, BW))
        else:
            w = odd_w_in[j]
            wnn = w[:, :3 * CW].astype(BF16)
            wg = w[:, 3 * CW:].T.astype(BF16)
            gb = jnp.concatenate([odd_b_i[j], odd_b_f[j]]).reshape(GATE_ROWS, 1)
            wq = mlstm_w_q[j].astype(BF16)
            wk = mlstm_w_k[j].astype(BF16)
            wkT = jnp.swapaxes(mlstm_w_k[j], 1, 2).astype(BF16)
            wv = mlstm_w_v[j].astype(BF16)
            wout = odd_w_out[j].astype(BF16)
            fin = final_g if last else None

            y, g = _proj(hp, norm_g[l], wnn, [], wg, gb, n_lin=C_HEADS, batch=B, seq=S, tm=tm, cum_chunk=C_CHUNK)
            hout, cp_stack, n1, m1 = _mlstm_prompt(
                y.reshape(B, S, 3 * CW), g, hp.reshape(B, S, D), conv_w[j], conv_b[j], wq, wkT, wv,
                mlstm_norm_g[j], mlstm_skip[j], wout, fin, cp_stack, layer=j, n_layers=n_odd, nb=2)
            hp = hout.reshape(B * S, D)
            np_.append(n1); mp.append(m1)
            bp.append(y.reshape(B, S, 3 * CW)[:, S - (CONV_W - 1):, :CW])

            y, g = _proj(hs, norm_g[l], wnn, [], wg, gb, n_lin=C_HEADS, batch=1, seq=DB, tm=DB)
            buf = jnp.transpose(state_conv[j], (1, 0, 2))
            xconv, q, k, v, nbuf = _qkv_decode(y, buf, conv_w[j], conv_b[j], wq, wk, wv)
            g_s = jnp.transpose(g, (2, 1, 0))
            r3 = lambda a: a.reshape(DB, 1, a.shape[1])
            ym, cs_stack, n2, m2 = _mlstm_decode(r3(q), r3(k), r3(v), g_s, r3(y), r3(xconv), mlstm_norm_g[j],
                                                 mlstm_skip[j], (state_c, state_n, state_m), cs_stack,
                                                 layer=j, n_layers=n_odd, nb=4)
            hs = _out_proj(ym.reshape(DB, CW), wout, hs, fin, tm=DB)
            ns.append(n2); ms.append(m2)
            bs.append(jnp.transpose(nbuf, (1, 0, 2)))
    y_prompt = hp.reshape(B, S, D)
    y_sample = hs.reshape(DB, 1, D)
    to_bshd = lambda t: jnp.transpose(t.reshape(n_even, B, A_HEADS, A_HEAD_DIM, S), (0, 1, 4, 2, 3))
    return (y_prompt, y_sample,
            to_bshd(qkv_stack[1]), to_bshd(qkv_stack[2]), jnp.stack(lp),
            jnp.stack(ks), jnp.stack(vs), jnp.stack(ls), jnp.stack(chv),
            cp_stack, jnp.stack(np_), jnp.stack(mp), jnp.stack(bp),
            cs_stack, jnp.stack(ns), jnp.stack(ms), jnp.stack(bs))
```

```python
import functools

import jax
import jax.numpy as jnp
import numpy as np
from jax import lax
from jax.experimental import pallas as pl
from jax.experimental.pallas import tpu as pltpu

F32 = jnp.float32
BF16 = jnp.bfloat16
EPS = 1e-6
NEG_INF = float("-inf")
LOG2E = 1.4426950408889634

A_HEADS = 8
A_HEAD_DIM = 64
A_WIDTH = A_HEADS * A_HEAD_DIM
B_GROUPS = 8
B_CHUNK = 128
C_HEADS = 4
C_CHUNK = 128
CONV_W = 4
LANES = 128
GATE_ROWS = 8
VMEM_LIMIT = 56 * 1024 * 1024


def _cparams(*sem):
    return pltpu.CompilerParams(dimension_semantics=sem, vmem_limit_bytes=VMEM_LIMIT)


def _mm(a, b):
    return jnp.dot(a.astype(BF16), b.astype(BF16), preferred_element_type=F32)


def _mm_nt(a, b):
    return lax.dot_general(a.astype(BF16), b.astype(BF16), (((1,), (1,)), ((), ())),
                           preferred_element_type=F32)


def _log_sigmoid(x):
    return jnp.minimum(x, 0.0) - jnp.log1p(jnp.exp(-jnp.abs(x)))


def _sigmoid(x):
    return 1.0 / (1.0 + jnp.exp(-x))


def _silu(x):
    return x * _sigmoid(x)


def _gelu(x):
    return 0.5 * x * (1.0 + lax.erf(x * np.float32(np.sqrt(0.5))))


def _row_to_col(r):
    n = r.shape[1]
    eye = lax.broadcasted_iota(jnp.int32, (n, n), 0) == lax.broadcasted_iota(jnp.int32, (n, n), 1)
    return jnp.sum(jnp.where(eye, r, 0.0), axis=1, keepdims=True)


def _proj_kernel(x_ref, g_ref, wnn_ref, wg_ref, gb_ref, *rest, n_nt, n_alias, n_lin, col_chunk, cum_chunk):
    wnt_refs = rest[:n_nt]
    rest = rest[n_nt + n_alias:]
    y_ref = rest[0]
    yt_refs = rest[1:1 + n_nt]
    gt_ref = rest[1 + n_nt]
    x = x_ref[...]
    xn = x * lax.rsqrt(jnp.mean(x * x, -1, keepdims=True) + EPS) * g_ref[...]
    xb = xn.astype(BF16)
    n_nn = y_ref.shape[1]
    for c in range(0, n_nn, col_chunk):
        y_ref[:, c:c + col_chunk] = jnp.dot(xb, wnn_ref[:, c:c + col_chunk], preferred_element_type=F32)
    for w_ref, o_ref in zip(wnt_refs, yt_refs):
        o_ref[0, 0] = _mm_nt(w_ref[...], xb)
    gt = _mm_nt(wg_ref[...], xb) + gb_ref[...]
    row = lax.broadcasted_iota(jnp.int32, gt.shape, 0)
    gt = jnp.where(row >= n_lin, _log_sigmoid(gt), gt)
    if cum_chunk is None:
        gt_ref[0] = gt
    else:
        upto = (lax.broadcasted_iota(jnp.int32, (cum_chunk, cum_chunk), 0)
                <= lax.broadcasted_iota(jnp.int32, (cum_chunk, cum_chunk), 1)).astype(F32)
        rowc = row[:, :cum_chunk]
        for c in range(0, gt.shape[1], cum_chunk):
            blk = gt[:, c:c + cum_chunk]
            cs = jnp.dot(blk, upto, precision=lax.Precision.HIGHEST, preferred_element_type=F32)
            gt_ref[0, :, c:c + cum_chunk] = jnp.where(rowc >= n_lin, cs, blk)


def _proj(x, g, wnn, wnts, wg, gb, *, n_lin, batch, seq, tm, n_nn=None, cum_chunk=None, layer=0, n_layers=1,
          stacked=None):
    T, D = x.shape
    tps = seq // tm
    n_nn = wnn.shape[1] if n_nn is None else n_nn
    n_nt = len(wnts)
    const = lambda i: (0, 0)
    tok_t = lambda i: (i // tps, 0, i % tps)
    in_specs = [pl.BlockSpec((tm, D), lambda i: (i, 0)),
                pl.BlockSpec((1, D), const),
                pl.BlockSpec((D, n_nn), const),
                pl.BlockSpec((GATE_ROWS, D), const),
                pl.BlockSpec((GATE_ROWS, 1), const)]
    in_specs += [pl.BlockSpec(w.shape, const) for w in wnts]
    args = [x, g.reshape(1, D), wnn, wg, gb, *wnts]
    aliases = {}
    if stacked is not None:
        for k, buf in enumerate(stacked):
            aliases[len(args)] = 1 + k
            in_specs.append(pl.BlockSpec(memory_space=pl.ANY))
            args.append(buf)
    out_shape = [jax.ShapeDtypeStruct((T, n_nn), F32)]
    out_specs = [pl.BlockSpec((tm, n_nn), lambda i: (i, 0))]
    for w in wnts:
        out_shape.append(jax.ShapeDtypeStruct((n_layers, batch, w.shape[0], seq), F32))
        out_specs.append(pl.BlockSpec((1, 1, w.shape[0], tm), lambda i: (layer, i // tps, 0, i % tps)))
    out_shape.append(jax.ShapeDtypeStruct((batch, GATE_ROWS, seq), F32))
    out_specs.append(pl.BlockSpec((1, GATE_ROWS, tm), tok_t))
    return pl.pallas_call(
        functools.partial(_proj_kernel, n_nt=n_nt, n_alias=len(aliases), n_lin=n_lin, col_chunk=512,
                          cum_chunk=cum_chunk),
        grid=(T // tm,), in_specs=in_specs, out_specs=out_specs, out_shape=out_shape,
        input_output_aliases=aliases,
        compiler_params=_cparams("parallel"), name="norm_proj",
    )(*args)


def _cumsum_kernel(x_ref, row_ref, col_ref):
    S = x_ref.shape[2]
    upto = (lax.broadcasted_iota(jnp.int32, (LANES, LANES), 0)
            <= lax.broadcasted_iota(jnp.int32, (LANES, LANES), 1)).astype(F32)
    carry = jnp.zeros((GATE_ROWS, 1), F32)
    for c in range(0, S, LANES):
        inc = jnp.dot(x_ref[0, :, c:c + LANES], upto, precision=lax.Precision.HIGHEST,
                      preferred_element_type=F32) + carry
        carry = inc[:, LANES - 1:LANES]
        inc2 = inc * np.float32(LOG2E)
        inc2_t = jnp.concatenate([inc2, jnp.zeros((LANES - GATE_ROWS, LANES), F32)], axis=0).T
        for p in range(GATE_ROWS // 2):
            row_ref[0, p, :, c:c + LANES] = inc2[2 * p:2 * p + 2, :]
            col_ref[0, p, c:c + LANES, :] = inc2_t[:, 2 * p:2 * p + 2]


def _seq_cumsum(x):
    B, R, S = x.shape
    return pl.pallas_call(_cumsum_kernel, grid=(B,),
                          in_specs=[pl.BlockSpec((1, R, S), lambda b: (b, 0, 0))],
                          out_specs=[pl.BlockSpec((1, R // 2, 2, S), lambda b: (b, 0, 0, 0)),
                                     pl.BlockSpec((1, R // 2, S, 2), lambda b: (b, 0, 0, 0))],
                          out_shape=[jax.ShapeDtypeStruct((B, R // 2, 2, S), F32),
                                     jax.ShapeDtypeStruct((B, R // 2, S, 2), F32)],
                          compiler_params=_cparams("parallel"), name="logf_cumsum")(x)


def _fox_kernel(qi_ref, kj_ref, pt_ref, qt_ref, k_ref, vt_ref, crow_ref, ccol_ref,
                dq_ref, dkn_ref, dvn_ref, dlf_ref, kc_hbm, vc_hbm, lfc_hbm, o_ref, do_ref,
                q_s, m_s, l_s, a_s, *decode_scratch, tile, layer, group_pages):
    qi = qi_ref[pl.program_id(2)]
    kj = kj_ref[pl.program_id(2)]
    hd = A_HEAD_DIM
    step = (pl.program_id(0) * pl.num_programs(1) + pl.program_id(1)) * pl.num_programs(2) + pl.program_id(2)
    seq_groups = pt_ref.shape[1] // group_pages
    n_groups = pt_ref.shape[0] * seq_groups
    start_group, wait_group, init_seq, compute_group, finish_seq = _paged_decode_ops(
        pt_ref, dq_ref, dkn_ref, dvn_ref, dlf_ref, kc_hbm, vc_hbm, lfc_hbm, do_ref, *decode_scratch,
        layer=layer, group_pages=group_pages)
    n_slots = decode_scratch[0].shape[0]
    ahead = n_slots - 1
    slot = step % n_slots

    @pl.when(step == 0)
    def _():
        for g0 in range(ahead):
            start_group(g0 // seq_groups, g0 % seq_groups, g0)

    @pl.when(step + ahead < n_groups)
    def _():
        start_group((step + ahead) // seq_groups, (step + ahead) % seq_groups, (step + ahead) % n_slots)

    @pl.when(kj == 0)
    def _():
        qt = qt_ref[0, 0] * np.float32(hd ** -0.5 * LOG2E)
        row = lax.broadcasted_iota(jnp.int32, qt.shape, 0)
        q_s[0] = jnp.where(row < hd, qt, 0.0).astype(BF16)
        q_s[1] = jnp.where(row >= hd, qt, 0.0).astype(BF16)
        m_s[...] = jnp.full(m_s.shape, NEG_INF, F32)
        l_s[...] = jnp.zeros(l_s.shape, F32)
        a_s[...] = jnp.zeros(a_s.shape, F32)

    def attend(diagonal):
        kb = k_ref[...].astype(BF16)
        cq = crow_ref[0, 0]
        ck = ccol_ref[0, 0]
        ones = jnp.ones((16, tile), BF16)
        if diagonal:
            causal = (lax.broadcasted_iota(jnp.int32, (tile, tile), 0)
                      <= lax.broadcasted_iota(jnp.int32, (tile, tile), 1))
        for hh in range(2):
            rows = slice(hh * hd, (hh + 1) * hd)
            s = jnp.dot(kb, q_s[hh], preferred_element_type=F32)
            s = s + cq[hh:hh + 1, :] - ck[:, hh:hh + 1]
            if diagonal:
                s = jnp.where(causal, s, NEG_INF)
            m_old = m_s[hh]
            m_new = jnp.maximum(m_old, jnp.max(s, 0, keepdims=True))
            alpha = jnp.exp2(m_old - m_new)
            p = jnp.exp2(s - m_new).astype(BF16)
            vt = jnp.concatenate([vt_ref[0, 0, rows, :].astype(BF16), ones], axis=0)
            pv = jnp.dot(vt, p, preferred_element_type=F32)
            a_s[rows, :] = alpha * a_s[rows, :] + pv[:hd]
            l_s[hh] = alpha * l_s[hh] + pv[hd:hd + 1]
            m_s[hh] = m_new

    def prompt_tile(diagonal):
        attend(diagonal)
        if diagonal:
            row = lax.broadcasted_iota(jnp.int32, (2 * hd, tile), 0)
            out_t = a_s[...] / jnp.where(row < hd, l_s[0], l_s[1])
            o_ref[...] = out_t.T

    has_group = step < n_groups

    @pl.when(has_group & (step % seq_groups == 0))
    def _():
        init_seq()

    for diagonal in (False, True):
        on_tile = (kj == qi) if diagonal else (kj < qi)

        @pl.when(on_tile & has_group)
        def _():
            wait_group(slot)
            prompt_tile(diagonal)
            compute_group(slot)

        @pl.when(on_tile & jnp.logical_not(has_group))
        def _():
            prompt_tile(diagonal)

    @pl.when(has_group & (step % seq_groups == seq_groups - 1))
    def _():
        finish_seq()


def _fox_attention(y, qt, vt, c, q, k_new, v_new, lf_new, cache_kt, cache_vt, cache_lft, page_table, *,
                   k_col, layer, batch, seq, tile, group_pages):
    T = y.shape[0]
    DB, n_pages = page_table.shape
    pairs = A_HEADS // 2
    nt = seq // tile
    kb0 = k_col // LANES
    crow, ccol = c
    tri = [(i, j) for i in range(nt) for j in range(i + 1)]
    qi_tab = jnp.asarray([i for i, _ in tri], jnp.int32)
    kj_tab = jnp.asarray([j for _, j in tri], jnp.int32)
    n_tri = len(tri)
    P = group_pages
    n_slots = 3
    seq_groups = n_pages // P
    assert n_pages % P == 0 and n_slots - 1 <= DB * seq_groups <= batch * pairs * n_tri, \
        "one page group per grid step"
    col = lambda a: a.reshape(DB, a.shape[1], 1)
    dec_seq = lambda b, p, t: jnp.minimum(((b * pairs + p) * n_tri + t) // seq_groups, DB - 1)
    vec = lambda n: pl.BlockSpec((1, n, 1), lambda b, p, t, qi, kj, pt: (dec_seq(b, p, t), 0, 0))
    hbm = pl.BlockSpec(memory_space=pl.ANY)
    page = (A_HEADS, A_HEAD_DIM, LANES)
    grid_spec = pltpu.PrefetchScalarGridSpec(
        num_scalar_prefetch=3, grid=(batch, pairs, n_tri),
        in_specs=[pl.BlockSpec((1, 1, LANES, tile), lambda b, p, t, qi, kj, pt: (layer, b, p, qi[t])),
                  pl.BlockSpec((tile, LANES), lambda b, p, t, qi, kj, pt: (b * nt + kj[t], kb0 + p)),
                  pl.BlockSpec((1, 1, LANES, tile), lambda b, p, t, qi, kj, pt: (layer, b, p, kj[t])),
                  pl.BlockSpec((1, 1, 2, tile), lambda b, p, t, qi, kj, pt: (b, p, 0, qi[t])),
                  pl.BlockSpec((1, 1, tile, 2), lambda b, p, t, qi, kj, pt: (b, p, kj[t], 0)),
                  vec(A_WIDTH), vec(A_WIDTH), vec(A_WIDTH), vec(A_HEADS), hbm, hbm, hbm],
        out_specs=[pl.BlockSpec((tile, LANES), lambda b, p, t, qi, kj, pt: (b * nt + qi[t], p)),
                   vec(A_WIDTH)],
        scratch_shapes=[pltpu.VMEM((2, LANES, tile), BF16), pltpu.VMEM((2, 1, tile), F32),
                        pltpu.VMEM((2, 1, tile), F32), pltpu.VMEM((LANES, tile), F32),
                        pltpu.VMEM((n_slots, P) + page, F32), pltpu.VMEM((n_slots, P) + page, F32),
                        pltpu.VMEM((n_slots, P, A_HEADS, LANES), F32),
                        pltpu.SemaphoreType.DMA((3, n_slots)),
                        pltpu.VMEM((A_HEADS, 1), F32), pltpu.VMEM((A_HEADS, 1), F32),
                        pltpu.VMEM((A_HEADS, 1), F32),
                        pltpu.VMEM(page, F32), pltpu.VMEM(page, F32)])
    att, att_dec = pl.pallas_call(
        functools.partial(_fox_kernel, tile=tile, layer=layer, group_pages=P),
        grid_spec=grid_spec,
        out_shape=[jax.ShapeDtypeStruct((T, A_WIDTH), F32), jax.ShapeDtypeStruct((DB, A_WIDTH, 1), F32)],
        compiler_params=_cparams("arbitrary", "arbitrary", "arbitrary"),
        name="fox_attention",
    )(qi_tab, kj_tab, page_table, qt, y, vt, crow, ccol, col(q), col(k_new), col(v_new), col(lf_new),
      cache_kt, cache_vt, cache_lft)
    return att, att_dec.reshape(DB, A_WIDTH)


def _paged_decode_ops(pt_ref, q_ref, kn_ref, vn_ref, lfn_ref, kc_hbm, vc_hbm, lfc_hbm, o_ref,
                      kbuf, vbuf, lfbuf, sem, m_s, l_s, c_s, a_s, q_s, *, layer, group_pages):
    P = group_pages
    n_pages = pt_ref.shape[1]
    H, hd = A_HEADS, A_HEAD_DIM
    hsl = [slice(h * hd, (h + 1) * hd) for h in range(H)]

    def group_copies(slot, page_of):
        out = []
        for i in range(P):
            pid = page_of(i)
            out.append(pltpu.make_async_copy(kc_hbm.at[layer, pid], kbuf.at[slot, i], sem.at[0, slot]))
            out.append(pltpu.make_async_copy(vc_hbm.at[layer, pid], vbuf.at[slot, i], sem.at[1, slot]))
            out.append(pltpu.make_async_copy(lfc_hbm.at[layer, pid], lfbuf.at[slot, i], sem.at[2, slot]))
        return out

    def start_group(seq, g, slot):
        for cp in group_copies(slot, lambda i: pt_ref[seq, n_pages - 1 - (g * P + i)]):
            cp.start()

    def wait_group(slot):
        for cp in group_copies(slot, lambda i: 0):
            cp.wait()

    later = (lax.broadcasted_iota(jnp.int32, (LANES, LANES), 0)
             > lax.broadcasted_iota(jnp.int32, (LANES, LANES), 1)).astype(F32)
    lane = lax.broadcasted_iota(jnp.int32, (hd, LANES), 1)

    def init_seq():
        qcols = [q_ref[0, hsl[h], :] * np.float32(hd ** -0.5) for h in range(H)]
        for h in range(H):
            q_s[h] = jnp.broadcast_to(qcols[h], (hd, LANES))
        m_s[...] = jnp.concatenate(
            [jnp.sum(qcols[h] * kn_ref[0, hsl[h], :], axis=0, keepdims=True) for h in range(H)], axis=0)
        l_s[...] = jnp.ones(l_s.shape, F32)
        c_s[...] = lfn_ref[0]
        for h in range(H):
            a_s[h] = jnp.where(lane == 0, vn_ref[0, hsl[h], :], 0.0)

    def compute_group(slot):
        c = c_s[...]
        lf_all = jnp.concatenate([lfbuf[slot, i] for i in range(P)], axis=0)
        excl_all = jnp.dot(lf_all, later, precision=lax.Precision.HIGHEST, preferred_element_type=F32)
        tot_all = jnp.sum(lf_all, -1, keepdims=True)
        s_pages = []
        for i in range(P):
            rows = [jnp.sum(kbuf[slot, i, h] * q_s[h], axis=0, keepdims=True) for h in range(H)]
            s_pages.append(jnp.concatenate(rows, axis=0) + (c + excl_all[i * H:(i + 1) * H, :]))
            c = c + tot_all[i * H:(i + 1) * H, :]
        c_s[...] = c
        s_all = jnp.concatenate(s_pages, axis=1)
        m_old = m_s[...]
        m_new = jnp.maximum(m_old, jnp.max(s_all, -1, keepdims=True))
        alpha = jnp.exp(m_old - m_new)
        p_all = jnp.exp(s_all - m_new)
        l_s[...] = alpha * l_s[...] + jnp.sum(p_all, -1, keepdims=True)
        m_s[...] = m_new
        for h in range(H):
            acc = a_s[h] * alpha[h:h + 1, :]
            for i in range(P):
                acc = acc + p_all[h:h + 1, i * LANES:(i + 1) * LANES] * vbuf[slot, i, h]
            a_s[h] = acc

    def finish_seq():
        l = l_s[...]
        for h in range(H):
            o_ref[0, hsl[h], :] = jnp.sum(a_s[h], axis=1, keepdims=True) / l[h:h + 1, :]

    return start_group, wait_group, init_seq, compute_group, finish_seq


def _even_mix_kernel(att_ref, za_ref, u_ref, vb_ref, zb_ref, h_ref, lng_ref, lnb_ref, ws_ref, bs_ref,
                     wout_ref, *outs, decode):
    o_ref = outs[0]
    ya = att_ref[...] * _silu(za_ref[...])
    u = _gelu(u_ref[...])
    vf = _gelu(vb_ref[...])
    mu = jnp.mean(vf, -1, keepdims=True)
    var = jnp.mean((vf - mu) ** 2, -1, keepdims=True)
    vn = (vf - mu) * lax.rsqrt(var + EPS) * lng_ref[...] + lnb_ref[...]
    tm, bw = vn.shape
    if decode:
        outs[1][...] = vn
        mix = vn * ws_ref[...] + bs_ref[...]
    else:
        lane = lax.broadcasted_iota(jnp.int32, (B_CHUNK, LANES), 1)
        tri = (lax.broadcasted_iota(jnp.int32, (B_CHUNK, B_CHUNK), 0)
               >= lax.broadcasted_iota(jnp.int32, (B_CHUNK, B_CHUNK), 1))
        wtril = [jnp.where(tri, ws_ref[g], 0.0).astype(BF16) for g in range(B_GROUPS)]
        gpl = LANES // (bw // B_GROUPS)
        rows = []
        for c in range(0, tm, B_CHUNK):
            blocks = []
            for lb in range(bw // LANES):
                vblk = vn[c:c + B_CHUNK, lb * LANES:(lb + 1) * LANES].astype(BF16)
                y0 = jnp.dot(wtril[lb * gpl], vblk, preferred_element_type=F32)
                y1 = jnp.dot(wtril[lb * gpl + 1], vblk, preferred_element_type=F32)
                blocks.append(jnp.where(lane < LANES // gpl, y0, y1))
            rows.append(jnp.concatenate(blocks, axis=1) + bs_ref[...])
        mix = jnp.concatenate(rows, axis=0)
    yb = u * mix * _silu(zb_ref[...])
    aw = ya.shape[1]
    hn = h_ref[...] + _mm(ya, wout_ref[:aw, :]) + _mm(yb, wout_ref[aw:, :])
    o_ref[...] = hn


def _even_mix(att, y, h, ln_g, ln_b, ws, bs, wout, *, col0, tm, decode):
    T, D = h.shape
    aw = att.shape[1]
    const2 = lambda i: (0, 0)
    yblk = lambda k: pl.BlockSpec((tm, aw), lambda i: (i, col0 + k))
    ws_spec = (pl.BlockSpec(ws.shape, const2) if decode else pl.BlockSpec(ws.shape, lambda i: (0, 0, 0)))
    in_specs = [pl.BlockSpec((tm, aw), lambda i: (i, 0)), yblk(0), yblk(1), yblk(2), yblk(3),
                pl.BlockSpec((tm, D), lambda i: (i, 0)),
                pl.BlockSpec((1, aw), const2), pl.BlockSpec((1, aw), const2),
                ws_spec, pl.BlockSpec(bs.shape, const2), pl.BlockSpec(wout.shape, const2)]
    out_shape = [jax.ShapeDtypeStruct((T, D), F32)]
    out_specs = [pl.BlockSpec((tm, D), lambda i: (i, 0))]
    if decode:
        out_shape.append(jax.ShapeDtypeStruct((T, aw), F32))
        out_specs.append(pl.BlockSpec((tm, aw), lambda i: (i, 0)))
    return pl.pallas_call(
        functools.partial(_even_mix_kernel, decode=decode),
        grid=(T // tm,), in_specs=in_specs, out_specs=out_specs, out_shape=out_shape,
        compiler_params=_cparams("parallel"), name="even_mix",
    )(att, y, y, y, y, h, ln_g.reshape(1, aw), ln_b.reshape(1, aw), ws, bs, wout)


def _qkv_decode_kernel(xc_ref, prev_ref, cw_ref, cb_ref, wq_ref, wk_ref, wv_ref,
                       xconv_ref, q_ref, k_ref, v_ref, nb_ref):
    xc = xc_ref[...]
    hd = xc.shape[1] // C_HEADS
    acc = cb_ref[...] + cw_ref[CONV_W - 1:CONV_W, :] * xc
    for j in range(CONV_W - 1):
        acc = acc + cw_ref[j:j + 1, :] * prev_ref[j]
    for j in range(CONV_W - 2):
        nb_ref[j] = prev_ref[j + 1]
    nb_ref[CONV_W - 2] = xc
    xconv = _silu(acc)
    xconv_ref[...] = xconv
    for h in range(C_HEADS):
        sl = slice(h * hd, (h + 1) * hd)
        q_ref[:, sl] = _mm(xconv[:, sl], wq_ref[h]) * np.float32(hd ** -0.5)
        k_ref[:, sl] = _mm(xconv[:, sl], wk_ref[h])
        v_ref[:, sl] = _mm(xc[:, sl], wv_ref[h])


def _qkv_decode(y, prev, cw, cb, wq, wk, wv):
    T = y.shape[0]
    W = cw.shape[1]
    const2 = lambda i: (0, 0)
    const3 = lambda i: (0, 0, 0)
    row_blk = pl.BlockSpec((T, W), lambda i: (0, 0))
    row_out = jax.ShapeDtypeStruct((T, W), F32)
    return pl.pallas_call(
        _qkv_decode_kernel, grid=(1,),
        in_specs=[row_blk, pl.BlockSpec(prev.shape, const3), pl.BlockSpec(cw.shape, const2),
                  pl.BlockSpec((1, W), const2), pl.BlockSpec(wq.shape, const3), pl.BlockSpec(wk.shape, const3),
                  pl.BlockSpec(wv.shape, const3)],
        out_specs=[row_blk, row_blk, row_blk, row_blk, pl.BlockSpec(prev.shape, const3)],
        out_shape=[row_out, row_out, row_out, row_out, jax.ShapeDtypeStruct(prev.shape, F32)],
        compiler_params=_cparams("arbitrary"), name="conv_qkv_decode",
    )(y, prev, cw, cb.reshape(1, W), wq, wk, wv)


def _mlstm_decode_kernel(q_ref, k_ref, v_ref, g_ref, o_ref, z_ref, xconv_ref, ng_ref, skip_ref,
                         c0_ref, n0_ref, m0_ref, *rest, nb, n_alias):
    y_ref, c_out, n_out, m_out = rest[n_alias:]
    W = q_ref.shape[2]
    hd = W // C_HEADS
    first = lax.broadcasted_iota(jnp.int32, (8, hd), 0) == 0
    pad8 = lambda r: jnp.where(first, jnp.broadcast_to(r, (8, hd)), 0.0)
    for b in range(nb):
        ys = []
        for h in range(C_HEADS):
            sl = slice(h * hd, (h + 1) * hd)
            q, k, v = q_ref[b, :, sl], k_ref[b, :, sl], v_ref[b, :, sl]
            log_i, log_f = g_ref[b, h:h + 1, :], g_ref[b, C_HEADS + h:C_HEADS + h + 1, :]
            c0, n0, m0 = c0_ref[0, b, h], n0_ref[0, b, h:h + 1, :], m0_ref[0, b, h:h + 1, :]
            m_new = jnp.maximum(log_f + m0, log_i)
            a = jnp.exp(log_f + m0 - m_new)
            w = jnp.exp(log_i - m_new)
            sm = w * jnp.sum(q * k, -1, keepdims=True)
            num = sm * v + a * _mm(pad8(q), c0)[0:1, :]
            den = sm + a * jnp.sum(q * n0, -1, keepdims=True)
            hc = num / jnp.maximum(jnp.abs(den), jnp.exp(-m_new))
            ktv = lax.dot_general(pad8(k).astype(BF16), pad8(v).astype(BF16), (((0,), (0,)), ((), ())),
                                  preferred_element_type=F32)
            c_out[0, b, h] = a * c0 + w * ktv
            n_out[b, h:h + 1, :] = a * n0 + w * k
            m_out[b, h:h + 1, :] = m_new
            ys.append(_mlstm_gate(hc, o_ref[b, :, sl], z_ref[b, :, sl], xconv_ref[b, :, sl],
                                  ng_ref[:, sl], skip_ref[:, sl]))
        y_ref[b] = jnp.concatenate(ys, axis=1)


def _mlstm_decode(q, k, v, gates, y_in, xconv, ng, skip, state, c_stack, *, layer, n_layers, nb):
    batch, W = q.shape[0], q.shape[2]
    hd = W // C_HEADS
    const2 = lambda b: (0, 0)
    rowblk = lambda k: pl.BlockSpec((nb, 1, W), lambda b: (b, 0, k))
    c0, n0, m0 = state
    in_specs = [rowblk(0), rowblk(0), rowblk(0),
                pl.BlockSpec((nb, GATE_ROWS, 1), lambda b: (b, 0, 0)),
                rowblk(2), rowblk(1), rowblk(0),
                pl.BlockSpec((1, W), const2), pl.BlockSpec((1, W), const2),
                pl.BlockSpec((1, nb, C_HEADS, hd, hd), lambda b: (layer, b, 0, 0, 0)),
                pl.BlockSpec((1, nb, C_HEADS, hd), lambda b: (layer, b, 0, 0)),
                pl.BlockSpec((1, nb, C_HEADS, 1), lambda b: (layer, b, 0, 0))]
    args = [q, k, v, gates, y_in, y_in, xconv, ng.reshape(1, W), skip.reshape(1, W),
            c0, n0, m0.reshape(m0.shape + (1,))]
    aliases = {}
    if c_stack is not None:
        aliases[len(args)] = 1
        in_specs.append(pl.BlockSpec(memory_space=pl.ANY))
        args.append(c_stack)
    out_shape = [jax.ShapeDtypeStruct((batch, 1, W), F32),
                 jax.ShapeDtypeStruct((n_layers, batch, C_HEADS, hd, hd), F32),
                 jax.ShapeDtypeStruct((batch, C_HEADS, hd), F32),
                 jax.ShapeDtypeStruct((batch, C_HEADS, 1), F32)]
    out_specs = [pl.BlockSpec((nb, 1, W), lambda b: (b, 0, 0)),
                 pl.BlockSpec((1, nb, C_HEADS, hd, hd), lambda b: (layer, b, 0, 0, 0)),
                 pl.BlockSpec((nb, C_HEADS, hd), lambda b: (b, 0, 0)),
                 pl.BlockSpec((nb, C_HEADS, 1), lambda b: (b, 0, 0))]
    y, c_new, n_new, m_new = pl.pallas_call(
        functools.partial(_mlstm_decode_kernel, nb=nb, n_alias=len(aliases)),
        grid=(batch // nb,), in_specs=in_specs, out_specs=out_specs, out_shape=out_shape,
        input_output_aliases=aliases,
        compiler_params=_cparams("parallel"), name="mlstm_decode",
    )(*args)
    return y, c_new, n_new, m_new.reshape(batch, C_HEADS)


def _mlstm_head(qh, kth, vh, i_row, b_row, m_prev, caug, tri, one_col):
    L, hd = qh.shape
    g_row = i_row - b_row
    dm = jnp.where(tri, g_row, NEG_INF)
    mcol = jnp.maximum(m_prev, jnp.max(dm, -1, keepdims=True))
    wmat = jnp.exp(dm - mcol)
    a = jnp.exp(m_prev - mcol)
    sm = wmat * _mm(qh, kth)
    qc = _mm(qh, caug)
    num = _mm(sm, vh) + a * qc[:, :hd]
    den = jnp.sum(sm, -1, keepdims=True) + a * qc[:, hd:hd + 1]
    den = jnp.maximum(jnp.abs(den), jnp.exp(-(_row_to_col(b_row) + mcol)))
    m_last = mcol[L - 1:L, :]
    wl = jnp.exp(g_row - m_last)
    a_l = jnp.exp(m_prev - m_last)
    caug_new = a_l * caug + _mm(kth * wl, jnp.concatenate([vh, one_col], axis=1))
    return num / den, caug_new, b_row[:, L - 1:L] + m_last


def _mlstm_gate(hc, o, z, xconv, ng, skip):
    mu = jnp.mean(hc, -1, keepdims=True)
    var = jnp.mean((hc - mu) ** 2, -1, keepdims=True)
    hn = (hc - mu) * lax.rsqrt(var + EPS) * ng
    return (_sigmoid(o) * hn + skip * xconv) * _silu(z)


def _mlstm_prompt_kernel(xc_ref, halo_ref, z_ref, o_ref, g_ref, h_ref, cw_ref, cb_ref, wq_ref, wkt_ref, wv_ref,
                         ng_ref, skip_ref, wout_ref, *rest, nb, n_alias, final):
    if final:
        fg_ref = rest[0]
        rest = rest[1:]
    hout_ref, c_out, n_out, m_out, caug_s, m_s = rest[n_alias:]
    L, W = xc_ref.shape[1], xc_ref.shape[2]
    hd = W // C_HEADS
    ci = pl.program_id(1)

    @pl.when(ci == 0)
    def _():
        caug_s[...] = jnp.zeros(caug_s.shape, F32)
        m_s[...] = jnp.zeros(m_s.shape, F32)

    xcs, xconvs = [], []
    for b in range(nb):
        xc = xc_ref[b]
        halo = jnp.where(ci == 0, 0.0, halo_ref[b])
        xx = jnp.concatenate([halo, xc], axis=0)
        acc = cb_ref[...] + cw_ref[CONV_W - 1:CONV_W, :] * xc
        for k in range(1, CONV_W):
            acc = acc + cw_ref[CONV_W - 1 - k:CONV_W - k, :] * pltpu.roll(xx, k, 0)[8:]
        xcs.append(xc)
        xconvs.append(_silu(acc))
    xc_all = jnp.concatenate(xcs, axis=0)
    xconv_all = jnp.concatenate(xconvs, axis=0)
    tri = (lax.broadcasted_iota(jnp.int32, (L, L), 0) >= lax.broadcasted_iota(jnp.int32, (L, L), 1))
    one_col = (lax.broadcasted_iota(jnp.int32, (L, LANES), 1) == 0).astype(F32)
    ys = [[None] * C_HEADS for _ in range(nb)]
    for h in range(C_HEADS):
        sl = slice(h * hd, (h + 1) * hd)
        xh = xconv_all[:, sl]
        q_h = _mm(xh, wq_ref[h]) * np.float32(hd ** -0.5)
        kt_h = _mm_nt(wkt_ref[h], xh)
        v_h = _mm(xc_all[:, sl], wv_ref[h])
        for b in range(nb):
            r = slice(b * L, (b + 1) * L)
            gates = g_ref[b]
            hc, caug_new, m_new = _mlstm_head(
                q_h[r], kt_h[:, r], v_h[r], gates[h:h + 1, :], gates[C_HEADS + h:C_HEADS + h + 1, :],
                m_s[b, h][0:1, 0:1], caug_s[b, h], tri, one_col)
            caug_s[b, h] = caug_new
            m_s[b, h] = jnp.broadcast_to(m_new, (8, LANES))
            ys[b][h] = _mlstm_gate(hc, o_ref[b][:, sl], z_ref[b][:, sl], xconv_all[r, sl],
                                   ng_ref[:, sl], skip_ref[:, sl])
    y_all = jnp.concatenate([jnp.concatenate(ys[b], axis=1) for b in range(nb)], axis=0)
    hn = jnp.concatenate([h_ref[b] for b in range(nb)], axis=0) + _mm(y_all, wout_ref[...])
    if final:
        hn = hn * lax.rsqrt(jnp.mean(hn * hn, -1, keepdims=True) + EPS) * fg_ref[...]
    for b in range(nb):
        hout_ref[b] = hn[b * L:(b + 1) * L]

    @pl.when(ci == pl.num_programs(1) - 1)
    def _():
        for b in range(nb):
            for h in range(C_HEADS):
                c_out[0, b, h] = caug_s[b, h, :, :hd]
                n_out[b, h] = caug_s[b, h, :, hd:hd + 1]
                m_out[b, h:h + 1, :] = m_s[b, h][0:1, 0:1]


def _mlstm_prompt(y_in, gates, h, cw, cb, wq, wkt, wv, ng, skip, wout, final_g, c_stack, *, layer, n_layers, nb):
    batch, S = y_in.shape[0], y_in.shape[1]
    W, D = wout.shape
    assert batch % nb == 0 and S % C_CHUNK == 0
    hd = W // C_HEADS
    L = C_CHUNK
    const2 = lambda b, c: (0, 0)
    const3 = lambda b, c: (0, 0, 0)
    yblk = lambda k: pl.BlockSpec((nb, L, W), lambda b, c: (b, c, k))
    in_specs = [yblk(0),
                pl.BlockSpec((nb, 8, W), lambda b, c: (b, jnp.maximum(c * (L // 8) - 1, 0), 0)),
                yblk(1), yblk(2),
                pl.BlockSpec((nb, GATE_ROWS, L), lambda b, c: (b, 0, c)),
                pl.BlockSpec((nb, L, D), lambda b, c: (b, c, 0)),
                pl.BlockSpec(cw.shape, const2), pl.BlockSpec((1, W), const2),
                pl.BlockSpec(wq.shape, const3), pl.BlockSpec(wkt.shape, const3), pl.BlockSpec(wv.shape, const3),
                pl.BlockSpec((1, W), const2), pl.BlockSpec((1, W), const2), pl.BlockSpec(wout.shape, const2)]
    args = [y_in, y_in, y_in, y_in, gates, h, cw, cb.reshape(1, W), wq, wkt, wv,
            ng.reshape(1, W), skip.reshape(1, W), wout]
    if final_g is not None:
        in_specs.append(pl.BlockSpec((1, D), const2))
        args.append(final_g.reshape(1, D))
    aliases = {}
    if c_stack is not None:
        aliases[len(args)] = 1
        in_specs.append(pl.BlockSpec(memory_space=pl.ANY))
        args.append(c_stack)
    out_shape = [jax.ShapeDtypeStruct((batch, S, D), F32),
                 jax.ShapeDtypeStruct((n_layers, batch, C_HEADS, hd, hd), F32),
                 jax.ShapeDtypeStruct((batch, C_HEADS, hd, 1), F32),
                 jax.ShapeDtypeStruct((batch, C_HEADS, 1), F32)]
    out_specs = [pl.BlockSpec((nb, L, D), lambda b, c: (b, c, 0)),
                 pl.BlockSpec((1, nb, C_HEADS, hd, hd), lambda b, c: (layer, b, 0, 0, 0)),
                 pl.BlockSpec((nb, C_HEADS, hd, 1), lambda b, c: (b, 0, 0, 0)),
                 pl.BlockSpec((nb, C_HEADS, 1), lambda b, c: (b, 0, 0))]
    hout, c_new, n_new, m_new = pl.pallas_call(
        functools.partial(_mlstm_prompt_kernel, nb=nb, n_alias=len(aliases), final=final_g is not None),
        grid=(batch // nb, S // L), in_specs=in_specs, out_specs=out_specs, out_shape=out_shape,
        input_output_aliases=aliases,
        scratch_shapes=[pltpu.VMEM((nb, C_HEADS, hd, hd + LANES), F32), pltpu.VMEM((nb, C_HEADS, 8, LANES), F32)],
        compiler_params=_cparams("parallel", "arbitrary"), name="mlstm_layer",
    )(*args)
    return hout, c_new, n_new.reshape(batch, C_HEADS, hd), m_new.reshape(batch, C_HEADS)


def _out_proj_kernel(y_ref, w_ref, h_ref, *rest, final):
    hn = h_ref[...] + _mm(y_ref[...], w_ref[...])
    if final:
        fg_ref, o_ref = rest
        o_ref[...] = hn * lax.rsqrt(jnp.mean(hn * hn, -1, keepdims=True) + EPS) * fg_ref[...]
    else:
        rest[0][...] = hn


def _out_proj(y, w, h, final_g, *, tm):
    T, D = h.shape
    K = y.shape[1]
    const2 = lambda i: (0, 0)
    in_specs = [pl.BlockSpec((tm, K), lambda i: (i, 0)), pl.BlockSpec(w.shape, const2),
                pl.BlockSpec((tm, D), lambda i: (i, 0))]
    args = [y, w, h]
    if final_g is not None:
        in_specs.append(pl.BlockSpec((1, D), const2))
        args.append(final_g.reshape(1, D))
    return pl.pallas_call(
        functools.partial(_out_proj_kernel, final=final_g is not None),
        grid=(T // tm,), in_specs=in_specs, out_specs=pl.BlockSpec((tm, D), lambda i: (i, 0)),
        out_shape=jax.ShapeDtypeStruct((T, D), F32),
        compiler_params=_cparams("parallel"), name="out_proj",
    )(*args)


def kernel(x_prompt, x_sample, cache_k, cache_v, cache_logf, state_c, state_n, state_m, state_conv, page_table,
           norm_g, final_g, even_w_in, even_b_f, gmlp_ln_g, gmlp_ln_b, gmlp_w_s, gmlp_b_s, even_w_out,
           odd_w_in, odd_b_i, odd_b_f, conv_w, conv_b, mlstm_w_q, mlstm_w_k, mlstm_w_v, mlstm_norm_g,
           mlstm_skip, odd_w_out):
    B, S, D = x_prompt.shape
    DB = x_sample.shape[0]
    depth = norm_g.shape[0]
    AW = A_WIDTH
    BW = gmlp_ln_g.shape[1]
    CW = conv_w.shape[2]
    gdim = BW // B_GROUPS
    tm = 512
    assert AW == BW, "even-layer column blocks are addressed in units of one common width"

    hp = x_prompt.reshape(B * S, D)
    hs = x_sample.reshape(DB, D)
    cache_kt = jnp.transpose(cache_k, (0, 1, 3, 4, 2))
    cache_vt = jnp.transpose(cache_v, (0, 1, 3, 4, 2))
    cache_lft = jnp.transpose(cache_logf, (0, 1, 3, 2))

    lp, ks, vs, ls, chv = [], [], [], [], []
    np_, mp, bp, ns, ms, bs = [], [], [], [], [], []
    n_even, n_odd = (depth + 1) // 2, depth // 2
    assert depth % 2 == 0, "the final rmsnorm is fused into the last (mLSTM) layer's output projection"
    qkv_stack = cp_stack = cs_stack = None
    for l in range(depth):
        j = l // 2
        last = l == depth - 1
        if l % 2 == 0:
            w = even_w_in[j]
            o = 0
            parts = {}
            for name, n in (("q", AW), ("k", AW), ("v", AW), ("fg", A_HEADS), ("za", AW), ("u", BW), ("vb", BW), ("zb", BW)):
                parts[name] = w[:, o:o + n]
                o += n
            wnn_s = jnp.concatenate([parts[n] for n in ("za", "u", "vb", "zb", "k", "q", "v")], 1).astype(BF16)
            wqt = parts["q"].T.astype(BF16)
            wkt = parts["k"].T.astype(BF16)
            wvt = parts["v"].T.astype(BF16)
            wg = parts["fg"].T.astype(BF16)
            gb = even_b_f[j].reshape(GATE_ROWS, 1)
            wout = even_w_out[j].astype(BF16)
            bs_full = jnp.repeat(gmlp_b_s[j].T, gdim, axis=1)
            ws_dec = jnp.repeat(gmlp_w_s[j][:, 0, 0], gdim).reshape(1, BW)
            bs_dec = jnp.repeat(gmlp_b_s[j][:, 0], gdim).reshape(1, BW)

            ys, lf = _proj(hs, norm_g[l], wnn_s, [], wg, gb, n_lin=0, batch=1, seq=DB, tm=DB)
            lf_s = lf[0].T
            k_s, q_s, v_s = ys[:, 4 * AW:5 * AW], ys[:, 5 * AW:6 * AW], ys[:, 6 * AW:7 * AW]
            y, qt, kt_all, vt_all, lf = _proj(hp, norm_g[l], wnn_s, [wqt, wkt, wvt], wg, gb, n_lin=0, n_nn=5 * AW,
                                              batch=B, seq=S, tm=tm, layer=j, n_layers=n_even, stacked=qkv_stack)
            qkv_stack = [qt, kt_all, vt_all]
            c = _seq_cumsum(lf)

            att, att_s = _fox_attention(y, qt, vt_all, c, q_s, k_s, v_s, lf_s, cache_kt, cache_vt, cache_lft,
                                        page_table, k_col=4 * AW, layer=j, batch=B, seq=S, tile=512,
                                        group_pages=8)
            hp = _even_mix(att, y, hp, gmlp_ln_g[j], gmlp_ln_b[j], gmlp_w_s[j], bs_full, wout,
                           col0=0, tm=tm, decode=False)[0]
            lp.append(jnp.transpose(lf, (0, 2, 1)))

            hs, vn_s = _even_mix(att_s, ys, hs, gmlp_ln_g[j], gmlp_ln_b[j], ws_dec, bs_dec, wout,
                                 col0=0, tm=DB, decode=True)
            ks.append(k_s.reshape(DB, 1, A_HEADS, A_HEAD_DIM))
            vs.append(v_s.reshape(DB, 1, A_HEADS, A_HEAD_DIM))
            ls.append(lf_s.reshape(DB, 1, A_HEADS))
            chv.append(vn_s.reshape(DB, 1, BW))
        else:
            w = odd_w_in[j]
            wnn = w[:, :3 * CW].astype(BF16)
            wg = w[:, 3 * CW:].T.astype(BF16)
            gb = jnp.concatenate([odd_b_i[j], odd_b_f[j]]).reshape(GATE_ROWS, 1)
            wq = mlstm_w_q[j].astype(BF16)
            wk = mlstm_w_k[j].astype(BF16)
            wkT = jnp.swapaxes(mlstm_w_k[j], 1, 2).astype(BF16)
            wv = mlstm_w_v[j].astype(BF16)
            wout = odd_w_out[j].astype(BF16)
            fin = final_g if last else None

            y, g = _proj(hp, norm_g[l], wnn, [], wg, gb, n_lin=C_HEADS, batch=B, seq=S, tm=tm, cum_chunk=C_CHUNK)
            hout, cp_stack, n1, m1 = _mlstm_prompt(
                y.reshape(B, S, 3 * CW), g, hp.reshape(B, S, D), conv_w[j], conv_b[j], wq, wkT, wv,
                mlstm_norm_g[j], mlstm_skip[j], wout, fin, cp_stack, layer=j, n_layers=n_odd,
                nb=max(n for n in (4, 2, 1) if B % n == 0))
            hp = hout.reshape(B * S, D)
            np_.append(n1); mp.append(m1)
            bp.append(y.reshape(B, S, 3 * CW)[:, S - (CONV_W - 1):, :CW])

            y, g = _proj(hs, norm_g[l], wnn, [], wg, gb, n_lin=C_HEADS, batch=1, seq=DB, tm=DB)
            buf = jnp.transpose(state_conv[j], (1, 0, 2))
            xconv, q, k, v, nbuf = _qkv_decode(y, buf, conv_w[j], conv_b[j], wq, wk, wv)
            g_s = jnp.transpose(g, (2, 1, 0))
            r3 = lambda a: a.reshape(DB, 1, a.shape[1])
            ym, cs_stack, n2, m2 = _mlstm_decode(r3(q), r3(k), r3(v), g_s, r3(y), r3(xconv), mlstm_norm_g[j],
                                                 mlstm_skip[j], (state_c, state_n, state_m), cs_stack,
                                                 layer=j, n_layers=n_odd, nb=4)
            hs = _out_proj(ym.reshape(DB, CW), wout, hs, fin, tm=DB)
            ns.append(n2); ms.append(m2)
            bs.append(jnp.transpose(nbuf, (1, 0, 2)))
    y_prompt = hp.reshape(B, S, D)
    y_sample = hs.reshape(DB, 1, D)
    to_bshd = lambda t: jnp.transpose(t.reshape(n_even, B, A_HEADS, A_HEAD_DIM, S), (0, 1, 4, 2, 3))
    return (y_prompt, y_sample,
            to_bshd(qkv_stack[1]), to_bshd(qkv_stack[2]), jnp.stack(lp),
            jnp.stack(ks), jnp.stack(vs), jnp.stack(ls), jnp.stack(chv),
            cp_stack, jnp.stack(np_), jnp.stack(mp), jnp.stack(bp),
            cs_stack, jnp.stack(ns), jnp.stack(ms), jnp.stack(bs))
```

```python
import functools

import jax
import jax.numpy as jnp
import numpy as np
from jax import lax
from jax.experimental import pallas as pl
from jax.experimental.pallas import tpu as pltpu

F32 = jnp.float32
BF16 = jnp.bfloat16
EPS = 1e-6
NEG_INF = float("-inf")
LOG2E = 1.4426950408889634

A_HEADS = 8
A_HEAD_DIM = 64
A_WIDTH = A_HEADS * A_HEAD_DIM
B_GROUPS = 8
B_CHUNK = 128
C_HEADS = 4
C_CHUNK = 128
CONV_W = 4
LANES = 128
GATE_ROWS = 8
VMEM_LIMIT = 56 * 1024 * 1024


def _cparams(*sem):
    return pltpu.CompilerParams(dimension_semantics=sem, vmem_limit_bytes=VMEM_LIMIT)


def _mm(a, b):
    return jnp.dot(a.astype(BF16), b.astype(BF16), preferred_element_type=F32)


def _mm_nt(a, b):
    return lax.dot_general(a.astype(BF16), b.astype(BF16), (((1,), (1,)), ((), ())),
                           preferred_element_type=F32)


def _log_sigmoid(x):
    return jnp.minimum(x, 0.0) - jnp.log1p(jnp.exp(-jnp.abs(x)))


def _sigmoid(x):
    return 1.0 / (1.0 + jnp.exp(-x))


def _silu(x):
    return x * _sigmoid(x)


def _gelu(x):
    return 0.5 * x * (1.0 + lax.erf(x * np.float32(np.sqrt(0.5))))


def _row_to_col(r):
    n = r.shape[1]
    eye = lax.broadcasted_iota(jnp.int32, (n, n), 0) == lax.broadcasted_iota(jnp.int32, (n, n), 1)
    return jnp.sum(jnp.where(eye, r, 0.0), axis=1, keepdims=True)


def _proj_kernel(x_ref, g_ref, wnn_ref, wg_ref, gb_ref, *rest, n_nt, n_alias, n_lin, col_chunk, cum_chunk):
    wnt_refs = rest[:n_nt]
    rest = rest[n_nt + n_alias:]
    y_ref = rest[0]
    yt_refs = rest[1:1 + n_nt]
    gt_ref = rest[1 + n_nt]
    x = x_ref[...]
    xn = x * lax.rsqrt(jnp.mean(x * x, -1, keepdims=True) + EPS) * g_ref[...]
    xb = xn.astype(BF16)
    n_nn = y_ref.shape[1]
    for c in range(0, n_nn, col_chunk):
        y_ref[:, c:c + col_chunk] = jnp.dot(xb, wnn_ref[:, c:c + col_chunk], preferred_element_type=F32)
    for w_ref, o_ref in zip(wnt_refs, yt_refs):
        o_ref[0, 0] = _mm_nt(w_ref[...], xb)
    gt = _mm_nt(wg_ref[...], xb) + gb_ref[...]
    row = lax.broadcasted_iota(jnp.int32, gt.shape, 0)
    gt = jnp.where(row >= n_lin, _log_sigmoid(gt), gt)
    if cum_chunk is None:
        gt_ref[0] = gt
    else:
        upto = (lax.broadcasted_iota(jnp.int32, (cum_chunk, cum_chunk), 0)
                <= lax.broadcasted_iota(jnp.int32, (cum_chunk, cum_chunk), 1)).astype(F32)
        rowc = row[:, :cum_chunk]
        for c in range(0, gt.shape[1], cum_chunk):
            blk = gt[:, c:c + cum_chunk]
            cs = jnp.dot(blk, upto, precision=lax.Precision.HIGHEST, preferred_element_type=F32)
            gt_ref[0, :, c:c + cum_chunk] = jnp.where(rowc >= n_lin, cs, blk)


def _proj(x, g, wnn, wnts, wg, gb, *, n_lin, batch, seq, tm, n_nn=None, cum_chunk=None, layer=0, n_layers=1,
          stacked=None):
    T, D = x.shape
    tps = seq // tm
    n_nn = wnn.shape[1] if n_nn is None else n_nn
    n_nt = len(wnts)
    const = lambda i: (0, 0)
    tok_t = lambda i: (i // tps, 0, i % tps)
    in_specs = [pl.BlockSpec((tm, D), lambda i: (i, 0)),
                pl.BlockSpec((1, D), const),
                pl.BlockSpec((D, n_nn), const),
                pl.BlockSpec((GATE_ROWS, D), const),
                pl.BlockSpec((GATE_ROWS, 1), const)]
    in_specs += [pl.BlockSpec(w.shape, const) for w in wnts]
    args = [x, g.reshape(1, D), wnn, wg, gb, *wnts]
    aliases = {}
    if stacked is not None:
        for k, buf in enumerate(stacked):
            aliases[len(args)] = 1 + k
            in_specs.append(pl.BlockSpec(memory_space=pl.ANY))
            args.append(buf)
    out_shape = [jax.ShapeDtypeStruct((T, n_nn), F32)]
    out_specs = [pl.BlockSpec((tm, n_nn), lambda i: (i, 0))]
    for w in wnts:
        out_shape.append(jax.ShapeDtypeStruct((n_layers, batch, w.shape[0], seq), F32))
        out_specs.append(pl.BlockSpec((1, 1, w.shape[0], tm), lambda i: (layer, i // tps, 0, i % tps)))
    out_shape.append(jax.ShapeDtypeStruct((batch, GATE_ROWS, seq), F32))
    out_specs.append(pl.BlockSpec((1, GATE_ROWS, tm), tok_t))
    return pl.pallas_call(
        functools.partial(_proj_kernel, n_nt=n_nt, n_alias=len(aliases), n_lin=n_lin, col_chunk=512,
                          cum_chunk=cum_chunk),
        grid=(T // tm,), in_specs=in_specs, out_specs=out_specs, out_shape=out_shape,
        input_output_aliases=aliases,
        compiler_params=_cparams("parallel"), name="norm_proj",
    )(*args)


def _cumsum_kernel(x_ref, row_ref, col_ref):
    S = x_ref.shape[2]
    upto = (lax.broadcasted_iota(jnp.int32, (LANES, LANES), 0)
            <= lax.broadcasted_iota(jnp.int32, (LANES, LANES), 1)).astype(F32)
    carry = jnp.zeros((GATE_ROWS, 1), F32)
    for c in range(0, S, LANES):
        inc = jnp.dot(x_ref[0, :, c:c + LANES], upto, precision=lax.Precision.HIGHEST,
                      preferred_element_type=F32) + carry
        carry = inc[:, LANES - 1:LANES]
        inc2 = inc * np.float32(LOG2E)
        inc2_t = jnp.concatenate([inc2, jnp.zeros((LANES - GATE_ROWS, LANES), F32)], axis=0).T
        for p in range(GATE_ROWS // 2):
            row_ref[0, p, :, c:c + LANES] = inc2[2 * p:2 * p + 2, :]
            col_ref[0, p, c:c + LANES, :] = inc2_t[:, 2 * p:2 * p + 2]


def _seq_cumsum(x):
    B, R, S = x.shape
    return pl.pallas_call(_cumsum_kernel, grid=(B,),
                          in_specs=[pl.BlockSpec((1, R, S), lambda b: (b, 0, 0))],
                          out_specs=[pl.BlockSpec((1, R // 2, 2, S), lambda b: (b, 0, 0, 0)),
                                     pl.BlockSpec((1, R // 2, S, 2), lambda b: (b, 0, 0, 0))],
                          out_shape=[jax.ShapeDtypeStruct((B, R // 2, 2, S), F32),
                                     jax.ShapeDtypeStruct((B, R // 2, S, 2), F32)],
                          compiler_params=_cparams("parallel"), name="logf_cumsum")(x)


def _fox_kernel(qi_ref, kj_ref, pt_ref, qt_ref, k_ref, vt_ref, crow_ref, ccol_ref,
                dq_ref, dkn_ref, dvn_ref, dlf_ref, kc_hbm, vc_hbm, lfc_hbm, o_ref, do_ref,
                q_s, m_s, l_s, a_s, *decode_scratch, tile, layer, group_pages):
    qi = qi_ref[pl.program_id(2)]
    kj = kj_ref[pl.program_id(2)]
    hd = A_HEAD_DIM
    step = (pl.program_id(0) * pl.num_programs(1) + pl.program_id(1)) * pl.num_programs(2) + pl.program_id(2)
    seq_groups = pt_ref.shape[1] // group_pages
    n_groups = pt_ref.shape[0] * seq_groups
    start_group, wait_group, init_seq, compute_group, finish_seq = _paged_decode_ops(
        pt_ref, dq_ref, dkn_ref, dvn_ref, dlf_ref, kc_hbm, vc_hbm, lfc_hbm, do_ref, *decode_scratch,
        layer=layer, group_pages=group_pages)
    n_slots = decode_scratch[0].shape[0]
    ahead = n_slots - 1
    slot = step % n_slots

    @pl.when(step == 0)
    def _():
        for g0 in range(ahead):
            start_group(g0 // seq_groups, g0 % seq_groups, g0)

    @pl.when(step + ahead < n_groups)
    def _():
        start_group((step + ahead) // seq_groups, (step + ahead) % seq_groups, (step + ahead) % n_slots)

    @pl.when(kj == 0)
    def _():
        qt = qt_ref[0, 0] * np.float32(hd ** -0.5 * LOG2E)
        row = lax.broadcasted_iota(jnp.int32, qt.shape, 0)
        q_s[0] = jnp.where(row < hd, qt, 0.0).astype(BF16)
        q_s[1] = jnp.where(row >= hd, qt, 0.0).astype(BF16)
        m_s[...] = jnp.full(m_s.shape, NEG_INF, F32)
        l_s[...] = jnp.zeros(l_s.shape, F32)
        a_s[...] = jnp.zeros(a_s.shape, F32)

    def attend(diagonal):
        kb = k_ref[...].astype(BF16)
        cq = crow_ref[0, 0]
        ck = ccol_ref[0, 0]
        ones = jnp.ones((16, tile), BF16)
        if diagonal:
            causal = (lax.broadcasted_iota(jnp.int32, (tile, tile), 0)
                      <= lax.broadcasted_iota(jnp.int32, (tile, tile), 1))
        for hh in range(2):
            rows = slice(hh * hd, (hh + 1) * hd)
            s = jnp.dot(kb, q_s[hh], preferred_element_type=F32)
            s = s + cq[hh:hh + 1, :] - ck[:, hh:hh + 1]
            if diagonal:
                s = jnp.where(causal, s, NEG_INF)
            m_old = m_s[hh]
            m_new = jnp.maximum(m_old, jnp.max(s, 0, keepdims=True))
            alpha = jnp.exp2(m_old - m_new)
            p = jnp.exp2(s - m_new).astype(BF16)
            vt = jnp.concatenate([vt_ref[0, 0, rows, :].astype(BF16), ones], axis=0)
            pv = jnp.dot(vt, p, preferred_element_type=F32)
            a_s[rows, :] = alpha * a_s[rows, :] + pv[:hd]
            l_s[hh] = alpha * l_s[hh] + pv[hd:hd + 1]
            m_s[hh] = m_new

    def prompt_tile(diagonal):
        attend(diagonal)
        if diagonal:
            row = lax.broadcasted_iota(jnp.int32, (2 * hd, tile), 0)
            out_t = a_s[...] / jnp.where(row < hd, l_s[0], l_s[1])
            o_ref[...] = out_t.T

    has_group = step < n_groups

    @pl.when(has_group & (step % seq_groups == 0))
    def _():
        init_seq()

    for diagonal in (False, True):
        on_tile = (kj == qi) if diagonal else (kj < qi)

        @pl.when(on_tile & has_group)
        def _():
            wait_group(slot)
            compute_group(slot)
            prompt_tile(diagonal)

        @pl.when(on_tile & jnp.logical_not(has_group))
        def _():
            prompt_tile(diagonal)

    @pl.when(has_group & (step % seq_groups == seq_groups - 1))
    def _():
        finish_seq()


def _fox_attention(y, qt, vt, c, q, k_new, v_new, lf_new, cache_kt, cache_vt, cache_lft, page_table, *,
                   k_col, layer, batch, seq, tile, group_pages):
    T = y.shape[0]
    DB, n_pages = page_table.shape
    pairs = A_HEADS // 2
    nt = seq // tile
    kb0 = k_col // LANES
    crow, ccol = c
    tri = [(i, j) for i in range(nt) for j in range(i + 1)]
    qi_tab = jnp.asarray([i for i, _ in tri], jnp.int32)
    kj_tab = jnp.asarray([j for _, j in tri], jnp.int32)
    n_tri = len(tri)
    P = group_pages
    n_slots = 3
    seq_groups = n_pages // P
    assert n_pages % P == 0 and n_slots - 1 <= DB * seq_groups <= batch * pairs * n_tri, \
        "one page group per grid step"
    col = lambda a: a.reshape(DB, a.shape[1], 1)
    dec_seq = lambda b, p, t: jnp.minimum(((b * pairs + p) * n_tri + t) // seq_groups, DB - 1)
    vec = lambda n: pl.BlockSpec((1, n, 1), lambda b, p, t, qi, kj, pt: (dec_seq(b, p, t), 0, 0))
    hbm = pl.BlockSpec(memory_space=pl.ANY)
    page = (A_HEADS, A_HEAD_DIM, LANES)
    grid_spec = pltpu.PrefetchScalarGridSpec(
        num_scalar_prefetch=3, grid=(batch, pairs, n_tri),
        in_specs=[pl.BlockSpec((1, 1, LANES, tile), lambda b, p, t, qi, kj, pt: (layer, b, p, qi[t])),
                  pl.BlockSpec((tile, LANES), lambda b, p, t, qi, kj, pt: (b * nt + kj[t], kb0 + p)),
                  pl.BlockSpec((1, 1, LANES, tile), lambda b, p, t, qi, kj, pt: (layer, b, p, kj[t])),
                  pl.BlockSpec((1, 1, 2, tile), lambda b, p, t, qi, kj, pt: (b, p, 0, qi[t])),
                  pl.BlockSpec((1, 1, tile, 2), lambda b, p, t, qi, kj, pt: (b, p, kj[t], 0)),
                  vec(A_WIDTH), vec(A_WIDTH), vec(A_WIDTH), vec(A_HEADS), hbm, hbm, hbm],
        out_specs=[pl.BlockSpec((tile, LANES), lambda b, p, t, qi, kj, pt: (b * nt + qi[t], p)),
                   vec(A_WIDTH)],
        scratch_shapes=[pltpu.VMEM((2, LANES, tile), BF16), pltpu.VMEM((2, 1, tile), F32),
                        pltpu.VMEM((2, 1, tile), F32), pltpu.VMEM((LANES, tile), F32),
                        pltpu.VMEM((n_slots, P) + page, F32), pltpu.VMEM((n_slots, P) + page, F32),
                        pltpu.VMEM((n_slots, P, A_HEADS, LANES), F32),
                        pltpu.SemaphoreType.DMA((3, n_slots)),
                        pltpu.VMEM((A_HEADS, 1), F32), pltpu.VMEM((A_HEADS, 1), F32),
                        pltpu.VMEM((A_HEADS, 1), F32),
                        pltpu.VMEM(page, F32), pltpu.VMEM(page, F32)])
    att, att_dec = pl.pallas_call(
        functools.partial(_fox_kernel, tile=tile, layer=layer, group_pages=P),
        grid_spec=grid_spec,
        out_shape=[jax.ShapeDtypeStruct((T, A_WIDTH), F32), jax.ShapeDtypeStruct((DB, A_WIDTH, 1), F32)],
        compiler_params=_cparams("arbitrary", "arbitrary", "arbitrary"),
        name="fox_attention",
    )(qi_tab, kj_tab, page_table, qt, y, vt, crow, ccol, col(q), col(k_new), col(v_new), col(lf_new),
      cache_kt, cache_vt, cache_lft)
    return att, att_dec.reshape(DB, A_WIDTH)


def _paged_decode_ops(pt_ref, q_ref, kn_ref, vn_ref, lfn_ref, kc_hbm, vc_hbm, lfc_hbm, o_ref,
                      kbuf, vbuf, lfbuf, sem, m_s, l_s, c_s, a_s, q_s, *, layer, group_pages):
    P = group_pages
    n_pages = pt_ref.shape[1]
    H, hd = A_HEADS, A_HEAD_DIM
    hsl = [slice(h * hd, (h + 1) * hd) for h in range(H)]

    def group_copies(slot, page_of):
        out = []
        for i in range(P):
            pid = page_of(i)
            out.append(pltpu.make_async_copy(kc_hbm.at[layer, pid], kbuf.at[slot, i], sem.at[0, slot]))
            out.append(pltpu.make_async_copy(vc_hbm.at[layer, pid], vbuf.at[slot, i], sem.at[1, slot]))
            out.append(pltpu.make_async_copy(lfc_hbm.at[layer, pid], lfbuf.at[slot, i], sem.at[2, slot]))
        return out

    def start_group(seq, g, slot):
        for cp in group_copies(slot, lambda i: pt_ref[seq, n_pages - 1 - (g * P + i)]):
            cp.start()

    def wait_group(slot):
        for cp in group_copies(slot, lambda i: 0):
            cp.wait()

    later = (lax.broadcasted_iota(jnp.int32, (LANES, LANES), 0)
             > lax.broadcasted_iota(jnp.int32, (LANES, LANES), 1)).astype(F32)
    lane = lax.broadcasted_iota(jnp.int32, (hd, LANES), 1)

    def init_seq():
        qcols = [q_ref[0, hsl[h], :] * np.float32(hd ** -0.5) for h in range(H)]
        for h in range(H):
            q_s[h] = jnp.broadcast_to(qcols[h], (hd, LANES))
        m_s[...] = jnp.concatenate(
            [jnp.sum(qcols[h] * kn_ref[0, hsl[h], :], axis=0, keepdims=True) for h in range(H)], axis=0)
        l_s[...] = jnp.ones(l_s.shape, F32)
        c_s[...] = lfn_ref[0]
        for h in range(H):
            a_s[h] = jnp.where(lane == 0, vn_ref[0, hsl[h], :], 0.0)

    def compute_group(slot):
        c = c_s[...]
        lf_all = jnp.concatenate([lfbuf[slot, i] for i in range(P)], axis=0)
        excl_all = jnp.dot(lf_all, later, precision=lax.Precision.HIGHEST, preferred_element_type=F32)
        tot_all = jnp.sum(lf_all, -1, keepdims=True)
        s_pages = []
        for i in range(P):
            rows = [jnp.sum(kbuf[slot, i, h] * q_s[h], axis=0, keepdims=True) for h in range(H)]
            s_pages.append(jnp.concatenate(rows, axis=0) + (c + excl_all[i * H:(i + 1) * H, :]))
            c = c + tot_all[i * H:(i + 1) * H, :]
        c_s[...] = c
        s_all = jnp.concatenate(s_pages, axis=1)
        m_old = m_s[...]
        m_new = jnp.maximum(m_old, jnp.max(s_all, -1, keepdims=True))
        alpha = jnp.exp(m_old - m_new)
        p_all = jnp.exp(s_all - m_new)
        l_s[...] = alpha * l_s[...] + jnp.sum(p_all, -1, keepdims=True)
        m_s[...] = m_new
        for h in range(H):
            acc = a_s[h] * alpha[h:h + 1, :]
            for i in range(P):
                acc = acc + p_all[h:h + 1, i * LANES:(i + 1) * LANES] * vbuf[slot, i, h]
            a_s[h] = acc

    def finish_seq():
        l = l_s[...]
        for h in range(H):
            o_ref[0, hsl[h], :] = jnp.sum(a_s[h], axis=1, keepdims=True) / l[h:h + 1, :]

    return start_group, wait_group, init_seq, compute_group, finish_seq


def _even_mix_kernel(att_ref, za_ref, u_ref, vb_ref, zb_ref, h_ref, lng_ref, lnb_ref, ws_ref, bs_ref,
                     wout_ref, *outs, decode):
    o_ref = outs[0]
    ya = att_ref[...] * _silu(za_ref[...])
    u = _gelu(u_ref[...])
    vf = _gelu(vb_ref[...])
    mu = jnp.mean(vf, -1, keepdims=True)
    var = jnp.mean((vf - mu) ** 2, -1, keepdims=True)
    vn = (vf - mu) * lax.rsqrt(var + EPS) * lng_ref[...] + lnb_ref[...]
    tm, bw = vn.shape
    if decode:
        outs[1][...] = vn
        mix = vn * ws_ref[...] + bs_ref[...]
    else:
        lane = lax.broadcasted_iota(jnp.int32, (B_CHUNK, LANES), 1)
        tri = (lax.broadcasted_iota(jnp.int32, (B_CHUNK, B_CHUNK), 0)
               >= lax.broadcasted_iota(jnp.int32, (B_CHUNK, B_CHUNK), 1))
        wtril = [jnp.where(tri, ws_ref[g], 0.0).astype(BF16) for g in range(B_GROUPS)]
        gpl = LANES // (bw // B_GROUPS)
        rows = []
        for c in range(0, tm, B_CHUNK):
            blocks = []
            for lb in range(bw // LANES):
                vblk = vn[c:c + B_CHUNK, lb * LANES:(lb + 1) * LANES].astype(BF16)
                y0 = jnp.dot(wtril[lb * gpl], vblk, preferred_element_type=F32)
                y1 = jnp.dot(wtril[lb * gpl + 1], vblk, preferred_element_type=F32)
                blocks.append(jnp.where(lane < LANES // gpl, y0, y1))
            rows.append(jnp.concatenate(blocks, axis=1) + bs_ref[...])
        mix = jnp.concatenate(rows, axis=0)
    yb = u * mix * _silu(zb_ref[...])
    aw = ya.shape[1]
    hn = h_ref[...] + _mm(ya, wout_ref[:aw, :]) + _mm(yb, wout_ref[aw:, :])
    o_ref[...] = hn


def _even_mix(att, y, h, ln_g, ln_b, ws, bs, wout, *, col0, tm, decode):
    T, D = h.shape
    aw = att.shape[1]
    const2 = lambda i: (0, 0)
    yblk = lambda k: pl.BlockSpec((tm, aw), lambda i: (i, col0 + k))
    ws_spec = (pl.BlockSpec(ws.shape, const2) if decode else pl.BlockSpec(ws.shape, lambda i: (0, 0, 0)))
    in_specs = [pl.BlockSpec((tm, aw), lambda i: (i, 0)), yblk(0), yblk(1), yblk(2), yblk(3),
                pl.BlockSpec((tm, D), lambda i: (i, 0)),
                pl.BlockSpec((1, aw), const2), pl.BlockSpec((1, aw), const2),
                ws_spec, pl.BlockSpec(bs.shape, const2), pl.BlockSpec(wout.shape, const2)]
    out_shape = [jax.ShapeDtypeStruct((T, D), F32)]
    out_specs = [pl.BlockSpec((tm, D), lambda i: (i, 0))]
    if decode:
        out_shape.append(jax.ShapeDtypeStruct((T, aw), F32))
        out_specs.append(pl.BlockSpec((tm, aw), lambda i: (i, 0)))
    return pl.pallas_call(
        functools.partial(_even_mix_kernel, decode=decode),
        grid=(T // tm,), in_specs=in_specs, out_specs=out_specs, out_shape=out_shape,
        compiler_params=_cparams("parallel"), name="even_mix",
    )(att, y, y, y, y, h, ln_g.reshape(1, aw), ln_b.reshape(1, aw), ws, bs, wout)


def _qkv_decode_kernel(xc_ref, prev_ref, cw_ref, cb_ref, wq_ref, wk_ref, wv_ref,
                       xconv_ref, q_ref, k_ref, v_ref, nb_ref):
    xc = xc_ref[...]
    hd = xc.shape[1] // C_HEADS
    acc = cb_ref[...] + cw_ref[CONV_W - 1:CONV_W, :] * xc
    for j in range(CONV_W - 1):
        acc = acc + cw_ref[j:j + 1, :] * prev_ref[j]
    for j in range(CONV_W - 2):
        nb_ref[j] = prev_ref[j + 1]
    nb_ref[CONV_W - 2] = xc
    xconv = _silu(acc)
    xconv_ref[...] = xconv
    for h in range(C_HEADS):
        sl = slice(h * hd, (h + 1) * hd)
        q_ref[:, sl] = _mm(xconv[:, sl], wq_ref[h]) * np.float32(hd ** -0.5)
        k_ref[:, sl] = _mm(xconv[:, sl], wk_ref[h])
        v_ref[:, sl] = _mm(xc[:, sl], wv_ref[h])


def _qkv_decode(y, prev, cw, cb, wq, wk, wv):
    T = y.shape[0]
    W = cw.shape[1]
    const2 = lambda i: (0, 0)
    const3 = lambda i: (0, 0, 0)
    row_blk = pl.BlockSpec((T, W), lambda i: (0, 0))
    row_out = jax.ShapeDtypeStruct((T, W), F32)
    return pl.pallas_call(
        _qkv_decode_kernel, grid=(1,),
        in_specs=[row_blk, pl.BlockSpec(prev.shape, const3), pl.BlockSpec(cw.shape, const2),
                  pl.BlockSpec((1, W), const2), pl.BlockSpec(wq.shape, const3), pl.BlockSpec(wk.shape, const3),
                  pl.BlockSpec(wv.shape, const3)],
        out_specs=[row_blk, row_blk, row_blk, row_blk, pl.BlockSpec(prev.shape, const3)],
        out_shape=[row_out, row_out, row_out, row_out, jax.ShapeDtypeStruct(prev.shape, F32)],
        compiler_params=_cparams("arbitrary"), name="conv_qkv_decode",
    )(y, prev, cw, cb.reshape(1, W), wq, wk, wv)


def _mlstm_decode_kernel(q_ref, k_ref, v_ref, g_ref, o_ref, z_ref, xconv_ref, ng_ref, skip_ref,
                         c0_ref, n0_ref, m0_ref, *rest, nb, n_alias):
    y_ref, c_out, n_out, m_out = rest[n_alias:]
    W = q_ref.shape[2]
    hd = W // C_HEADS
    first = lax.broadcasted_iota(jnp.int32, (8, hd), 0) == 0
    pad8 = lambda r: jnp.where(first, jnp.broadcast_to(r, (8, hd)), 0.0)
    for b in range(nb):
        ys = []
        for h in range(C_HEADS):
            sl = slice(h * hd, (h + 1) * hd)
            q, k, v = q_ref[b, :, sl], k_ref[b, :, sl], v_ref[b, :, sl]
            log_i, log_f = g_ref[b, h:h + 1, :], g_ref[b, C_HEADS + h:C_HEADS + h + 1, :]
            c0, n0, m0 = c0_ref[0, b, h], n0_ref[0, b, h:h + 1, :], m0_ref[0, b, h:h + 1, :]
            m_new = jnp.maximum(log_f + m0, log_i)
            a = jnp.exp(log_f + m0 - m_new)
            w = jnp.exp(log_i - m_new)
            sm = w * jnp.sum(q * k, -1, keepdims=True)
            num = sm * v + a * _mm(pad8(q), c0)[0:1, :]
            den = sm + a * jnp.sum(q * n0, -1, keepdims=True)
            hc = num / jnp.maximum(jnp.abs(den), jnp.exp(-m_new))
            ktv = lax.dot_general(pad8(k).astype(BF16), pad8(v).astype(BF16), (((0,), (0,)), ((), ())),
                                  preferred_element_type=F32)
            c_out[0, b, h] = a * c0 + w * ktv
            n_out[b, h:h + 1, :] = a * n0 + w * k
            m_out[b, h:h + 1, :] = m_new
            ys.append(_mlstm_gate(hc, o_ref[b, :, sl], z_ref[b, :, sl], xconv_ref[b, :, sl],
                                  ng_ref[:, sl], skip_ref[:, sl]))
        y_ref[b] = jnp.concatenate(ys, axis=1)


def _mlstm_decode(q, k, v, gates, y_in, xconv, ng, skip, state, c_stack, *, layer, n_layers, nb):
    batch, W = q.shape[0], q.shape[2]
    hd = W // C_HEADS
    const2 = lambda b: (0, 0)
    rowblk = lambda k: pl.BlockSpec((nb, 1, W), lambda b: (b, 0, k))
    c0, n0, m0 = state
    in_specs = [rowblk(0), rowblk(0), rowblk(0),
                pl.BlockSpec((nb, GATE_ROWS, 1), lambda b: (b, 0, 0)),
                rowblk(2), rowblk(1), rowblk(0),
                pl.BlockSpec((1, W), const2), pl.BlockSpec((1, W), const2),
                pl.BlockSpec((1, nb, C_HEADS, hd, hd), lambda b: (layer, b, 0, 0, 0)),
                pl.BlockSpec((1, nb, C_HEADS, hd), lambda b: (layer, b, 0, 0)),
                pl.BlockSpec((1, nb, C_HEADS, 1), lambda b: (layer, b, 0, 0))]
    args = [q, k, v, gates, y_in, y_in, xconv, ng.reshape(1, W), skip.reshape(1, W),
            c0, n0, m0.reshape(m0.shape + (1,))]
    aliases = {}
    if c_stack is not None:
        aliases[len(args)] = 1
        in_specs.append(pl.BlockSpec(memory_space=pl.ANY))
        args.append(c_stack)
    out_shape = [jax.ShapeDtypeStruct((batch, 1, W), F32),
                 jax.ShapeDtypeStruct((n_layers, batch, C_HEADS, hd, hd), F32),
                 jax.ShapeDtypeStruct((batch, C_HEADS, hd), F32),
                 jax.ShapeDtypeStruct((batch, C_HEADS, 1), F32)]
    out_specs = [pl.BlockSpec((nb, 1, W), lambda b: (b, 0, 0)),
                 pl.BlockSpec((1, nb, C_HEADS, hd, hd), lambda b: (layer, b, 0, 0, 0)),
                 pl.BlockSpec((nb, C_HEADS, hd), lambda b: (b, 0, 0)),
                 pl.BlockSpec((nb, C_HEADS, 1), lambda b: (b, 0, 0))]
    y, c_new, n_new, m_new = pl.pallas_call(
        functools.partial(_mlstm_decode_kernel, nb=nb, n_alias=len(aliases)),
        grid=(batch // nb,), in_specs=in_specs, out_specs=out_specs, out_shape=out_shape,
        input_output_aliases=aliases,
        compiler_params=_cparams("parallel"), name="mlstm_decode",
    )(*args)
    return y, c_new, n_new, m_new.reshape(batch, C_HEADS)


def _mlstm_head(qh, kth, vh, i_row, b_row, m_prev, caug, tri, one_col):
    L, hd = qh.shape
    g_row = i_row - b_row
    dm = jnp.where(tri, g_row, NEG_INF)
    mcol = jnp.maximum(m_prev, jnp.max(dm, -1, keepdims=True))
    wmat = jnp.exp(dm - mcol)
    a = jnp.exp(m_prev - mcol)
    sm = wmat * _mm(qh, kth)
    qc = _mm(qh, caug)
    num = _mm(sm, vh) + a * qc[:, :hd]
    den = jnp.sum(sm, -1, keepdims=True) + a * qc[:, hd:hd + 1]
    den = jnp.maximum(jnp.abs(den), jnp.exp(-(_row_to_col(b_row) + mcol)))
    m_last = mcol[L - 1:L, :]
    wl = jnp.exp(g_row - m_last)
    a_l = jnp.exp(m_prev - m_last)
    caug_new = a_l * caug + _mm(kth * wl, jnp.concatenate([vh, one_col], axis=1))
    return num / den, caug_new, b_row[:, L - 1:L] + m_last


def _mlstm_gate(hc, o, z, xconv, ng, skip):
    mu = jnp.mean(hc, -1, keepdims=True)
    var = jnp.mean((hc - mu) ** 2, -1, keepdims=True)
    hn = (hc - mu) * lax.rsqrt(var + EPS) * ng
    return (_sigmoid(o) * hn + skip * xconv) * _silu(z)


def _mlstm_prompt_kernel(xc_ref, halo_ref, z_ref, o_ref, g_ref, h_ref, cw_ref, cb_ref, wq_ref, wkt_ref, wv_ref,
                         ng_ref, skip_ref, wout_ref, *rest, nb, n_alias, final):
    if final:
        fg_ref = rest[0]
        rest = rest[1:]
    hout_ref, c_out, n_out, m_out, caug_s, m_s = rest[n_alias:]
    L, W = xc_ref.shape[1], xc_ref.shape[2]
    hd = W // C_HEADS
    ci = pl.program_id(1)

    @pl.when(ci == 0)
    def _():
        caug_s[...] = jnp.zeros(caug_s.shape, F32)
        m_s[...] = jnp.zeros(m_s.shape, F32)

    xcs, xconvs = [], []
    for b in range(nb):
        xc = xc_ref[b]
        halo = jnp.where(ci == 0, 0.0, halo_ref[b])
        xx = jnp.concatenate([halo, xc], axis=0)
        acc = cb_ref[...] + cw_ref[CONV_W - 1:CONV_W, :] * xc
        for k in range(1, CONV_W):
            acc = acc + cw_ref[CONV_W - 1 - k:CONV_W - k, :] * pltpu.roll(xx, k, 0)[8:]
        xcs.append(xc)
        xconvs.append(_silu(acc))
    xc_all = jnp.concatenate(xcs, axis=0)
    xconv_all = jnp.concatenate(xconvs, axis=0)
    tri = (lax.broadcasted_iota(jnp.int32, (L, L), 0) >= lax.broadcasted_iota(jnp.int32, (L, L), 1))
    one_col = (lax.broadcasted_iota(jnp.int32, (L, LANES), 1) == 0).astype(F32)
    ys = [[None] * C_HEADS for _ in range(nb)]
    for h in range(C_HEADS):
        sl = slice(h * hd, (h + 1) * hd)
        xh = xconv_all[:, sl]
        q_h = _mm(xh, wq_ref[h]) * np.float32(hd ** -0.5)
        kt_h = _mm_nt(wkt_ref[h], xh)
        v_h = _mm(xc_all[:, sl], wv_ref[h])
        for b in range(nb):
            r = slice(b * L, (b + 1) * L)
            gates = g_ref[b]
            hc, caug_new, m_new = _mlstm_head(
                q_h[r], kt_h[:, r], v_h[r], gates[h:h + 1, :], gates[C_HEADS + h:C_HEADS + h + 1, :],
                m_s[b, h][0:1, 0:1], caug_s[b, h], tri, one_col)
            caug_s[b, h] = caug_new
            m_s[b, h] = jnp.broadcast_to(m_new, (8, LANES))
            ys[b][h] = _mlstm_gate(hc, o_ref[b][:, sl], z_ref[b][:, sl], xconv_all[r, sl],
                                   ng_ref[:, sl], skip_ref[:, sl])
    y_all = jnp.concatenate([jnp.concatenate(ys[b], axis=1) for b in range(nb)], axis=0)
    hn = jnp.concatenate([h_ref[b] for b in range(nb)], axis=0) + _mm(y_all, wout_ref[...])
    if final:
        hn = hn * lax.rsqrt(jnp.mean(hn * hn, -1, keepdims=True) + EPS) * fg_ref[...]
    for b in range(nb):
        hout_ref[b] = hn[b * L:(b + 1) * L]

    @pl.when(ci == pl.num_programs(1) - 1)
    def _():
        for b in range(nb):
            for h in range(C_HEADS):
                c_out[0, b, h] = caug_s[b, h, :, :hd]
                n_out[b, h] = caug_s[b, h, :, hd:hd + 1]
                m_out[b, h:h + 1, :] = m_s[b, h][0:1, 0:1]


def _mlstm_prompt(y_in, gates, h, cw, cb, wq, wkt, wv, ng, skip, wout, final_g, c_stack, *, layer, n_layers, nb):
    batch, S = y_in.shape[0], y_in.shape[1]
    W, D = wout.shape
    assert batch % nb == 0 and S % C_CHUNK == 0
    hd = W // C_HEADS
    L = C_CHUNK
    const2 = lambda b, c: (0, 0)
    const3 = lambda b, c: (0, 0, 0)
    yblk = lambda k: pl.BlockSpec((nb, L, W), lambda b, c: (b, c, k))
    in_specs = [yblk(0),
                pl.BlockSpec((nb, 8, W), lambda b, c: (b, jnp.maximum(c * (L // 8) - 1, 0), 0)),
                yblk(1), yblk(2),
                pl.BlockSpec((nb, GATE_ROWS, L), lambda b, c: (b, 0, c)),
                pl.BlockSpec((nb, L, D), lambda b, c: (b, c, 0)),
                pl.BlockSpec(cw.shape, const2), pl.BlockSpec((1, W), const2),
                pl.BlockSpec(wq.shape, const3), pl.BlockSpec(wkt.shape, const3), pl.BlockSpec(wv.shape, const3),
                pl.BlockSpec((1, W), const2), pl.BlockSpec((1, W), const2), pl.BlockSpec(wout.shape, const2)]
    args = [y_in, y_in, y_in, y_in, gates, h, cw, cb.reshape(1, W), wq, wkt, wv,
            ng.reshape(1, W), skip.reshape(1, W), wout]
    if final_g is not None:
        in_specs.append(pl.BlockSpec((1, D), const2))
        args.append(final_g.reshape(1, D))
    aliases = {}
    if c_stack is not None:
        aliases[len(args)] = 1
        in_specs.append(pl.BlockSpec(memory_space=pl.ANY))
        args.append(c_stack)
    out_shape = [jax.ShapeDtypeStruct((batch, S, D), F32),
                 jax.ShapeDtypeStruct((n_layers, batch, C_HEADS, hd, hd), F32),
                 jax.ShapeDtypeStruct((batch, C_HEADS, hd, 1), F32),
                 jax.ShapeDtypeStruct((batch, C_HEADS, 1), F32)]
    out_specs = [pl.BlockSpec((nb, L, D), lambda b, c: (b, c, 0)),
                 pl.BlockSpec((1, nb, C_HEADS, hd, hd), lambda b, c: (layer, b, 0, 0, 0)),
                 pl.BlockSpec((nb, C_HEADS, hd, 1), lambda b, c: (b, 0, 0, 0)),
                 pl.BlockSpec((nb, C_HEADS, 1), lambda b, c: (b, 0, 0))]
    hout, c_new, n_new, m_new = pl.pallas_call(
        functools.partial(_mlstm_prompt_kernel, nb=nb, n_alias=len(aliases), final=final_g is not None),
        grid=(batch // nb, S // L), in_specs=in_specs, out_specs=out_specs, out_shape=out_shape,
        input_output_aliases=aliases,
        scratch_shapes=[pltpu.VMEM((nb, C_HEADS, hd, hd + LANES), F32), pltpu.VMEM((nb, C_HEADS, 8, LANES), F32)],
        compiler_params=_cparams("parallel", "arbitrary"), name="mlstm_layer",
    )(*args)
    return hout, c_new, n_new.reshape(batch, C_HEADS, hd), m_new.reshape(batch, C_HEADS)


def _out_proj_kernel(y_ref, w_ref, h_ref, *rest, final):
    hn = h_ref[...] + _mm(y_ref[...], w_ref[...])
    if final:
        fg_ref, o_ref = rest
        o_ref[...] = hn * lax.rsqrt(jnp.mean(hn * hn, -1, keepdims=True) + EPS) * fg_ref[...]
    else:
        rest[0][...] = hn


def _out_proj(y, w, h, final_g, *, tm):
    T, D = h.shape
    K = y.shape[1]
    const2 = lambda i: (0, 0)
    in_specs = [pl.BlockSpec((tm, K), lambda i: (i, 0)), pl.BlockSpec(w.shape, const2),
                pl.BlockSpec((tm, D), lambda i: (i, 0))]
    args = [y, w, h]
    if final_g is not None:
        in_specs.append(pl.BlockSpec((1, D), const2))
        args.append(final_g.reshape(1, D))
    return pl.pallas_call(
        functools.partial(_out_proj_kernel, final=final_g is not None),
        grid=(T // tm,), in_specs=in_specs, out_specs=pl.BlockSpec((tm, D), lambda i: (i, 0)),
        out_shape=jax.ShapeDtypeStruct((T, D), F32),
        compiler_params=_cparams("parallel"), name="out_proj",
    )(*args)


def kernel(x_prompt, x_sample, cache_k, cache_v, cache_logf, state_c, state_n, state_m, state_conv, page_table,
           norm_g, final_g, even_w_in, even_b_f, gmlp_ln_g, gmlp_ln_b, gmlp_w_s, gmlp_b_s, even_w_out,
           odd_w_in, odd_b_i, odd_b_f, conv_w, conv_b, mlstm_w_q, mlstm_w_k, mlstm_w_v, mlstm_norm_g,
           mlstm_skip, odd_w_out):
    B, S, D = x_prompt.shape
    DB = x_sample.shape[0]
    depth = norm_g.shape[0]
    AW = A_WIDTH
    BW = gmlp_ln_g.shape[1]
    CW = conv_w.shape[2]
    gdim = BW // B_GROUPS
    tm = 512
    assert AW == BW, "even-layer column blocks are addressed in units of one common width"

    hp = x_prompt.reshape(B * S, D)
    hs = x_sample.reshape(DB, D)
    cache_kt = jnp.transpose(cache_k, (0, 1, 3, 4, 2))
    cache_vt = jnp.transpose(cache_v, (0, 1, 3, 4, 2))
    cache_lft = jnp.transpose(cache_logf, (0, 1, 3, 2))

    lp, ks, vs, ls, chv = [], [], [], [], []
    np_, mp, bp, ns, ms, bs = [], [], [], [], [], []
    n_even, n_odd = (depth + 1) // 2, depth // 2
    assert depth % 2 == 0, "the final rmsnorm is fused into the last (mLSTM) layer's output projection"
    qkv_stack = cp_stack = cs_stack = None
    for l in range(depth):
        j = l // 2
        last = l == depth - 1
        if l % 2 == 0:
            w = even_w_in[j]
            o = 0
            parts = {}
            for name, n in (("q", AW), ("k", AW), ("v", AW), ("fg", A_HEADS), ("za", AW), ("u", BW), ("vb", BW), ("zb", BW)):
                parts[name] = w[:, o:o + n]
                o += n
            wnn_s = jnp.concatenate([parts[n] for n in ("za", "u", "vb", "zb", "k", "q", "v")], 1).astype(BF16)
            wqt = parts["q"].T.astype(BF16)
            wkt = parts["k"].T.astype(BF16)
            wvt = parts["v"].T.astype(BF16)
            wg = parts["fg"].T.astype(BF16)
            gb = even_b_f[j].reshape(GATE_ROWS, 1)
            wout = even_w_out[j].astype(BF16)
            bs_full = jnp.repeat(gmlp_b_s[j].T, gdim, axis=1)
            ws_dec = jnp.repeat(gmlp_w_s[j][:, 0, 0], gdim).reshape(1, BW)
            bs_dec = jnp.repeat(gmlp_b_s[j][:, 0], gdim).reshape(1, BW)

            ys, lf = _proj(hs, norm_g[l], wnn_s, [], wg, gb, n_lin=0, batch=1, seq=DB, tm=DB)
            lf_s = lf[0].T
            k_s, q_s, v_s = ys[:, 4 * AW:5 * AW], ys[:, 5 * AW:6 * AW], ys[:, 6 * AW:7 * AW]
            y, qt, kt_all, vt_all, lf = _proj(hp, norm_g[l], wnn_s, [wqt, wkt, wvt], wg, gb, n_lin=0, n_nn=5 * AW,
                                              batch=B, seq=S, tm=tm, layer=j, n_layers=n_even, stacked=qkv_stack)
            qkv_stack = [qt, kt_all, vt_all]
            c = _seq_cumsum(lf)

            att, att_s = _fox_attention(y, qt, vt_all, c, q_s, k_s, v_s, lf_s, cache_kt, cache_vt, cache_lft,
                                        page_table, k_col=4 * AW, layer=j, batch=B, seq=S, tile=512,
                                        group_pages=8)
            hp = _even_mix(att, y, hp, gmlp_ln_g[j], gmlp_ln_b[j], gmlp_w_s[j], bs_full, wout,
                           col0=0, tm=tm, decode=False)[0]
            lp.append(jnp.transpose(lf, (0, 2, 1)))

            hs, vn_s = _even_mix(att_s, ys, hs, gmlp_ln_g[j], gmlp_ln_b[j], ws_dec, bs_dec, wout,
                                 col0=0, tm=DB, decode=True)
            ks.append(k_s.reshape(DB, 1, A_HEADS, A_HEAD_DIM))
            vs.append(v_s.reshape(DB, 1, A_HEADS, A_HEAD_DIM))
            ls.append(lf_s.reshape(DB, 1, A_HEADS))
            chv.append(vn_s.reshape(DB, 1, BW))
        else:
            w = odd_w_in[j]
            wnn = w[:, :3 * CW].astype(BF16)
            wg = w[:, 3 * CW:].T.astype(BF16)
            gb = jnp.concatenate([odd_b_i[j], odd_b_f[j]]).reshape(GATE_ROWS, 1)
            wq = mlstm_w_q[j].astype(BF16)
            wk = mlstm_w_k[j].astype(BF16)
            wkT = jnp.swapaxes(mlstm_w_k[j], 1, 2).astype(BF16)
            wv = mlstm_w_v[j].astype(BF16)
            wout = odd_w_out[j].astype(BF16)
            fin = final_g if last else None

            y, g = _proj(hp, norm_g[l], wnn, [], wg, gb, n_lin=C_HEADS, batch=B, seq=S, tm=tm, cum_chunk=C_CHUNK)
            hout, cp_stack, n1, m1 = _mlstm_prompt(
                y.reshape(B, S, 3 * CW), g, hp.reshape(B, S, D), conv_w[j], conv_b[j], wq, wkT, wv,
                mlstm_norm_g[j], mlstm_skip[j], wout, fin, cp_stack, layer=j, n_layers=n_odd,
                nb=max(n for n in (4, 2, 1) if B % n == 0))
            hp = hout.reshape(B * S, D)
            np_.append(n1); mp.append(m1)
            bp.append(y.reshape(B, S, 3 * CW)[:, S - (CONV_W - 1):, :CW])

            y, g = _proj(hs, norm_g[l], wnn, [], wg, gb, n_lin=C_HEADS, batch=1, seq=DB, tm=DB)
            buf = jnp.transpose(state_conv[j], (1, 0, 2))
            xconv, q, k, v, nbuf = _qkv_decode(y, buf, conv_w[j], conv_b[j], wq, wk, wv)
            g_s = jnp.transpose(g, (2, 1, 0))
            r3 = lambda a: a.reshape(DB, 1, a.shape[1])
            ym, cs_stack, n2, m2 = _mlstm_decode(r3(q), r3(k), r3(v), g_s, r3(y), r3(xconv), mlstm_norm_g[j],
                                                 mlstm_skip[j], (state_c, state_n, state_m), cs_stack,
                                                 layer=j, n_layers=n_odd, nb=4)
            hs = _out_proj(ym.reshape(DB, CW), wout, hs, fin, tm=DB)
            ns.append(n2); ms.append(m2)
            bs.append(jnp.transpose(nbuf, (1, 0, 2)))
    y_prompt = hp.reshape(B, S, D)
    y_sample = hs.reshape(DB, 1, D)
    to_bshd = lambda t: jnp.transpose(t.reshape(n_even, B, A_HEADS, A_HEAD_DIM, S), (0, 1, 4, 2, 3))
    return (y_prompt, y_sample,
            to_bshd(qkv_stack[1]), to_bshd(qkv_stack[2]), jnp.stack(lp),
            jnp.stack(ks), jnp.stack(vs), jnp.stack(ls), jnp.stack(chv),
            cp_stack, jnp.stack(np_), jnp.stack(mp), jnp.stack(bp),
            cs_stack, jnp.stack(ns), jnp.stack(ms), jnp.stack(bs))
```
